```python
import functools
import jax
import jax.numpy as jnp
from jax import lax
import numpy as np

D_MODEL = 2048
BATCH = 8
SEQ = 2048
DEPTH = 1
DEC_BATCH = 32
DEC_SEQ = 4
PAST_LEN = 16384
PAGE_SIZE = 128

HEAD_DIM = 64
N_ATT_HEADS = 16
N_KV_HEADS = 4
KV_GROUP = N_ATT_HEADS // N_KV_HEADS
ATT_WIDTH = N_ATT_HEADS * HEAD_DIM
N_RWKV_HEADS = 16
RWKV_WIDTH = N_RWKV_HEADS * HEAD_DIM
MIX_WIDTH = ATT_WIDTH + RWKV_WIDTH
ROPE_DIM = HEAD_DIM // 4
ROPE_THETA = 500000.0
MOBA_BLOCK = 256
MOBA_TOPK = 3
Q_BLOCK = 64
D_DECAY_LORA = 64
D_AAA_LORA = 64
D_GATE_LORA = 160
D_FF = 5632
CONV_W = 3
LN_EPS = 1e-5
GN_EPS = 64e-5
NEG_INF = -1e30
DN_ALPHA = (2.0 * DEPTH) ** 0.25
DN_BETA = (8.0 * DEPTH) ** -0.25

K_OFF = ATT_WIDTH
V_OFF = K_OFF + N_KV_HEADS * HEAD_DIM
ATT_END = V_OFF + N_KV_HEADS * HEAD_DIM
RWKV_PROJ = 3 * RWKV_WIDTH + D_DECAY_LORA + D_AAA_LORA + D_GATE_LORA
IN_WIDTH = ATT_END + RWKV_PROJ

kernel_name = "hymba_moba_rwkv7_deepnorm_convffn_step"


def layer_norm(x, g, b):
    xf = x.astype(jnp.float32)
    mu = jnp.mean(xf, axis=-1, keepdims=True)
    var = jnp.mean(jnp.square(xf - mu), axis=-1, keepdims=True)
    return ((xf - mu) * lax.rsqrt(var + LN_EPS) * g + b).astype(x.dtype)


def rope_partial(x, pos):
    half = ROPE_DIM // 2
    inv = jnp.power(ROPE_THETA, -(jnp.arange(half, dtype=jnp.float32) * 2.0 / ROPE_DIM))
    ang = pos.astype(jnp.float32)[:, None] * inv[None, :]
    cos = jnp.cos(ang)[:, None, :]
    sin = jnp.sin(ang)[:, None, :]
    xf = x.astype(jnp.float32)
    x1 = xf[..., :half]
    x2 = xf[..., half:ROPE_DIM]
    out = jnp.concatenate([x1 * cos - x2 * sin, x2 * cos + x1 * sin, xf[..., ROPE_DIM:]], axis=-1)
    return out.astype(x.dtype)


def moba_prompt_seq(q, k, v):
    S = q.shape[0]
    n_blk = -(-S // MOBA_BLOCK)
    n_cand = n_blk - 1
    topk = min(MOBA_TOPK, n_cand)
    pad = n_blk * MOBA_BLOCK - S
    kb = jnp.pad(k, ((0, pad), (0, 0), (0, 0))).reshape(n_blk, MOBA_BLOCK, N_KV_HEADS, HEAD_DIM)
    vb = jnp.pad(v, ((0, pad), (0, 0), (0, 0))).reshape(n_blk, MOBA_BLOCK, N_KV_HEADS, HEAD_DIM)
    kb_h = jnp.transpose(kb, (2, 0, 1, 3))
    vb_h = jnp.transpose(vb, (2, 0, 1, 3))
    kmean = jnp.mean(kb[:n_cand].astype(jnp.float32), axis=1)
    scale = HEAD_DIM ** -0.5
    kv_idx = jnp.arange(N_KV_HEADS)[None, :, None, None]
    blk_pos = jnp.arange(MOBA_BLOCK)
    nsel = topk * MOBA_BLOCK

    def chunk(args):
        qc, ci = args
        q0 = ci * Q_BLOCK
        own = q0 // MOBA_BLOCK
        qg = qc.reshape(Q_BLOCK, N_KV_HEADS, KV_GROUP, HEAD_DIM)
        q_pos = q0 + jnp.arange(Q_BLOCK)
        k_own = lax.dynamic_index_in_dim(kb, own, axis=0, keepdims=False)
        v_own = lax.dynamic_index_in_dim(vb, own, axis=0, keepdims=False)
        s_own = jnp.einsum('tcgd,scd->tcgs', qg, k_own, preferred_element_type=jnp.float32) * scale
        causal = (own * MOBA_BLOCK + blk_pos)[None, :] <= q_pos[:, None]
        s_own = jnp.where(causal[:, None, None, :], s_own, NEG_INF)
        if topk == 0:
            p = jax.nn.softmax(s_own, axis=-1)
            o = jnp.einsum('tcgs,scd->tcgd', p, v_own.astype(jnp.float32))
        else:
            gate = jnp.einsum('tcgd,ncd->tcgn', qg.astype(jnp.float32), kmean)
            gate = jnp.where(jnp.arange(n_cand) < own, gate, NEG_INF)
            _, sel = lax.top_k(gate, topk)
            valid = sel < own
            k_sel = kb_h[kv_idx, sel]
            v_sel = vb_h[kv_idx, sel]
            s_sel = jnp.einsum('tcgd,tcgjsd->tcgjs', qg, k_sel, preferred_element_type=jnp.float32) * scale
            s_sel = jnp.where(valid[..., None], s_sel, NEG_INF).reshape(Q_BLOCK, N_KV_HEADS, KV_GROUP, nsel)
            p = jax.nn.softmax(jnp.concatenate([s_sel, s_own], axis=-1), axis=-1)
            p_sel = p[..., :nsel].reshape(Q_BLOCK, N_KV_HEADS, KV_GROUP, topk, MOBA_BLOCK)
            o = (jnp.einsum('tcgjs,tcgjsd->tcgd', p_sel, v_sel.astype(jnp.float32))
                 + jnp.einsum('tcgs,scd->tcgd', p[..., nsel:], v_own.astype(jnp.float32)))
        return o.reshape(Q_BLOCK, ATT_WIDTH).astype(q.dtype)

    qs = q.reshape(S // Q_BLOCK, Q_BLOCK, N_ATT_HEADS, HEAD_DIM)
    out = lax.map(chunk, (qs, jnp.arange(S // Q_BLOCK, dtype=jnp.int32)))
    return out.reshape(S, ATT_WIDTH)


def moba_prompt(q, k, v):
    return lax.map(lambda qkv: moba_prompt_seq(qkv[0], qkv[1], qkv[2]), (q, k, v))


def moba_decode(q, k_new, v_new, cache_k, cache_v, page_table):
    Bd, T = q.shape[0], q.shape[1]
    past = page_table.shape[1] * PAGE_SIZE
    n_cand = past // MOBA_BLOCK
    topk = min(MOBA_TOPK, n_cand)
    own_start = n_cand * MOBA_BLOCK
    own_len = past - own_start
    ppb = MOBA_BLOCK // PAGE_SIZE
    scale = HEAD_DIM ** -0.5
    qg = q.reshape(Bd, T, N_KV_HEADS, KV_GROUP, HEAD_DIM)
    q_pos = past + jnp.arange(T)
    own_pages = page_table[:, own_start // PAGE_SIZE:]
    k_own = jnp.concatenate([cache_k[own_pages].reshape(Bd, own_len, N_KV_HEADS, HEAD_DIM).astype(k_new.dtype), k_new], axis=1)
    v_own = jnp.concatenate([cache_v[own_pages].reshape(Bd, own_len, N_KV_HEADS, HEAD_DIM).astype(v_new.dtype), v_new], axis=1)
    k_pos = own_start + jnp.arange(own_len + T)
    s_own = jnp.einsum('btcgd,bscd->btcgs', qg, k_own, preferred_element_type=jnp.float32) * scale
    causal = k_pos[None, :] <= q_pos[:, None]
    s_own = jnp.where(causal[None, :, None, None, :], s_own, NEG_INF)
    if topk == 0:
        p = jax.nn.softmax(s_own, axis=-1)
        o = jnp.einsum('btcgs,bscd->btcgd', p, v_own.astype(jnp.float32))
    else:
        k_full = cache_k[page_table[:, :own_start // PAGE_SIZE]].reshape(Bd, n_cand, MOBA_BLOCK, N_KV_HEADS, HEAD_DIM)
        kmean = jnp.mean(k_full.astype(jnp.float32), axis=2)
        gate = jnp.einsum('btcgd,bncd->btcgn', qg.astype(jnp.float32), kmean)
        _, sel = lax.top_k(gate, topk)
        logical = sel[..., None] * ppb + jnp.arange(ppb)
        phys = page_table[jnp.arange(Bd)[:, None, None, None, None, None], logical]
        kvi = jnp.arange(N_KV_HEADS)[None, None, :, None, None, None, None]
        rows = jnp.arange(PAGE_SIZE)
        nsel = topk * MOBA_BLOCK
        k_sel = cache_k[phys[..., None], rows, kvi].reshape(Bd, T, N_KV_HEADS, KV_GROUP, nsel, HEAD_DIM)
        v_sel = cache_v[phys[..., None], rows, kvi].reshape(Bd, T, N_KV_HEADS, KV_GROUP, nsel, HEAD_DIM)
        s_sel = jnp.einsum('btcgd,btcgsd->btcgs', qg, k_sel.astype(q.dtype), preferred_element_type=jnp.float32) * scale
        p = jax.nn.softmax(jnp.concatenate([s_sel, s_own], axis=-1), axis=-1)
        o = (jnp.einsum('btcgs,btcgsd->btcgd', p[..., :nsel], v_sel.astype(jnp.float32))
             + jnp.einsum('btcgs,bscd->btcgd', p[..., nsel:], v_own.astype(jnp.float32)))
    return o.reshape(Bd, T, ATT_WIDTH).astype(q.dtype)


def rwkv7_mix(z, z_last, wkv0, shift_mu, w0, w2, a0, a2, g2, k_k, k_a, r_k, gn_g, gn_b):
    B, T = z.shape[0], z.shape[1]
    f32 = jnp.float32
    z_cat = jnp.concatenate([z_last[:, None, :].astype(z.dtype), z], axis=1)
    zs = z + (z_cat[:, :-1] - z) * shift_mu
    splits = (RWKV_WIDTH, 2 * RWKV_WIDTH, 3 * RWKV_WIDTH, 3 * RWKV_WIDTH + D_DECAY_LORA,
              3 * RWKV_WIDTH + D_DECAY_LORA + D_AAA_LORA)
    r, k, v, w_lo, a_lo, g_lo = jnp.split(zs, splits, axis=-1)
    w = -jax.nn.softplus(-(w0 + jnp.tanh(w_lo) @ w2).astype(f32)) - 0.5
    decay = jnp.exp(-jnp.exp(w))
    a = jax.nn.sigmoid((a0 + a_lo @ a2).astype(f32))
    g = jax.nn.sigmoid(g_lo) @ g2
    r = r.astype(f32)
    k = k.astype(f32)
    v = v.astype(f32)
    kk = k * k_k
    k = k * (1.0 + (a - 1.0) * k_a)
    heads = lambda t: t.reshape(B, T, N_RWKV_HEADS, HEAD_DIM)
    r, k, v, a, decay, kk = (heads(r), heads(k), heads(v), heads(a), heads(decay), heads(kk))
    kk = kk / jnp.maximum(jnp.sqrt(jnp.sum(kk * kk, axis=-1, keepdims=True)), 1e-12)

    def step(S, inp):
        r_t, w_t, k_t, v_t, kk_t, a_t = inp
        sa = jnp.einsum('bhij,bhj->bhi', S, -kk_t)
        S = (S * w_t[:, :, None, :] + sa[..., :, None] * (kk_t * a_t)[..., None, :]
             + v_t[..., :, None] * k_t[..., None, :])
        return S, jnp.einsum('bhij,bhj->bhi', S, r_t)

    xs = tuple(jnp.swapaxes(t, 0, 1) for t in (r, decay, k, v, kk, a))
    S_T, o = lax.scan(step, wkv0.astype(f32), xs)
    o = jnp.swapaxes(o, 0, 1)
    mu = jnp.mean(o, axis=-1, keepdims=True)
    var = jnp.mean(jnp.square(o - mu), axis=-1, keepdims=True)
    on = ((o - mu) * lax.rsqrt(var + GN_EPS)).reshape(B, T, RWKV_WIDTH) * gn_g + gn_b
    bonus = jnp.sum(r * k * r_k.reshape(N_RWKV_HEADS, HEAD_DIM), axis=-1, keepdims=True) * v
    out = (on + bonus.reshape(B, T, RWKV_WIDTH)) * g
    return out.astype(z.dtype), z[:, -1], S_T.astype(wkv0.dtype)


def conv_ffn(x, hist, w_up, conv_w, conv_b, w_down):
    u = x @ w_up
    gp, up = u[..., :D_FF], u[..., D_FF:]
    zc = jnp.concatenate([hist.astype(gp.dtype), gp], axis=1)
    conv = lax.conv_general_dilated(zc, conv_w[:, None, :].astype(zc.dtype), window_strides=(1,), padding='VALID',
                                    dimension_numbers=('NWC', 'WIO', 'NWC'), feature_group_count=D_FF) + conv_b
    h = jax.nn.silu(conv) * up
    return h @ w_down, zc[:, -(CONV_W - 1):]


def trunk_layer(x, pos, attend, shift0, wkv0, conv0, lw):
    (w_in, w_out, shift_mu, w0, w2, a0, a2, g2, k_k, k_a, r_k, gn_g, gn_b,
     ln1_g, ln1_b, ln2_g, ln2_b, w_up, conv_w, conv_b, w_down) = lw
    B, T = x.shape[0], x.shape[1]
    proj = x @ w_in
    q = rope_partial(proj[..., :K_OFF].reshape(B, T, N_ATT_HEADS, HEAD_DIM), pos)
    k = rope_partial(proj[..., K_OFF:V_OFF].reshape(B, T, N_KV_HEADS, HEAD_DIM), pos)
    v = proj[..., V_OFF:ATT_END].reshape(B, T, N_KV_HEADS, HEAD_DIM)
    att = attend(q, k, v)
    rw, shift_new, wkv_new = rwkv7_mix(proj[..., ATT_END:], shift0, wkv0, shift_mu, w0, w2, a0, a2, g2,
                                       k_k, k_a, r_k, gn_g, gn_b)
    mix = jnp.concatenate([att, rw], axis=-1) @ w_out
    x1 = layer_norm(DN_ALPHA * x + mix, ln1_g, ln1_b)
    ffn, conv_new = conv_ffn(x1, conv0, w_up, conv_w, conv_b, w_down)
    y = layer_norm(DN_ALPHA * x1 + ffn, ln2_g, ln2_b)
    return y, k, v, wkv_new, shift_new, conv_new


def setup_inputs(seed: int = 0) -> dict:
    key = jax.random.key(seed)
    ks = jax.random.split(key, 32)
    f32 = jnp.float32
    nrm = lambda k, shape, s: jax.random.normal(k, shape, f32) * s
    n_pages = PAST_LEN // PAGE_SIZE
    n_used = DEC_BATCH * n_pages
    n_phys = n_used + n_used // 4
    col_scale = np.ones((IN_WIDTH,), np.float32)
    col_scale[V_OFF:ATT_END] = DN_BETA
    col_scale[ATT_END + 2 * RWKV_WIDTH:ATT_END + 3 * RWKV_WIDTH] = DN_BETA
    return {
        "x_prompt": nrm(ks[0], (BATCH, SEQ, D_MODEL), 1.0),
        "x_sample": nrm(ks[1], (DEC_BATCH, DEC_SEQ, D_MODEL), 1.0),
        "cache_k": nrm(ks[2], (DEPTH, n_phys, PAGE_SIZE, N_KV_HEADS, HEAD_DIM), 1.0),
        "cache_v": nrm(ks[3], (DEPTH, n_phys, PAGE_SIZE, N_KV_HEADS, HEAD_DIM), 1.0),
        "page_table": jax.random.permutation(ks[4], n_phys)[:n_used].reshape(DEC_BATCH, n_pages).astype(jnp.int32),
        "state_wkv": nrm(ks[5], (DEPTH, DEC_BATCH, N_RWKV_HEADS, HEAD_DIM, HEAD_DIM), 0.3),
        "state_shift": nrm(ks[6], (DEPTH, DEC_BATCH, RWKV_PROJ), 1.0),
        "state_conv": nrm(ks[7], (DEPTH, DEC_BATCH, CONV_W - 1, D_FF), 1.0),
        "w_in": nrm(ks[8], (DEPTH, D_MODEL, IN_WIDTH), D_MODEL ** -0.5) * jnp.asarray(col_scale),
        "w_out": nrm(ks[9], (DEPTH, MIX_WIDTH, D_MODEL), MIX_WIDTH ** -0.5 * DN_BETA),
        "shift_mu": jax.random.uniform(ks[10], (DEPTH, RWKV_PROJ), f32, 0.0, 1.0),
        "w0": jax.random.uniform(ks[11], (DEPTH, RWKV_WIDTH), f32, -6.0, -1.0),
        "w2": nrm(ks[12], (DEPTH, D_DECAY_LORA, RWKV_WIDTH), 0.1 * D_DECAY_LORA ** -0.5),
        "a0": nrm(ks[13], (DEPTH, RWKV_WIDTH), 0.1),
        "a2": nrm(ks[14], (DEPTH, D_AAA_LORA, RWKV_WIDTH), D_AAA_LORA ** -0.5),
        "g2": nrm(ks[15], (DEPTH, D_GATE_LORA, RWKV_WIDTH), D_GATE_LORA ** -0.5),
        "k_k": 0.85 + nrm(ks[16], (DEPTH, RWKV_WIDTH), 0.05),
        "k_a": 1.0 + nrm(ks[17], (DEPTH, RWKV_WIDTH), 0.05),
        "r_k": nrm(ks[18], (DEPTH, RWKV_WIDTH), 0.1),
        "gn_g": 1.0 + nrm(ks[19], (DEPTH, RWKV_WIDTH), 0.05),
        "gn_b": nrm(ks[20], (DEPTH, RWKV_WIDTH), 0.01),
        "ln1_g": 1.0 + nrm(ks[21], (DEPTH, D_MODEL), 0.05),
        "ln1_b": nrm(ks[22], (DEPTH, D_MODEL), 0.01),
        "ln2_g": 1.0 + nrm(ks[23], (DEPTH, D_MODEL), 0.05),
        "ln2_b": nrm(ks[24], (DEPTH, D_MODEL), 0.01),
        "w_up": nrm(ks[25], (DEPTH, D_MODEL, 2 * D_FF), D_MODEL ** -0.5),
        "conv_w": nrm(ks[26], (DEPTH, CONV_W, D_FF), CONV_W ** -0.5),
        "conv_b": nrm(ks[27], (DEPTH, D_FF), 0.01),
        "w_down": nrm(ks[28], (DEPTH, D_FF, D_MODEL), D_FF ** -0.5 * DN_BETA),
    }


def reference(x_prompt, x_sample, cache_k, cache_v, page_table, state_wkv, state_shift, state_conv,
              w_in, w_out, shift_mu, w0, w2, a0, a2, g2, k_k, k_a, r_k, gn_g, gn_b,
              ln1_g, ln1_b, ln2_g, ln2_b, w_up, conv_w, conv_b, w_down):
    B, S = x_prompt.shape[0], x_prompt.shape[1]
    T = x_sample.shape[1]
    past = page_table.shape[1] * PAGE_SIZE
    pos_p = jnp.arange(S, dtype=jnp.int32)
    pos_s = past + jnp.arange(T, dtype=jnp.int32)
    yp, ys = x_prompt, x_sample
    kp, vp, wp, zp, cp = [], [], [], [], []
    kq, vq, wq, zq, cq = [], [], [], [], []
    for l in range(DEPTH):
        lw = (w_in[l], w_out[l], shift_mu[l], w0[l], w2[l], a0[l], a2[l], g2[l], k_k[l], k_a[l], r_k[l],
              gn_g[l], gn_b[l], ln1_g[l], ln1_b[l], ln2_g[l], ln2_b[l], w_up[l], conv_w[l], conv_b[l], w_down[l])
        yp, k_l, v_l, wkv_l, z_l, c_l = trunk_layer(
            yp, pos_p, moba_prompt,
            jnp.zeros((B, RWKV_PROJ), x_prompt.dtype),
            jnp.zeros((B, N_RWKV_HEADS, HEAD_DIM, HEAD_DIM), state_wkv.dtype),
            jnp.zeros((B, CONV_W - 1, D_FF), x_prompt.dtype), lw)
        kp.append(k_l); vp.append(v_l); wp.append(wkv_l); zp.append(z_l); cp.append(c_l)
        sample_attend = functools.partial(moba_decode, cache_k=cache_k[l], cache_v=cache_v[l], page_table=page_table)
        ys, k_l, v_l, wkv_l, z_l, c_l = trunk_layer(
            ys, pos_s, sample_attend, state_shift[l], state_wkv[l], state_conv[l], lw)
        kq.append(k_l); vq.append(v_l); wq.append(wkv_l); zq.append(z_l); cq.append(c_l)
    k_prompt = jnp.stack(kp)
    v_prompt = jnp.stack(vp)
    wkv_prompt = jnp.stack(wp)
    shift_prompt = jnp.stack(zp)
    conv_prompt = jnp.stack(cp)
    k_sample = jnp.stack(kq)
    v_sample = jnp.stack(vq)
    wkv_sample = jnp.stack(wq)
    shift_sample = jnp.stack(zq)
    conv_sample = jnp.stack(cq)
    return (yp, ys, k_prompt, v_prompt, wkv_prompt, shift_prompt, conv_prompt,
            k_sample, v_sample, wkv_sample, shift_sample, conv_sample)
```

```python
import functools

import jax
import jax.numpy as jnp
from jax import lax
from jax.experimental import pallas as pl
from jax.experimental.pallas import tpu as pltpu

F32 = jnp.float32
BF16 = jnp.bfloat16

HEAD_DIM = 64
N_ATT_HEADS = 16
N_KV_HEADS = 4
KV_GROUP = N_ATT_HEADS // N_KV_HEADS
ATT_WIDTH = N_ATT_HEADS * HEAD_DIM
KV_WIDTH = N_KV_HEADS * HEAD_DIM
ATT_END = ATT_WIDTH + 2 * KV_WIDTH
N_RWKV_HEADS = 16
RWKV_WIDTH = N_RWKV_HEADS * HEAD_DIM
ROPE_DIM = HEAD_DIM // 4
ROPE_THETA = 500000.0
MOBA_BLOCK = 256
MOBA_TOPK = 3
D_DECAY_LORA = 64
D_AAA_LORA = 64
D_GATE_LORA = 160
LORA_WIDTH = D_DECAY_LORA + D_AAA_LORA + D_GATE_LORA
RWKV_PROJ = 3 * RWKV_WIDTH + LORA_WIDTH
CONV_W = 3
LN_EPS = 1e-5
GN_EPS = 64e-5
NEG_INF = -1e30
PAGE_SIZE = 128

LANES = 128
SUBLANES = 8
VMEM_LIMIT_BYTES = 56 * 1024 * 1024

PROJ_TN = 512
Z_PAD = 3584
Q_COL = Z_PAD
K_COL = Q_COL + ATT_WIDTH
V_COL = K_COL + KV_WIDTH
PROJ_W = V_COL + KV_WIDTH
LORA_COL = 3 * RWKV_WIDTH
LORA_BLOCK = 384
GATE_LORA_PAD = 256

RWKV_CHUNK = HEAD_DIM
PAGES_PER_STEP = 16

_NN = (((1,), (0,)), ((), ()))
_NT = (((1,), (1,)), ((), ()))


def _dot(a, b, dims=_NN):
    return lax.dot_general(a, b, dims, preferred_element_type=F32)


def _split_bf16(x):
    hi = x.astype(BF16)
    lo = (x - hi.astype(F32)).astype(BF16)
    return hi, lo


def _dot3(a, b, dims=_NN):
    ah, al = _split_bf16(a)
    bh, bl = _split_bf16(b)
    return _dot(ah, bh, dims) + (_dot(ah, bl, dims) + _dot(al, bh, dims))


def _sigmoid(x):
    return 1.0 / (1.0 + jnp.exp(-x))


def _layer_norm(x, g, b):
    mu = jnp.mean(x, axis=-1, keepdims=True)
    d = x - mu
    var = jnp.mean(d * d, axis=-1, keepdims=True)
    return d * lax.rsqrt(var + LN_EPS) * g + b


def _params(sem):
    return pltpu.CompilerParams(dimension_semantics=sem, vmem_limit_bytes=VMEM_LIMIT_BYTES)


def _rope_tables(pos):
    half = ROPE_DIM // 2
    inv = jnp.power(ROPE_THETA, -(jnp.arange(half, dtype=F32) * 2.0 / ROPE_DIM))
    ang = pos.astype(F32)[:, None] * inv[None, :]
    cos = jnp.cos(ang)
    sin = jnp.sin(ang)
    t = pos.shape[0]
    one = jnp.ones((t, HEAD_DIM - ROPE_DIM), F32)
    z8 = jnp.zeros((t, half), F32)
    zr = jnp.zeros((t, HEAD_DIM - ROPE_DIM), F32)
    c64 = jnp.concatenate([cos, cos, one], axis=1)
    a64 = jnp.concatenate([-sin, z8, zr], axis=1)
    b64 = jnp.concatenate([z8, sin, zr], axis=1)
    rep = LANES // HEAD_DIM
    return tuple(jnp.tile(m, (1, rep)) for m in (c64, a64, b64))


def _in_proj_kernel(x_ref, w_ref, cos_ref, sa_ref, sb_ref, proj_ref, *rest, attn_extras):
    if attn_extras:
        kh_ref, vt_ref, xb_ref = rest
    else:
        (xb_ref,) = rest
    j = pl.program_id(2)
    n_z = Z_PAD // PROJ_TN
    n_q = ATT_WIDTH // PROJ_TN
    half = ROPE_DIM // 2

    @pl.when(j == 0)
    def _():
        xb_ref[...] = x_ref[0].astype(BF16)

    acc = _dot(xb_ref[...], w_ref[...])

    def rope(slab):
        return (slab * cos_ref[...] + pltpu.roll(slab, LANES - half, 1) * sa_ref[...]
                + pltpu.roll(slab, half, 1) * sb_ref[...])

    @pl.when(j < n_z)
    def _():
        proj_ref[0] = acc

    @pl.when((j >= n_z) & (j < n_z + n_q))
    def _():
        for s in range(PROJ_TN // LANES):
            proj_ref[0, :, s * LANES:(s + 1) * LANES] = rope(acc[:, s * LANES:(s + 1) * LANES])

    @pl.when(j == n_z + n_q)
    def _():
        k_slabs = []
        for s in range(KV_WIDTH // LANES):
            ks = rope(acc[:, s * LANES:(s + 1) * LANES])
            proj_ref[0, :, s * LANES:(s + 1) * LANES] = ks
            k_slabs.append(ks)
        v = acc[:, KV_WIDTH:]
        proj_ref[0, :, KV_WIDTH:] = v
        if attn_extras:
            k = jnp.concatenate(k_slabs, axis=1)
            for c in range(N_KV_HEADS):
                kh_ref[0, c] = k[:, c * HEAD_DIM:(c + 1) * HEAD_DIM].astype(BF16)
            vt_ref[0] = v.T.astype(BF16)


def _in_proj(x, w_b, pos, *, tm, attn_extras):
    b, s, d = x.shape
    assert s % tm == 0 and PROJ_W % PROJ_TN == 0
    cos, sa, sb = _rope_tables(pos)
    tab_spec = pl.BlockSpec((tm, LANES), lambda bi, i, j: (i, 0))
    out_shape = [jax.ShapeDtypeStruct((b, s, PROJ_W), F32)]
    out_specs = [pl.BlockSpec((1, tm, PROJ_TN), lambda bi, i, j: (bi, i, j))]
    if attn_extras:
        out_shape += [jax.ShapeDtypeStruct((b, N_KV_HEADS, s, HEAD_DIM), BF16),
                      jax.ShapeDtypeStruct((b, KV_WIDTH, s), BF16)]
        out_specs += [pl.BlockSpec((1, N_KV_HEADS, tm, HEAD_DIM), lambda bi, i, j: (bi, 0, i, 0)),
                      pl.BlockSpec((1, KV_WIDTH, tm), lambda bi, i, j: (bi, 0, i))]
    return pl.pallas_call(
        functools.partial(_in_proj_kernel, attn_extras=attn_extras),
        grid=(b, s // tm, PROJ_W // PROJ_TN),
        in_specs=[pl.BlockSpec((1, tm, d), lambda bi, i, j: (bi, i, 0)),
                  pl.BlockSpec((d, PROJ_TN), lambda bi, i, j: (0, j)),
                  tab_spec, tab_spec, tab_spec],
        out_specs=out_specs,
        out_shape=out_shape,
        scratch_shapes=[pltpu.VMEM((tm, d), BF16)],
        compiler_params=_params(("parallel", "parallel", "arbitrary")),
        name="in_proj",
    )(x, w_b, cos, sa, sb)


def _moba_prompt_kernel(q_ref, kh_ref, vt_ref, o_ref, kmean_ref, sel_ref, *, n_blk, topk):
    i = pl.program_id(2)
    blk = MOBA_BLOCK
    n_cand = n_blk - 1
    scale = HEAD_DIM ** -0.5

    @pl.when(i == 0)
    def _():
        kmean_ref[...] = jnp.zeros_like(kmean_ref)
        for n in range(n_blk):
            kmean_ref[n:n + 1, :] = jnp.mean(
                kh_ref[0, 0, n * blk:(n + 1) * blk, :].astype(F32), axis=0, keepdims=True)

    q_t = q_ref[0].T
    qs_t = jnp.concatenate([q_t[g * HEAD_DIM:(g + 1) * HEAD_DIM, :] for g in range(KV_GROUP)],
                           axis=1)

    gate = _dot3(kmean_ref[...], qs_t)
    n_idx = lax.broadcasted_iota(jnp.int32, gate.shape, 0)
    gate = jnp.where(n_idx < i, gate, NEG_INF)
    rank = jnp.zeros(gate.shape, F32)
    for m in range(n_cand):
        gm = gate[m:m + 1, :]
        beats = (gm > gate) | ((gm == gate) & (m < n_idx))
        rank = rank + beats.astype(F32)
    sel_ref[...] = ((rank < topk) & (n_idx < i)).astype(F32)

    qb = (qs_t * scale).astype(BF16)

    row0 = pl.multiple_of(i * blk, blk)
    s_t = _dot(kh_ref[0, 0, pl.ds(row0, blk), :], qb)
    kpos = lax.broadcasted_iota(jnp.int32, s_t.shape, 0)
    qpos = lax.broadcasted_iota(jnp.int32, s_t.shape, 1) & (blk - 1)
    s_t = jnp.where(kpos <= qpos, s_t, NEG_INF)
    m0 = jnp.max(s_t, axis=0, keepdims=True)
    p = jnp.exp(s_t - m0)
    l0 = jnp.sum(p, axis=0, keepdims=True)
    acc0 = _dot(vt_ref[0, :, pl.ds(row0, blk)], p.astype(BF16))

    def past_block(n, carry):
        m, l, acc = carry
        r0 = pl.multiple_of(n * blk, blk)
        s_n = _dot(kh_ref[0, 0, pl.ds(r0, blk), :], qb)
        s_n = jnp.where(sel_ref[pl.ds(n, 1), :] > 0.5, s_n, NEG_INF)
        m_new = jnp.maximum(m, jnp.max(s_n, axis=0, keepdims=True))
        alpha = jnp.exp(m - m_new)
        p_n = jnp.exp(s_n - m_new)
        l = l * alpha + jnp.sum(p_n, axis=0, keepdims=True)
        acc = acc * alpha + _dot(vt_ref[0, :, pl.ds(r0, blk)], p_n.astype(BF16))
        return m_new, l, acc

    _, l, acc = lax.fori_loop(0, i, past_block, (m0, l0, acc0))
    o = acc / l
    o4 = jnp.concatenate([o[:, g * blk:(g + 1) * blk] for g in range(KV_GROUP)], axis=0)
    o_ref[0] = o4.T.astype(o_ref.dtype)


def _moba_prompt(proj, kh, vt):
    b, s, _ = proj.shape
    assert s % MOBA_BLOCK == 0
    n_blk = s // MOBA_BLOCK
    topk = min(MOBA_TOPK, n_blk - 1)
    gw = KV_GROUP * HEAD_DIM
    n_rows = -(-n_blk // SUBLANES) * SUBLANES
    return pl.pallas_call(
        functools.partial(_moba_prompt_kernel, n_blk=n_blk, topk=topk),
        grid=(b, N_KV_HEADS, n_blk),
        in_specs=[pl.BlockSpec((1, MOBA_BLOCK, gw), lambda bi, c, i: (bi, i, Q_COL // gw + c)),
                  pl.BlockSpec((1, 1, s, HEAD_DIM), lambda bi, c, i: (bi, c, 0, 0)),
                  pl.BlockSpec((1, HEAD_DIM, s), lambda bi, c, i: (bi, c, 0))],
        out_specs=pl.BlockSpec((1, MOBA_BLOCK, gw), lambda bi, c, i: (bi, i, c)),
        out_shape=jax.ShapeDtypeStruct((b, s, ATT_WIDTH), BF16),
        scratch_shapes=[pltpu.VMEM((n_rows, HEAD_DIM), F32),
                        pltpu.VMEM((n_rows, KV_GROUP * MOBA_BLOCK), F32)],
        compiler_params=_params(("parallel", "parallel", "arbitrary")),
        name="moba_prompt",
    )(proj, kh, vt)


def _moba_decode_kernel(pt_ref, wt_ref, kvn_ref, *rest, n_steps, n_cand, n_new):
    del pt_ref
    r_pages = PAGES_PER_STEP
    k_refs = rest[:r_pages]
    v_refs = rest[r_pages:2 * r_pages]
    o_ref = rest[2 * r_pages]
    s_ref, p_ref, kmean_ref, acc_ref, l_ref = rest[2 * r_pages + 1:]
    j = pl.program_id(1)
    scale = HEAD_DIM ** -0.5
    nq = N_ATT_HEADS * n_new
    step_keys = r_pages * PAGE_SIZE
    pages_per_blk = MOBA_BLOCK // PAGE_SIZE
    blks_per_step = r_pages // pages_per_blk
    wt = wt_ref[0]
    wt_b = (wt * scale).astype(BF16)

    @pl.when(j == 0)
    def _():
        kmean_ref[...] = jnp.zeros_like(kmean_ref)

    @pl.when(j < n_steps)
    def _():
        base = pl.multiple_of(j * step_keys, step_keys)
        for r in range(r_pages):
            kp = k_refs[r][0]
            s_ref[:, pl.ds(base + r * PAGE_SIZE, PAGE_SIZE)] = _dot(wt_b, kp.astype(BF16), _NT)
            if r % pages_per_blk == 0:
                ksum = jnp.sum(kp, axis=0, keepdims=True)
            else:
                ksum = ksum + jnp.sum(kp, axis=0, keepdims=True)
            if r % pages_per_blk == pages_per_blk - 1:
                kmean_ref[pl.ds(j * blks_per_step + r // pages_per_blk, 1), :] = ksum * (1.0 / MOBA_BLOCK)

    @pl.when(j == n_steps)
    def _():
        gate = _dot3(wt, kmean_ref[...], _NT)
        n_idx = lax.broadcasted_iota(jnp.int32, gate.shape, 1)
        gate = jnp.where(n_idx < n_cand, gate, NEG_INF)
        rank = jnp.zeros(gate.shape, F32)
        for m in range(n_cand):
            gm = gate[:, m:m + 1]
            beats = (gm > gate) | ((gm == gate) & (m < n_idx))
            rank = rank + beats.astype(F32)
        sel = (rank < min(MOBA_TOPK, n_cand)) & (n_idx < n_cand)

        kvn = kvn_ref[0]
        k_new = kvn[:, :KV_WIDTH].astype(BF16)
        v_new = kvn[:, KV_WIDTH:].astype(BF16)
        s_new = _dot(wt_b, k_new, _NT)
        t_q = lax.broadcasted_iota(jnp.int32, s_new.shape, 0) % n_new
        t_k = lax.broadcasted_iota(jnp.int32, s_new.shape, 1)
        ok_new = (t_k <= t_q) & (t_k < n_new)
        s_new = jnp.where(ok_new, s_new, NEG_INF)
        m = jnp.max(s_new, axis=1, keepdims=True)
        for n in range(n_cand):
            sb = s_ref[:, n * MOBA_BLOCK:(n + 1) * MOBA_BLOCK]
            sb = jnp.where(sel[:, n:n + 1], sb, NEG_INF)
            m = jnp.maximum(m, jnp.max(sb, axis=1, keepdims=True))
        p_new = jnp.where(ok_new, jnp.exp(s_new - m), 0.0)
        l = jnp.sum(p_new, axis=1, keepdims=True)
        for n in range(n_cand):
            sb = s_ref[:, n * MOBA_BLOCK:(n + 1) * MOBA_BLOCK]
            pb = jnp.where(sel[:, n:n + 1], jnp.exp(sb - m), 0.0)
            l = l + jnp.sum(pb, axis=1, keepdims=True)
            p_ref[:, n * MOBA_BLOCK:(n + 1) * MOBA_BLOCK] = pb.astype(BF16)
        l_ref[...] = jnp.broadcast_to(l, l_ref.shape)
        acc_ref[...] = _dot(p_new.astype(BF16), v_new)

    @pl.when(j >= n_steps)
    def _():
        base = pl.multiple_of((j - n_steps) * step_keys, step_keys)
        acc = acc_ref[...]
        for r in range(r_pages):
            acc = acc + _dot(p_ref[:, pl.ds(base + r * PAGE_SIZE, PAGE_SIZE)], v_refs[r][0].astype(BF16))
        acc_ref[...] = acc

    @pl.when(j == 2 * n_steps - 1)
    def _():
        o_ref[0] = acc_ref[...] / l_ref[:, 0:1]


def _moba_decode(proj_d, cache_k, cache_v, page_table, n_new):
    bd = proj_d.shape[0]
    n_pages = page_table.shape[1]
    past = n_pages * PAGE_SIZE
    assert past % MOBA_BLOCK == 0 and n_pages % PAGES_PER_STEP == 0 and n_new <= SUBLANES
    n_cand = past // MOBA_BLOCK
    n_steps = n_pages // PAGES_PER_STEP
    n_phys = cache_k.shape[0]
    nq = N_ATT_HEADS * n_new
    ck = cache_k.reshape(n_phys, PAGE_SIZE, KV_WIDTH)
    cv = cache_v.reshape(n_phys, PAGE_SIZE, KV_WIDTH)

    q = proj_d[:, :, Q_COL:Q_COL + ATT_WIDTH].reshape(bd, n_new, N_KV_HEADS, KV_GROUP, HEAD_DIM)
    q = jnp.transpose(q, (0, 2, 3, 1, 4)).reshape(bd, N_KV_HEADS, KV_GROUP * n_new, 1, HEAD_DIM)
    eye = jnp.eye(N_KV_HEADS, dtype=F32)[None, :, None, :, None]
    wt = (q * eye).reshape(bd, nq, KV_WIDTH)
    kvn = jnp.pad(proj_d[:, :, K_COL:], ((0, 0), (0, SUBLANES - n_new), (0, 0)))

    def k_map(r):
        return lambda b, j, pt: (pt[b, jnp.minimum(j, n_steps - 1) * PAGES_PER_STEP + r], 0, 0)

    def v_map(r):
        return lambda b, j, pt: (pt[b, jnp.maximum(j - n_steps, 0) * PAGES_PER_STEP + r], 0, 0)

    page_block = (1, PAGE_SIZE, KV_WIDTH)
    in_specs = ([pl.BlockSpec((1, nq, KV_WIDTH), lambda b, j, pt: (b, 0, 0)),
                 pl.BlockSpec((1, SUBLANES, 2 * KV_WIDTH), lambda b, j, pt: (b, 0, 0))]
                + [pl.BlockSpec(page_block, k_map(r)) for r in range(PAGES_PER_STEP)]
                + [pl.BlockSpec(page_block, v_map(r)) for r in range(PAGES_PER_STEP)])
    n_cand_pad = -(-n_cand // LANES) * LANES
    out = pl.pallas_call(
        functools.partial(_moba_decode_kernel, n_steps=n_steps, n_cand=n_cand, n_new=n_new),
        grid_spec=pltpu.PrefetchScalarGridSpec(
            num_scalar_prefetch=1,
            grid=(bd, 2 * n_steps),
            in_specs=in_specs,
            out_specs=pl.BlockSpec((1, nq, KV_WIDTH), lambda b, j, pt: (b, 0, 0)),
            scratch_shapes=[pltpu.VMEM((nq, past), F32),
                            pltpu.VMEM((nq, past), BF16),
                            pltpu.VMEM((n_cand_pad, KV_WIDTH), F32),
                            pltpu.VMEM((nq, KV_WIDTH), F32),
                            pltpu.VMEM((nq, LANES), F32)]),
        out_shape=jax.ShapeDtypeStruct((bd, nq, KV_WIDTH), F32),
        compiler_params=_params(("parallel", "arbitrary")),
        name="moba_decode",
    )(page_table, wt, kvn, *([ck] * PAGES_PER_STEP), *([cv] * PAGES_PER_STEP))
    o = out.reshape(bd, N_KV_HEADS, KV_GROUP, n_new, N_KV_HEADS, HEAD_DIM)
    o = jnp.stack([o[:, c, :, :, c, :] for c in range(N_KV_HEADS)], axis=1)
    o = jnp.transpose(o, (0, 3, 1, 2, 4)).reshape(bd, n_new, ATT_WIDTH)
    return o.astype(BF16)


def _rwkv_kernel(zr_ref, zk_ref, zv_ref, zl_ref, sr_ref, sk_ref, sv_ref, sl_ref,
                 mu_ref, mul_ref, par_ref, w2_ref, a2_ref, g2_ref, s0_ref,
                 rw_ref, wkv_ref,
                 state_ref, pr_ref, pk_ref, pv_ref, plo_ref, *, t_real):
    tb = pl.program_id(2)
    n_tb = pl.num_programs(2)
    tblk = zr_ref.shape[1]
    c_len = RWKV_CHUNK
    width = LANES
    lane = lax.broadcasted_iota(jnp.int32, (1, width), 1)
    head_a = lane < HEAD_DIM

    @pl.when(tb == 0)
    def _():
        z = jnp.zeros((HEAD_DIM, HEAD_DIM), F32)
        state_ref[...] = jnp.concatenate(
            [jnp.concatenate([s0_ref[0, 0], z], axis=1),
             jnp.concatenate([z, s0_ref[0, 1]], axis=1)], axis=0)
        pr_ref[0:1, :] = sr_ref[0]
        pk_ref[0:1, :] = sk_ref[0]
        pv_ref[0:1, :] = sv_ref[0]
        plo_ref[0:1, :] = sl_ref[0]

    row = lax.broadcasted_iota(jnp.int32, (tblk, 1), 0)

    def token_shift(z, prev_ref, mu):
        z_prev = jnp.where(row == 0, prev_ref[0:1, :], pltpu.roll(z, 1, 0))
        prev_ref[0:1, :] = z[tblk - 1:tblk, :]
        return z + (z_prev - z) * mu

    def seg_sum(x):
        sa = jnp.sum(jnp.where(head_a, x, 0.0), axis=1, keepdims=True)
        sb = jnp.sum(jnp.where(head_a, 0.0, x), axis=1, keepdims=True)
        return jnp.where(head_a, sa, sb)

    r = token_shift(zr_ref[0], pr_ref, mu_ref[0:1, :])
    k = token_shift(zk_ref[0], pk_ref, mu_ref[1:2, :])
    v = token_shift(zv_ref[0], pv_ref, mu_ref[2:3, :])
    lo = token_shift(zl_ref[0], plo_ref, mul_ref[...])
    par = par_ref[...]
    w0, a0, k_k, k_a, r_k, gn_g, gn_b = (par[n:n + 1, :] for n in range(7))

    lo_wa = lo[:, :LANES]
    nx = -(w0 + _dot(jnp.tanh(lo_wa).astype(BF16), w2_ref[...]))
    softplus = jnp.maximum(nx, 0.0) + jnp.log(1.0 + jnp.exp(-jnp.abs(nx)))
    logw = -jnp.exp(-softplus - 0.5)
    a = _sigmoid(a0 + _dot(lo_wa.astype(BF16), a2_ref[...]))
    g = _dot(_sigmoid(lo[:, LANES:]).astype(BF16), g2_ref[...])
    kk = k * k_k
    kk = kk / jnp.maximum(jnp.sqrt(seg_sum(kk * kk)), 1e-12)
    k = k * (1.0 + (a - 1.0) * k_a)
    bonus = seg_sum(r * k * r_k) * v

    if t_real is not None:
        valid = (tb * tblk + row) < t_real
        logw = jnp.where(valid, logw, 0.0)
        kk = jnp.where(valid, kk, 0.0)
        k = jnp.where(valid, k, 0.0)
        v_in = jnp.where(valid, v, 0.0)
    else:
        v_in = v

    def pair_rows(x):
        return jnp.concatenate([jnp.where(head_a, x, 0.0), jnp.where(head_a, 0.0, x)], axis=0)

    ri = lax.broadcasted_iota(jnp.int32, (c_len, 2 * c_len), 0)
    ci = lax.broadcasted_iota(jnp.int32, (c_len, 2 * c_len), 1)
    first = ci < c_len
    tcol = ci & (c_len - 1)
    strict = tcol < ri
    incl = tcol <= ri
    eye2 = (tcol == ri).astype(F32)
    tri = (lax.broadcasted_iota(jnp.int32, (c_len, c_len), 0)
           >= lax.broadcasted_iota(jnp.int32, (c_len, c_len), 1)).astype(BF16)
    bi = lax.broadcasted_iota(jnp.int32, (width, width), 0) < HEAD_DIM
    bj = lax.broadcasted_iota(jnp.int32, (width, width), 1) < HEAD_DIM
    same_head = bi == bj
    n_levels = c_len.bit_length() - 2

    s = state_ref[...]
    outs = []
    for c in range(tblk // c_len):
        sl = slice(c * c_len, (c + 1) * c_len)
        lw = logw[sl]
        lh = lw.astype(BF16)
        l1 = lw - lh.astype(F32)
        lm = l1.astype(BF16)
        ll = (l1 - lm.astype(F32)).astype(BF16)
        lc = _dot(tri, lh) + (_dot(tri, lm) + _dot(tri, ll))
        g_in = jnp.exp(lc)
        g_ex = jnp.exp(lc - lw)
        g_inv = jnp.exp(-lc)
        g_end = g_in[c_len - 1:c_len, :]
        kkc = kk[sl]
        a_t = -kkc * g_ex
        b_t = kkc * a[sl] * g_inv
        k_t = k[sl] * g_inv
        r_t = r[sl] * g_in
        v_c = v_in[sl]

        lhs = jnp.concatenate([pair_rows(a_t), pair_rows(r_t)], axis=0).astype(BF16)
        rhs = jnp.concatenate([b_t, k_t], axis=0).astype(BF16)
        s4 = _dot(lhs, rhs, _NT)
        sa_, sb_ = s4[0:c_len], s4[c_len:2 * c_len]
        ra_, rb_ = s4[2 * c_len:3 * c_len], s4[3 * c_len:4 * c_len]
        m_ab = jnp.where(strict, jnp.where(first, sa_, pltpu.roll(sb_, c_len, 1)), 0.0)
        m_ak = jnp.where(strict, jnp.where(first, pltpu.roll(sa_, c_len, 1), sb_), 0.0)
        m_rb = jnp.where(incl, jnp.where(first, ra_, pltpu.roll(rb_, c_len, 1)), 0.0)
        m_rk = jnp.where(incl, jnp.where(first, pltpu.roll(ra_, c_len, 1), rb_), 0.0)

        t_inv = m_ab + eye2
        l_pow = m_ab
        for _ in range(n_levels):
            l_pow = _dot(l_pow.astype(BF16), pair_rows(l_pow).astype(BF16))
            t_inv = t_inv + _dot(t_inv.astype(BF16), pair_rows(l_pow).astype(BF16))

        v_pair = pair_rows(v_c).astype(BF16)
        mak_v = _dot(m_ak.astype(BF16), v_pair)
        tw = _dot(t_inv.astype(BF16),
                  jnp.concatenate([pair_rows(a_t), pair_rows(mak_v)], axis=1).astype(BF16))
        w_t, v_hat = tw[:, :width], tw[:, width:]
        mw = _dot(m_rb.astype(BF16),
                  jnp.concatenate([pair_rows(w_t), pair_rows(v_hat)], axis=1).astype(BF16))
        p_m = r_t + mw[:, :width]
        q_m = mw[:, width:] + _dot(m_rk.astype(BF16), v_pair)

        bk = jnp.concatenate([b_t * g_end, k_t * g_end], axis=0).astype(BF16)
        left = jnp.concatenate(
            [jnp.concatenate([w_t, jnp.zeros_like(w_t)], axis=0),
             jnp.concatenate([v_hat, v_c], axis=0)], axis=1)
        pp = _dot(left.T.astype(BF16), bk)
        phi = jnp.where(same_head, pp[:width], 0.0)
        psi = jnp.where(same_head, pp[width:], 0.0)

        sb16 = s.astype(BF16)
        outs.append(_dot(p_m.astype(BF16), sb16, _NT) + q_m)
        s = s * g_end + _dot(sb16, phi.astype(BF16)) + psi

    state_ref[...] = s
    o = jnp.concatenate(outs, axis=0) if len(outs) > 1 else outs[0]
    inv_n = 1.0 / HEAD_DIM
    mu_o = seg_sum(o) * inv_n
    d = o - mu_o
    var = seg_sum(d * d) * inv_n
    on = d * lax.rsqrt(var + GN_EPS) * gn_g + gn_b
    rw_ref[0] = ((on + bonus) * g).astype(rw_ref.dtype)

    @pl.when(tb == n_tb - 1)
    def _():
        wkv_ref[0, 0] = s[:HEAD_DIM, :HEAD_DIM]
        wkv_ref[0, 1] = s[HEAD_DIM:, HEAD_DIM:]


def _rwkv(proj, shift0, wkv0, lw, *, tblk, t_real):
    b, s, _ = proj.shape
    assert s % tblk == 0 and tblk % RWKV_CHUNK == 0
    n_pairs = RWKV_WIDTH // LANES
    rkv = RWKV_WIDTH // LANES
    lora_blk = LORA_COL // LORA_BLOCK

    def z_spec(width, col0):
        return pl.BlockSpec((1, tblk, width), lambda bi, p, t: (bi, t, col0(p)))

    def s_spec(width, col0):
        return pl.BlockSpec((1, 1, width), lambda bi, p, t: (bi, 0, col0(p)))

    cols = [lambda p: p, lambda p: rkv + p, lambda p: 2 * rkv + p]
    in_specs = ([z_spec(LANES, c) for c in cols] + [z_spec(LORA_BLOCK, lambda p: lora_blk)]
                + [s_spec(LANES, c) for c in cols] + [s_spec(LORA_BLOCK, lambda p: lora_blk)]
                + [pl.BlockSpec((SUBLANES, LANES), lambda bi, p, t: (0, p)),
                   pl.BlockSpec((1, LORA_BLOCK), lambda bi, p, t: (0, 0)),
                   pl.BlockSpec((SUBLANES, LANES), lambda bi, p, t: (0, p)),
                   pl.BlockSpec((LANES, LANES), lambda bi, p, t: (0, p)),
                   pl.BlockSpec((LANES, LANES), lambda bi, p, t: (0, p)),
                   pl.BlockSpec((GATE_LORA_PAD, LANES), lambda bi, p, t: (0, p)),
                   pl.BlockSpec((1, 2, HEAD_DIM, HEAD_DIM), lambda bi, p, t: (bi, p, 0, 0))])
    return pl.pallas_call(
        functools.partial(_rwkv_kernel, t_real=t_real),
        grid=(b, n_pairs, s // tblk),
        in_specs=in_specs,
        out_specs=[pl.BlockSpec((1, tblk, LANES), lambda bi, p, t: (bi, t, p)),
                   pl.BlockSpec((1, 2, HEAD_DIM, HEAD_DIM), lambda bi, p, t: (bi, p, 0, 0))],
        out_shape=[jax.ShapeDtypeStruct((b, s, RWKV_WIDTH), BF16),
                   jax.ShapeDtypeStruct((b, N_RWKV_HEADS, HEAD_DIM, HEAD_DIM), F32)],
        scratch_shapes=[pltpu.VMEM((LANES, LANES), F32),
                        pltpu.VMEM((SUBLANES, LANES), F32),
                        pltpu.VMEM((SUBLANES, LANES), F32),
                        pltpu.VMEM((SUBLANES, LANES), F32),
                        pltpu.VMEM((SUBLANES, LORA_BLOCK), F32)],
        compiler_params=_params(("parallel", "parallel", "arbitrary")),
        name="rwkv7",
    )(proj, proj, proj, proj, shift0, shift0, shift0, shift0,
      lw["mu_rkv"], lw["mu_lora"], lw["rwkv_par"], lw["w2"], lw["a2"], lw["g2"], wkv0)


def _out_ln_kernel(att_ref, rw_ref, x_ref, wa_ref, wr_ref, g_ref, b_ref, o_ref, *, alpha):
    mix = _dot(att_ref[...], wa_ref[...]) + _dot(rw_ref[...], wr_ref[...])
    o_ref[...] = _layer_norm(alpha * x_ref[...] + mix, g_ref[...], b_ref[...])


def _out_ln(att, rw, x, lw, *, tm, alpha):
    m, d = x.shape
    assert m % tm == 0
    row = lambda w: pl.BlockSpec((tm, w), lambda i: (i, 0))
    full = lambda r, w: pl.BlockSpec((r, w), lambda i: (0, 0))
    return pl.pallas_call(
        functools.partial(_out_ln_kernel, alpha=alpha),
        grid=(m // tm,),
        in_specs=[row(ATT_WIDTH), row(RWKV_WIDTH), row(d),
                  full(ATT_WIDTH, d), full(RWKV_WIDTH, d), full(1, d), full(1, d)],
        out_specs=row(d),
        out_shape=jax.ShapeDtypeStruct((m, d), F32),
        compiler_params=_params(("parallel",)),
        name="out_proj_ln",
    )(att, rw, x, lw["w_out_att"], lw["w_out_rw"], lw["ln1_g"], lw["ln1_b"])


def _ffn_kernel(x_ref, wg_ref, wu_ref, wd_ref, cw_ref, cb_ref, g_ref, b_ref, *rest,
                alpha, tiles_per_seq, dec_seq):
    decode = dec_seq is not None
    if decode:
        h1_ref, h2_ref, y_ref, gp_ref, xb_ref, acc_ref = rest
    else:
        y_ref, gp_ref, xb_ref, acc_ref, tail_ref = rest
    i = pl.program_id(0)
    j = pl.program_id(1)
    tm = x_ref.shape[0]
    keep = SUBLANES

    @pl.when(j == 0)
    def _():
        xb_ref[...] = x_ref[...].astype(BF16)
        acc_ref[...] = jnp.zeros_like(acc_ref)

    xb = xb_ref[...]
    gp = _dot(xb, wg_ref[...])
    up = _dot(xb, wu_ref[...])
    row = lax.broadcasted_iota(jnp.int32, (tm, 1), 0)
    r1 = pltpu.roll(gp, 1, 0)
    r2 = pltpu.roll(gp, 2, 0)
    if decode:
        t = row % dec_seq
        g1 = jnp.where(t >= 1, r1, 0.0) + h1_ref[...]
        g2 = jnp.where(t >= 2, r2, 0.0) + h2_ref[...]
        gp_ref[...] = gp
    else:
        tail = tail_ref[j]
        seq_start = (i % tiles_per_seq) == 0
        t6 = jnp.where(seq_start, 0.0, tail[keep - 2:keep - 1, :])
        t7 = jnp.where(seq_start, 0.0, tail[keep - 1:keep, :])
        g1 = jnp.where(row == 0, t7, r1)
        g2 = jnp.where(row == 0, t6, jnp.where(row == 1, t7, r2))
        tail_ref[j] = gp[tm - keep:, :]
        gp_ref[0] = gp[tm - keep:, :]
    cw = cw_ref[...]
    conv = cw[0:1, :] * g2 + cw[1:2, :] * g1 + cw[2:3, :] * gp + cb_ref[...]
    h = conv * _sigmoid(conv) * up
    acc_ref[...] += _dot(h.astype(BF16), wd_ref[...])

    @pl.when(j == pl.num_programs(1) - 1)
    def _():
        y_ref[...] = _layer_norm(alpha * x_ref[...] + acc_ref[...], g_ref[...], b_ref[...])


def _ffn(x1, lw, *, tm, tf, alpha, seq_len=None, hist=None, dec_seq=None):
    m, d = x1.shape
    d_ff = lw["w_down"].shape[0]
    assert m % tm == 0 and d_ff % tf == 0
    nf = d_ff // tf
    decode = hist is not None
    in_specs = [pl.BlockSpec((tm, d), lambda i, j: (i, 0)),
                pl.BlockSpec((d, tf), lambda i, j: (0, j)),
                pl.BlockSpec((d, tf), lambda i, j: (0, nf + j)),
                pl.BlockSpec((tf, d), lambda i, j: (j, 0)),
                pl.BlockSpec((SUBLANES, tf), lambda i, j: (0, j)),
                pl.BlockSpec((1, tf), lambda i, j: (0, j)),
                pl.BlockSpec((1, d), lambda i, j: (0, 0)),
                pl.BlockSpec((1, d), lambda i, j: (0, 0))]
    args = [x1, lw["w_up"], lw["w_up"], lw["w_down"], lw["conv_w"], lw["conv_b"], lw["ln2_g"], lw["ln2_b"]]
    scratch = [pltpu.VMEM((tm, d), BF16), pltpu.VMEM((tm, d), F32)]
    if decode:
        assert m == tm
        in_specs += [pl.BlockSpec((tm, tf), lambda i, j: (0, j))] * 2
        args += list(hist)
        gp_shape = jax.ShapeDtypeStruct((m, d_ff), F32)
        gp_spec = pl.BlockSpec((tm, tf), lambda i, j: (0, j))
        tiles_per_seq = None
    else:
        assert seq_len % tm == 0 and tm % SUBLANES == 0
        tiles_per_seq = seq_len // tm
        gp_shape = jax.ShapeDtypeStruct((m // tm, SUBLANES, d_ff), F32)
        gp_spec = pl.BlockSpec((1, SUBLANES, tf), lambda i, j: (i, 0, j))
        scratch += [pltpu.VMEM((nf, SUBLANES, tf), F32)]
    return pl.pallas_call(
        functools.partial(_ffn_kernel, alpha=alpha, tiles_per_seq=tiles_per_seq, dec_seq=dec_seq),
        grid=(m // tm, nf),
        in_specs=in_specs,
        out_specs=[pl.BlockSpec((tm, d), lambda i, j: (i, 0)), gp_spec],
        out_shape=[jax.ShapeDtypeStruct((m, d), F32), gp_shape],
        scratch_shapes=scratch,
        compiler_params=_params(("arbitrary", "arbitrary")),
        name="conv_ffn_ln",
    )(*args)


def _prep_layer(w_in, w_out, shift_mu, w0, w2, a0, a2, g2, k_k, k_a, r_k, gn_g, gn_b,
                ln1_g, ln1_b, ln2_g, ln2_b, w_up, conv_w, conv_b, w_down):
    row = lambda v: v.reshape(1, -1)
    zpad = Z_PAD - RWKV_PROJ
    w_in_r = jnp.concatenate([jnp.pad(w_in[:, ATT_END:], ((0, 0), (0, zpad))), w_in[:, :ATT_END]], axis=1)
    mu = jnp.pad(shift_mu, (0, zpad))
    zero_row = jnp.zeros((RWKV_WIDTH,), F32)
    return {
        "w_in": w_in_r.astype(BF16),
        "w_out_att": w_out[:ATT_WIDTH].astype(BF16),
        "w_out_rw": w_out[ATT_WIDTH:].astype(BF16),
        "mu_rkv": jnp.pad(mu[:LORA_COL].reshape(3, RWKV_WIDTH), ((0, SUBLANES - 3), (0, 0))),
        "mu_lora": row(mu[LORA_COL:LORA_COL + LORA_BLOCK]),
        "rwkv_par": jnp.stack([w0, a0, k_k, k_a, r_k, gn_g, gn_b, zero_row]),
        "w2": jnp.pad(w2, ((0, LANES - D_DECAY_LORA), (0, 0))).astype(BF16),
        "a2": jnp.pad(a2, ((D_DECAY_LORA, LANES - D_DECAY_LORA - D_AAA_LORA), (0, 0))).astype(BF16),
        "g2": jnp.pad(g2, ((0, GATE_LORA_PAD - D_GATE_LORA), (0, 0))).astype(BF16),
        "ln1_g": row(ln1_g), "ln1_b": row(ln1_b), "ln2_g": row(ln2_g), "ln2_b": row(ln2_b),
        "w_up": w_up.astype(BF16),
        "conv_w": jnp.pad(conv_w, ((0, SUBLANES - CONV_W), (0, 0))),
        "conv_b": row(conv_b),
        "w_down": w_down.astype(BF16),
    }


def _pick_tile(n, cap):
    t = min(n, cap)
    while n % t:
        t //= 2
    return t


def _prompt_layer(x, lw, alpha):
    b, s, d = x.shape
    d_ff = lw["w_down"].shape[0]
    pos = jnp.arange(s, dtype=jnp.int32)
    proj, kh, vt = _in_proj(x, lw["w_in"], pos, tm=_pick_tile(s, 1024), attn_extras=True)
    att = _moba_prompt(proj, kh, vt)
    rw, wkv = _rwkv(proj, jnp.zeros((b, 1, Z_PAD), F32),
                    jnp.zeros((b, N_RWKV_HEADS, HEAD_DIM, HEAD_DIM), F32), lw,
                    tblk=_pick_tile(s, 256), t_real=None)
    x1 = _out_ln(att.reshape(b * s, ATT_WIDTH), rw.reshape(b * s, RWKV_WIDTH), x.reshape(b * s, d), lw,
                 tm=_pick_tile(s, 512), alpha=alpha)
    tm = _pick_tile(s, 512)
    y, gp_tail = _ffn(x1, lw, tm=tm, tf=512, alpha=alpha, seq_len=s)
    k_new = proj[:, :, K_COL:V_COL].reshape(b, s, N_KV_HEADS, HEAD_DIM)
    v_new = proj[:, :, V_COL:].reshape(b, s, N_KV_HEADS, HEAD_DIM)
    shift_new = proj[:, s - 1, :RWKV_PROJ]
    conv_new = gp_tail.reshape(b, s // tm, SUBLANES, d_ff)[:, -1, SUBLANES - (CONV_W - 1):, :]
    return y.reshape(b, s, d), k_new, v_new, wkv, shift_new, conv_new


def _decode_layer(x, cache_k, cache_v, page_table, shift0, wkv0, conv0, lw, alpha):
    bd, t, d = x.shape
    m = bd * t
    past = page_table.shape[1] * PAGE_SIZE
    pos = past + (jnp.arange(m, dtype=jnp.int32) % t)
    (proj,) = _in_proj(x.reshape(1, m, d), lw["w_in"], pos, tm=m, attn_extras=False)
    proj = proj.reshape(bd, t, PROJ_W)
    att = _moba_decode(proj, cache_k, cache_v, page_table, t)
    proj_pad = jnp.pad(proj, ((0, 0), (0, RWKV_CHUNK - t), (0, 0)))
    shift_pad = jnp.pad(shift0, ((0, 0), (0, Z_PAD - RWKV_PROJ))).reshape(bd, 1, Z_PAD)
    rw, wkv = _rwkv(proj_pad, shift_pad, wkv0, lw, tblk=RWKV_CHUNK, t_real=t)
    rw = rw[:, :t]
    x1 = _out_ln(att.reshape(m, ATT_WIDTH), rw.reshape(m, RWKV_WIDTH), x.reshape(m, d), lw, tm=m, alpha=alpha)
    d_ff = conv0.shape[-1]
    zeros = lambda n: jnp.zeros((bd, n, d_ff), F32)
    h1 = jnp.concatenate([conv0[:, 1:2], zeros(t - 1)], axis=1).reshape(m, d_ff)
    h2 = jnp.concatenate([conv0[:, 0:1], conv0[:, 1:2], zeros(t - 2)], axis=1).reshape(m, d_ff)
    y, gp = _ffn(x1, lw, tm=m, tf=512, alpha=alpha, hist=(h1, h2), dec_seq=t)
    k_new = proj[:, :, K_COL:V_COL].reshape(bd, t, N_KV_HEADS, HEAD_DIM)
    v_new = proj[:, :, V_COL:].reshape(bd, t, N_KV_HEADS, HEAD_DIM)
    shift_new = proj[:, t - 1, :RWKV_PROJ]
    conv_new = gp.reshape(bd, t, d_ff)[:, t - (CONV_W - 1):]
    return y.reshape(bd, t, d), k_new, v_new, wkv, shift_new, conv_new


def kernel(x_prompt, x_sample, cache_k, cache_v, page_table, state_wkv, state_shift, state_conv,
           w_in, w_out, shift_mu, w0, w2, a0, a2, g2, k_k, k_a, r_k, gn_g, gn_b,
           ln1_g, ln1_b, ln2_g, ln2_b, w_up, conv_w, conv_b, w_down):
    depth = w_in.shape[0]
    alpha = (2.0 * depth) ** 0.25
    assert x_sample.shape[1] >= CONV_W - 1
    yp, ys = x_prompt, x_sample
    outs_p, outs_s = [], []
    for l in range(depth):
        lw = _prep_layer(w_in[l], w_out[l], shift_mu[l], w0[l], w2[l], a0[l], a2[l], g2[l], k_k[l], k_a[l],
                         r_k[l], gn_g[l], gn_b[l], ln1_g[l], ln1_b[l], ln2_g[l], ln2_b[l],
                         w_up[l], conv_w[l], conv_b[l], w_down[l])
        yp, *rest_p = _prompt_layer(yp, lw, alpha)
        outs_p.append(rest_p)
        ys, *rest_s = _decode_layer(ys, cache_k[l], cache_v[l], page_table, state_shift[l], state_wkv[l],
                                    state_conv[l], lw, alpha)
        outs_s.append(rest_s)
    stack = lambda outs, n: jnp.stack([o[n] for o in outs])
    return (yp, ys,
            stack(outs_p, 0), stack(outs_p, 1), stack(outs_p, 2), stack(outs_p, 3), stack(outs_p, 4),
            stack(outs_s, 0), stack(outs_s, 1), stack(outs_s, 2), stack(outs_s, 3), stack(outs_s, 4))
```

```python
import functools

import jax
import jax.numpy as jnp
from jax import lax
from jax.experimental import pallas as pl
from jax.experimental.pallas import tpu as pltpu

F32 = jnp.float32
BF16 = jnp.bfloat16

HEAD_DIM = 64
N_ATT_HEADS = 16
N_KV_HEADS = 4
KV_GROUP = N_ATT_HEADS // N_KV_HEADS
ATT_WIDTH = N_ATT_HEADS * HEAD_DIM
KV_WIDTH = N_KV_HEADS * HEAD_DIM
ATT_END = ATT_WIDTH + 2 * KV_WIDTH
N_RWKV_HEADS = 16
RWKV_WIDTH = N_RWKV_HEADS * HEAD_DIM
ROPE_DIM = HEAD_DIM // 4
ROPE_THETA = 500000.0
MOBA_BLOCK = 256
MOBA_TOPK = 3
D_DECAY_LORA = 64
D_AAA_LORA = 64
D_GATE_LORA = 160
LORA_WIDTH = D_DECAY_LORA + D_AAA_LORA + D_GATE_LORA
RWKV_PROJ = 3 * RWKV_WIDTH + LORA_WIDTH
CONV_W = 3
LN_EPS = 1e-5
GN_EPS = 64e-5
NEG_INF = -1e30
PAGE_SIZE = 128

LANES = 128
SUBLANES = 8
VMEM_LIMIT_BYTES = 56 * 1024 * 1024

PROJ_TN = 512
Z_PAD = 3584
Q_COL = Z_PAD
K_COL = Q_COL + ATT_WIDTH
V_COL = K_COL + KV_WIDTH
PROJ_W = V_COL + KV_WIDTH
LORA_COL = 3 * RWKV_WIDTH
LORA_BLOCK = 384
GATE_LORA_PAD = 256

RWKV_CHUNK = HEAD_DIM
PAGES_PER_STEP = 16

_NN = (((1,), (0,)), ((), ()))
_NT = (((1,), (1,)), ((), ()))


def _dot(a, b, dims=_NN):
    return lax.dot_general(a, b, dims, preferred_element_type=F32)


def _split_bf16(x):
    hi = x.astype(BF16)
    lo = (x - hi.astype(F32)).astype(BF16)
    return hi, lo


def _dot3(a, b, dims=_NN):
    ah, al = _split_bf16(a)
    bh, bl = _split_bf16(b)
    return _dot(ah, bh, dims) + (_dot(ah, bl, dims) + _dot(al, bh, dims))


def _sigmoid(x):
    return 1.0 / (1.0 + jnp.exp(-x))


def _layer_norm(x, g, b):
    mu = jnp.mean(x, axis=-1, keepdims=True)
    d = x - mu
    var = jnp.mean(d * d, axis=-1, keepdims=True)
    return d * lax.rsqrt(var + LN_EPS) * g + b


def _params(sem):
    return pltpu.CompilerParams(dimension_semantics=sem, vmem_limit_bytes=VMEM_LIMIT_BYTES)


def _rope_tables(pos):
    half = ROPE_DIM // 2
    inv = jnp.power(ROPE_THETA, -(jnp.arange(half, dtype=F32) * 2.0 / ROPE_DIM))
    ang = pos.astype(F32)[:, None] * inv[None, :]
    cos = jnp.cos(ang)
    sin = jnp.sin(ang)
    t = pos.shape[0]
    one = jnp.ones((t, HEAD_DIM - ROPE_DIM), F32)
    z8 = jnp.zeros((t, half), F32)
    zr = jnp.zeros((t, HEAD_DIM - ROPE_DIM), F32)
    c64 = jnp.concatenate([cos, cos, one], axis=1)
    a64 = jnp.concatenate([-sin, z8, zr], axis=1)
    b64 = jnp.concatenate([z8, sin, zr], axis=1)
    rep = LANES // HEAD_DIM
    return tuple(jnp.tile(m, (1, rep)) for m in (c64, a64, b64))


def _in_proj_kernel(x_ref, w_ref, cos_ref, sa_ref, sb_ref, proj_ref, *rest, attn_extras):
    if attn_extras:
        kh_ref, vt_ref, xb_ref = rest
    else:
        (xb_ref,) = rest
    j = pl.program_id(2)
    n_z = Z_PAD // PROJ_TN
    n_q = ATT_WIDTH // PROJ_TN
    half = ROPE_DIM // 2

    @pl.when(j == 0)
    def _():
        xb_ref[...] = x_ref[0].astype(BF16)

    acc = _dot(xb_ref[...], w_ref[...])

    def rope(slab):
        return (slab * cos_ref[...] + pltpu.roll(slab, LANES - half, 1) * sa_ref[...]
                + pltpu.roll(slab, half, 1) * sb_ref[...])

    @pl.when(j < n_z)
    def _():
        proj_ref[0] = acc

    @pl.when((j >= n_z) & (j < n_z + n_q))
    def _():
        for s in range(PROJ_TN // LANES):
            proj_ref[0, :, s * LANES:(s + 1) * LANES] = rope(acc[:, s * LANES:(s + 1) * LANES])

    @pl.when(j == n_z + n_q)
    def _():
        k_slabs = []
        for s in range(KV_WIDTH // LANES):
            ks = rope(acc[:, s * LANES:(s + 1) * LANES])
            proj_ref[0, :, s * LANES:(s + 1) * LANES] = ks
            k_slabs.append(ks)
        v = acc[:, KV_WIDTH:]
        proj_ref[0, :, KV_WIDTH:] = v
        if attn_extras:
            k = jnp.concatenate(k_slabs, axis=1)
            for c in range(N_KV_HEADS):
                kh_ref[0, c] = k[:, c * HEAD_DIM:(c + 1) * HEAD_DIM].astype(BF16)
            vt_ref[0] = v.T.astype(BF16)


def _in_proj(x, w_b, pos, *, tm, attn_extras):
    b, s, d = x.shape
    assert s % tm == 0 and PROJ_W % PROJ_TN == 0
    cos, sa, sb = _rope_tables(pos)
    tab_spec = pl.BlockSpec((tm, LANES), lambda bi, i, j: (i, 0))
    out_shape = [jax.ShapeDtypeStruct((b, s, PROJ_W), F32)]
    out_specs = [pl.BlockSpec((1, tm, PROJ_TN), lambda bi, i, j: (bi, i, j))]
    if attn_extras:
        out_shape += [jax.ShapeDtypeStruct((b, N_KV_HEADS, s, HEAD_DIM), BF16),
                      jax.ShapeDtypeStruct((b, KV_WIDTH, s), BF16)]
        out_specs += [pl.BlockSpec((1, N_KV_HEADS, tm, HEAD_DIM), lambda bi, i, j: (bi, 0, i, 0)),
                      pl.BlockSpec((1, KV_WIDTH, tm), lambda bi, i, j: (bi, 0, i))]
    return pl.pallas_call(
        functools.partial(_in_proj_kernel, attn_extras=attn_extras),
        grid=(b, s // tm, PROJ_W // PROJ_TN),
        in_specs=[pl.BlockSpec((1, tm, d), lambda bi, i, j: (bi, i, 0)),
                  pl.BlockSpec((d, PROJ_TN), lambda bi, i, j: (0, j)),
                  tab_spec, tab_spec, tab_spec],
        out_specs=out_specs,
        out_shape=out_shape,
        scratch_shapes=[pltpu.VMEM((tm, d), BF16)],
        compiler_params=_params(("parallel", "parallel", "arbitrary")),
        name="in_proj",
    )(x, w_b, cos, sa, sb)


def _moba_prompt_kernel(q_ref, kh_ref, vt_ref, o_ref, kmean_ref, sel_ref, *, n_blk, topk):
    i = pl.program_id(2)
    blk = MOBA_BLOCK
    n_cand = n_blk - 1
    scale = HEAD_DIM ** -0.5

    @pl.when(i == 0)
    def _():
        kmean_ref[...] = jnp.zeros_like(kmean_ref)
        for n in range(n_blk):
            kmean_ref[n:n + 1, :] = jnp.mean(
                kh_ref[0, 0, n * blk:(n + 1) * blk, :].astype(F32), axis=0, keepdims=True)

    q_t = q_ref[0].T
    qs_t = jnp.concatenate([q_t[g * HEAD_DIM:(g + 1) * HEAD_DIM, :] for g in range(KV_GROUP)],
                           axis=1)

    gate = _dot3(kmean_ref[...], qs_t)
    n_idx = lax.broadcasted_iota(jnp.int32, gate.shape, 0)
    gate = jnp.where(n_idx < i, gate, NEG_INF)
    rank = jnp.zeros(gate.shape, F32)
    for m in range(n_cand):
        gm = gate[m:m + 1, :]
        beats = (gm > gate) | ((gm == gate) & (m < n_idx))
        rank = rank + beats.astype(F32)
    sel_ref[...] = ((rank < topk) & (n_idx < i)).astype(F32)

    qb = (qs_t * scale).astype(BF16)

    row0 = pl.multiple_of(i * blk, blk)
    s_t = _dot(kh_ref[0, 0, pl.ds(row0, blk), :], qb)
    kpos = lax.broadcasted_iota(jnp.int32, s_t.shape, 0)
    qpos = lax.broadcasted_iota(jnp.int32, s_t.shape, 1) & (blk - 1)
    s_t = jnp.where(kpos <= qpos, s_t, NEG_INF)
    m0 = jnp.max(s_t, axis=0, keepdims=True)
    p = jnp.exp(s_t - m0)
    l0 = jnp.sum(p, axis=0, keepdims=True)
    acc0 = _dot(vt_ref[0, :, pl.ds(row0, blk)], p.astype(BF16))

    def past_block(n, carry):
        m, l, acc = carry
        r0 = pl.multiple_of(n * blk, blk)
        s_n = _dot(kh_ref[0, 0, pl.ds(r0, blk), :], qb)
        s_n = jnp.where(sel_ref[pl.ds(n, 1), :] > 0.5, s_n, NEG_INF)
        m_new = jnp.maximum(m, jnp.max(s_n, axis=0, keepdims=True))
        alpha = jnp.exp(m - m_new)
        p_n = jnp.exp(s_n - m_new)
        l = l * alpha + jnp.sum(p_n, axis=0, keepdims=True)
        acc = acc * alpha + _dot(vt_ref[0, :, pl.ds(r0, blk)], p_n.astype(BF16))
        return m_new, l, acc

    _, l, acc = lax.fori_loop(0, i, past_block, (m0, l0, acc0))
    o = acc / l
    o4 = jnp.concatenate([o[:, g * blk:(g + 1) * blk] for g in range(KV_GROUP)], axis=0)
    o_ref[0] = o4.T.astype(o_ref.dtype)


def _moba_prompt(proj, kh, vt):
    b, s, _ = proj.shape
    assert s % MOBA_BLOCK == 0
    n_blk = s // MOBA_BLOCK
    topk = min(MOBA_TOPK, n_blk - 1)
    gw = KV_GROUP * HEAD_DIM
    n_rows = -(-n_blk // SUBLANES) * SUBLANES
    return pl.pallas_call(
        functools.partial(_moba_prompt_kernel, n_blk=n_blk, topk=topk),
        grid=(b, N_KV_HEADS, n_blk),
        in_specs=[pl.BlockSpec((1, MOBA_BLOCK, gw), lambda bi, c, i: (bi, i, Q_COL // gw + c)),
                  pl.BlockSpec((1, 1, s, HEAD_DIM), lambda bi, c, i: (bi, c, 0, 0)),
                  pl.BlockSpec((1, HEAD_DIM, s), lambda bi, c, i: (bi, c, 0))],
        out_specs=pl.BlockSpec((1, MOBA_BLOCK, gw), lambda bi, c, i: (bi, i, c)),
        out_shape=jax.ShapeDtypeStruct((b, s, ATT_WIDTH), BF16),
        scratch_shapes=[pltpu.VMEM((n_rows, HEAD_DIM), F32),
                        pltpu.VMEM((n_rows, KV_GROUP * MOBA_BLOCK), F32)],
        compiler_params=_params(("parallel", "parallel", "arbitrary")),
        name="moba_prompt",
    )(proj, kh, vt)


def _moba_decode_kernel(pt_ref, wt_ref, kvn_ref, *rest, n_steps, n_cand, n_new):
    del pt_ref
    r_pages = PAGES_PER_STEP
    k_refs = rest[:r_pages]
    v_refs = rest[r_pages:2 * r_pages]
    o_ref = rest[2 * r_pages]
    s_ref, p_ref, gate_ref, acc_ref, l_ref = rest[2 * r_pages + 1:]
    j = pl.program_id(1)
    scale = HEAD_DIM ** -0.5
    nq = N_ATT_HEADS * n_new
    step_keys = r_pages * PAGE_SIZE
    pages_per_blk = MOBA_BLOCK // PAGE_SIZE
    blks_per_step = r_pages // pages_per_blk
    wt_hi, wt_lo = _split_bf16(wt_ref[0])
    wt_b = (wt_ref[0] * scale).astype(BF16)

    @pl.when(j == 0)
    def _():
        gate_ref[...] = jnp.zeros_like(gate_ref)

    @pl.when(j < n_steps)
    def _():
        base = pl.multiple_of(j * step_keys, step_keys)
        wt2 = jnp.concatenate([wt_hi, wt_lo], axis=0)
        lane = lax.broadcasted_iota(jnp.int32, gate_ref.shape, 1)
        gate = gate_ref[...]
        for r in range(r_pages):
            s2 = _dot(wt2, k_refs[r][0].astype(BF16))
            raw = s2[:nq] + s2[nq:]
            s_ref[:, pl.ds(base + r * PAGE_SIZE, PAGE_SIZE)] = raw * scale
            blk_raw = raw if r % pages_per_blk == 0 else blk_raw + raw
            if r % pages_per_blk == pages_per_blk - 1:
                n = j * blks_per_step + r // pages_per_blk
                gate = gate + jnp.where(lane == n, jnp.sum(blk_raw, axis=1, keepdims=True), 0.0)
        gate_ref[...] = gate

    @pl.when(j == n_steps)
    def _():
        gate = gate_ref[...] * (1.0 / MOBA_BLOCK)
        n_idx = lax.broadcasted_iota(jnp.int32, gate.shape, 1)
        gate = jnp.where(n_idx < n_cand, gate, NEG_INF)
        rank = jnp.zeros(gate.shape, F32)
        for m in range(n_cand):
            gm = gate[:, m:m + 1]
            beats = (gm > gate) | ((gm == gate) & (m < n_idx))
            rank = rank + beats.astype(F32)
        sel = (rank < min(MOBA_TOPK, n_cand)) & (n_idx < n_cand)

        kvn = kvn_ref[0]
        k_new = kvn[:, :KV_WIDTH].astype(BF16)
        v_new = kvn[:, KV_WIDTH:].astype(BF16)
        s_new = _dot(wt_b, k_new, _NT)
        t_q = lax.broadcasted_iota(jnp.int32, s_new.shape, 0) % n_new
        t_k = lax.broadcasted_iota(jnp.int32, s_new.shape, 1)
        ok_new = (t_k <= t_q) & (t_k < n_new)
        s_new = jnp.where(ok_new, s_new, NEG_INF)
        m = jnp.max(s_new, axis=1, keepdims=True)
        for n in range(n_cand):
            sb = s_ref[:, n * MOBA_BLOCK:(n + 1) * MOBA_BLOCK]
            sb = jnp.where(sel[:, n:n + 1], sb, NEG_INF)
            m = jnp.maximum(m, jnp.max(sb, axis=1, keepdims=True))
        p_new = jnp.where(ok_new, jnp.exp(s_new - m), 0.0)
        l = jnp.sum(p_new, axis=1, keepdims=True)
        for n in range(n_cand):
            sb = s_ref[:, n * MOBA_BLOCK:(n + 1) * MOBA_BLOCK]
            pb = jnp.where(sel[:, n:n + 1], jnp.exp(sb - m), 0.0)
            l = l + jnp.sum(pb, axis=1, keepdims=True)
            p_ref[:, n * MOBA_BLOCK:(n + 1) * MOBA_BLOCK] = pb.astype(BF16)
        l_ref[...] = jnp.broadcast_to(l, l_ref.shape)
        acc_ref[...] = _dot(p_new.astype(BF16), v_new)

    @pl.when(j >= n_steps)
    def _():
        base = pl.multiple_of((j - n_steps) * step_keys, step_keys)
        acc = acc_ref[...]
        for r in range(r_pages):
            acc = acc + _dot(p_ref[:, pl.ds(base + r * PAGE_SIZE, PAGE_SIZE)], v_refs[r][0].astype(BF16), _NT)
        acc_ref[...] = acc

    @pl.when(j == 2 * n_steps - 1)
    def _():
        o_ref[0] = acc_ref[...] / l_ref[:, 0:1]


def _moba_decode(proj_d, cache_k, cache_v, page_table, n_new):
    bd = proj_d.shape[0]
    n_pages = page_table.shape[1]
    past = n_pages * PAGE_SIZE
    assert past % MOBA_BLOCK == 0 and n_pages % PAGES_PER_STEP == 0 and n_new <= SUBLANES
    n_cand = past // MOBA_BLOCK
    n_steps = n_pages // PAGES_PER_STEP
    n_phys = cache_k.shape[0]
    nq = N_ATT_HEADS * n_new
    ck = jnp.transpose(cache_k, (0, 2, 3, 1)).reshape(n_phys, KV_WIDTH, PAGE_SIZE)
    cv = jnp.transpose(cache_v, (0, 2, 3, 1)).reshape(n_phys, KV_WIDTH, PAGE_SIZE)

    q = proj_d[:, :, Q_COL:Q_COL + ATT_WIDTH].reshape(bd, n_new, N_KV_HEADS, KV_GROUP, HEAD_DIM)
    q = jnp.transpose(q, (0, 2, 3, 1, 4)).reshape(bd, N_KV_HEADS, KV_GROUP * n_new, 1, HEAD_DIM)
    eye = jnp.eye(N_KV_HEADS, dtype=F32)[None, :, None, :, None]
    wt = (q * eye).reshape(bd, nq, KV_WIDTH)
    kvn = jnp.pad(proj_d[:, :, K_COL:], ((0, 0), (0, SUBLANES - n_new), (0, 0)))

    def k_map(r):
        return lambda b, j, pt: (pt[b, jnp.minimum(j, n_steps - 1) * PAGES_PER_STEP + r], 0, 0)

    def v_map(r):
        return lambda b, j, pt: (pt[b, jnp.maximum(j - n_steps, 0) * PAGES_PER_STEP + r], 0, 0)

    page_block = (1, KV_WIDTH, PAGE_SIZE)
    in_specs = ([pl.BlockSpec((1, nq, KV_WIDTH), lambda b, j, pt: (b, 0, 0)),
                 pl.BlockSpec((1, SUBLANES, 2 * KV_WIDTH), lambda b, j, pt: (b, 0, 0))]
                + [pl.BlockSpec(page_block, k_map(r)) for r in range(PAGES_PER_STEP)]
                + [pl.BlockSpec(page_block, v_map(r)) for r in range(PAGES_PER_STEP)])
    n_cand_pad = -(-n_cand // LANES) * LANES
    out = pl.pallas_call(
        functools.partial(_moba_decode_kernel, n_steps=n_steps, n_cand=n_cand, n_new=n_new),
        grid_spec=pltpu.PrefetchScalarGridSpec(
            num_scalar_prefetch=1,
            grid=(bd, 2 * n_steps),
            in_specs=in_specs,
            out_specs=pl.BlockSpec((1, nq, KV_WIDTH), lambda b, j, pt: (b, 0, 0)),
            scratch_shapes=[pltpu.VMEM((nq, past), F32),
                            pltpu.VMEM((nq, past), BF16),
                            pltpu.VMEM((nq, n_cand_pad), F32),
                            pltpu.VMEM((nq, KV_WIDTH), F32),
                            pltpu.VMEM((nq, LANES), F32)]),
        out_shape=jax.ShapeDtypeStruct((bd, nq, KV_WIDTH), F32),
        compiler_params=_params(("parallel", "arbitrary")),
        name="moba_decode",
    )(page_table, wt, kvn, *([ck] * PAGES_PER_STEP), *([cv] * PAGES_PER_STEP))
    o = out.reshape(bd, N_KV_HEADS, KV_GROUP, n_new, N_KV_HEADS, HEAD_DIM)
    o = jnp.stack([o[:, c, :, :, c, :] for c in range(N_KV_HEADS)], axis=1)
    o = jnp.transpose(o, (0, 3, 1, 2, 4)).reshape(bd, n_new, ATT_WIDTH)
    return o.astype(BF16)


def _rwkv_kernel(zr_ref, zk_ref, zv_ref, zl_ref, sr_ref, sk_ref, sv_ref, sl_ref,
                 mu_ref, mul_ref, par_ref, w2_ref, a2_ref, g2_ref, s0_ref,
                 rw_ref, wkv_ref,
                 state_ref, pr_ref, pk_ref, pv_ref, plo_ref, *, t_real):
    tb = pl.program_id(2)
    n_tb = pl.num_programs(2)
    tblk = zr_ref.shape[1]
    c_len = RWKV_CHUNK
    width = LANES
    lane = lax.broadcasted_iota(jnp.int32, (1, width), 1)
    head_a = lane < HEAD_DIM

    @pl.when(tb == 0)
    def _():
        z = jnp.zeros((HEAD_DIM, HEAD_DIM), F32)
        state_ref[...] = jnp.concatenate(
            [jnp.concatenate([s0_ref[0, 0], z], axis=1),
             jnp.concatenate([z, s0_ref[0, 1]], axis=1)], axis=0)
        pr_ref[0:1, :] = sr_ref[0]
        pk_ref[0:1, :] = sk_ref[0]
        pv_ref[0:1, :] = sv_ref[0]
        plo_ref[0:1, :] = sl_ref[0]

    row = lax.broadcasted_iota(jnp.int32, (tblk, 1), 0)

    def token_shift(z, prev_ref, mu):
        z_prev = jnp.where(row == 0, prev_ref[0:1, :], pltpu.roll(z, 1, 0))
        prev_ref[0:1, :] = z[tblk - 1:tblk, :]
        return z + (z_prev - z) * mu

    def seg_sum(x):
        sa = jnp.sum(jnp.where(head_a, x, 0.0), axis=1, keepdims=True)
        sb = jnp.sum(jnp.where(head_a, 0.0, x), axis=1, keepdims=True)
        return jnp.where(head_a, sa, sb)

    r = token_shift(zr_ref[0], pr_ref, mu_ref[0:1, :])
    k = token_shift(zk_ref[0], pk_ref, mu_ref[1:2, :])
    v = token_shift(zv_ref[0], pv_ref, mu_ref[2:3, :])
    lo = token_shift(zl_ref[0], plo_ref, mul_ref[...])
    par = par_ref[...]
    w0, a0, k_k, k_a, r_k, gn_g, gn_b = (par[n:n + 1, :] for n in range(7))

    lo_wa = lo[:, :LANES]
    nx = -(w0 + _dot(jnp.tanh(lo_wa).astype(BF16), w2_ref[...]))
    softplus = jnp.maximum(nx, 0.0) + jnp.log(1.0 + jnp.exp(-jnp.abs(nx)))
    logw = -jnp.exp(-softplus - 0.5)
    a = _sigmoid(a0 + _dot(lo_wa.astype(BF16), a2_ref[...]))
    g = _dot(_sigmoid(lo[:, LANES:]).astype(BF16), g2_ref[...])
    kk = k * k_k
    kk = kk / jnp.maximum(jnp.sqrt(seg_sum(kk * kk)), 1e-12)
    k = k * (1.0 + (a - 1.0) * k_a)
    bonus = seg_sum(r * k * r_k) * v

    if t_real is not None:
        valid = (tb * tblk + row) < t_real
        logw = jnp.where(valid, logw, 0.0)
        kk = jnp.where(valid, kk, 0.0)
        k = jnp.where(valid, k, 0.0)
        v_in = jnp.where(valid, v, 0.0)
    else:
        v_in = v

    def pair_rows(x):
        return jnp.concatenate([jnp.where(head_a, x, 0.0), jnp.where(head_a, 0.0, x)], axis=0)

    ri = lax.broadcasted_iota(jnp.int32, (c_len, 2 * c_len), 0)
    ci = lax.broadcasted_iota(jnp.int32, (c_len, 2 * c_len), 1)
    first = ci < c_len
    tcol = ci & (c_len - 1)
    strict = tcol < ri
    incl = tcol <= ri
    eye2 = (tcol == ri).astype(F32)
    bi = lax.broadcasted_iota(jnp.int32, (width, width), 0) < HEAD_DIM
    bj = lax.broadcasted_iota(jnp.int32, (width, width), 1) < HEAD_DIM
    same_head = bi == bj
    n_levels = c_len.bit_length() - 2
    shift = c_len.bit_length() - 1
    chunks = [slice(c * c_len, (c + 1) * c_len) for c in range(tblk // c_len)]
    bf = lambda x: x.astype(BF16)

    tr = lax.broadcasted_iota(jnp.int32, (tblk, tblk), 0)
    tc = lax.broadcasted_iota(jnp.int32, (tblk, tblk), 1)
    tri = ((tr >= tc) & ((tr >> shift) == (tc >> shift))).astype(BF16)
    lh = logw.astype(BF16)
    l1 = logw - lh.astype(F32)
    lm = l1.astype(BF16)
    ll = (l1 - lm.astype(F32)).astype(BF16)
    lc = _dot(tri, lh) + (_dot(tri, lm) + _dot(tri, ll))
    g_in = jnp.exp(lc)
    g_inv = jnp.exp(-lc)
    a_all = -kk * jnp.exp(lc - logw)
    b_all = kk * a * g_inv
    k_all = k * g_inv
    r_all = r * g_in

    a_t = [a_all[sl] for sl in chunks]
    r_t = [r_all[sl] for sl in chunks]
    b_t = [b_all[sl] for sl in chunks]
    k_t = [k_all[sl] for sl in chunks]
    v_c = [v_in[sl] for sl in chunks]
    g_end = [g_in[sl.stop - 1:sl.stop, :] for sl in chunks]
    a_pair = [pair_rows(x) for x in a_t]
    s4 = [_dot(bf(jnp.concatenate([ap, pair_rows(rt)], axis=0)),
               bf(jnp.concatenate([bt, kt], axis=0)), _NT)
          for ap, rt, bt, kt in zip(a_pair, r_t, b_t, k_t)]
    m_ab, m_ak, m_rb, m_rk = [], [], [], []
    for x in s4:
        sa_, sb_ = x[0:c_len], x[c_len:2 * c_len]
        ra_, rb_ = x[2 * c_len:3 * c_len], x[3 * c_len:4 * c_len]
        m_ab.append(jnp.where(strict, jnp.where(first, sa_, pltpu.roll(sb_, c_len, 1)), 0.0))
        m_ak.append(jnp.where(strict, jnp.where(first, pltpu.roll(sa_, c_len, 1), sb_), 0.0))
        m_rb.append(jnp.where(incl, jnp.where(first, ra_, pltpu.roll(rb_, c_len, 1)), 0.0))
        m_rk.append(jnp.where(incl, jnp.where(first, pltpu.roll(ra_, c_len, 1), rb_), 0.0))

    t_inv = [m + eye2 for m in m_ab]
    l_pow = m_ab
    for _ in range(n_levels):
        l_pow = [_dot(bf(lp), bf(pair_rows(lp))) for lp in l_pow]
        t_inv = [ti + _dot(bf(ti), bf(pair_rows(lp))) for ti, lp in zip(t_inv, l_pow)]

    v_pair = [bf(pair_rows(x)) for x in v_c]
    mak_v = [_dot(bf(m), vp) for m, vp in zip(m_ak, v_pair)]
    tw = [_dot(bf(ti), bf(jnp.concatenate([ap, pair_rows(mv)], axis=1)))
          for ti, ap, mv in zip(t_inv, a_pair, mak_v)]
    w_t = [x[:, :width] for x in tw]
    v_hat = [x[:, width:] for x in tw]
    mw = [_dot(bf(m), bf(jnp.concatenate([pair_rows(w), pair_rows(vh)], axis=1)))
          for m, w, vh in zip(m_rb, w_t, v_hat)]
    p_m = [rt + x[:, :width] for rt, x in zip(r_t, mw)]
    q_m = [x[:, width:] + _dot(bf(m), vp) for x, m, vp in zip(mw, m_rk, v_pair)]
    phi, psi = [], []
    for w, vh, vc, bt, kt, ge in zip(w_t, v_hat, v_c, b_t, k_t, g_end):
        bk = bf(jnp.concatenate([bt * ge, kt * ge], axis=0))
        left = jnp.concatenate(
            [jnp.concatenate([w, jnp.zeros_like(w)], axis=0),
             jnp.concatenate([vh, vc], axis=0)], axis=1)
        pp = _dot(bf(left.T), bk)
        phi.append(bf(jnp.where(same_head, pp[:width], 0.0)))
        psi.append(jnp.where(same_head, pp[width:], 0.0))

    s = state_ref[...]
    outs = []
    for pm, qm, ph, ps, ge in zip(p_m, q_m, phi, psi, g_end):
        sb16 = bf(s)
        outs.append(_dot(bf(pm), sb16, _NT) + qm)
        s = s * ge + _dot(sb16, ph) + ps

    state_ref[...] = s
    o = jnp.concatenate(outs, axis=0) if len(outs) > 1 else outs[0]
    inv_n = 1.0 / HEAD_DIM
    mu_o = seg_sum(o) * inv_n
    d = o - mu_o
    var = seg_sum(d * d) * inv_n
    on = d * lax.rsqrt(var + GN_EPS) * gn_g + gn_b
    rw_ref[0] = ((on + bonus) * g).astype(rw_ref.dtype)

    @pl.when(tb == n_tb - 1)
    def _():
        wkv_ref[0, 0] = s[:HEAD_DIM, :HEAD_DIM]
        wkv_ref[0, 1] = s[HEAD_DIM:, HEAD_DIM:]


def _rwkv(proj, shift0, wkv0, lw, *, tblk, t_real):
    b, s, _ = proj.shape
    assert s % tblk == 0 and tblk % RWKV_CHUNK == 0
    n_pairs = RWKV_WIDTH // LANES
    rkv = RWKV_WIDTH // LANES
    lora_blk = LORA_COL // LORA_BLOCK

    def z_spec(width, col0):
        return pl.BlockSpec((1, tblk, width), lambda bi, p, t: (bi, t, col0(p)))

    def s_spec(width, col0):
        return pl.BlockSpec((1, 1, width), lambda bi, p, t: (bi, 0, col0(p)))

    cols = [lambda p: p, lambda p: rkv + p, lambda p: 2 * rkv + p]
    in_specs = ([z_spec(LANES, c) for c in cols] + [z_spec(LORA_BLOCK, lambda p: lora_blk)]
                + [s_spec(LANES, c) for c in cols] + [s_spec(LORA_BLOCK, lambda p: lora_blk)]
                + [pl.BlockSpec((SUBLANES, LANES), lambda bi, p, t: (0, p)),
                   pl.BlockSpec((1, LORA_BLOCK), lambda bi, p, t: (0, 0)),
                   pl.BlockSpec((SUBLANES, LANES), lambda bi, p, t: (0, p)),
                   pl.BlockSpec((LANES, LANES), lambda bi, p, t: (0, p)),
                   pl.BlockSpec((LANES, LANES), lambda bi, p, t: (0, p)),
                   pl.BlockSpec((GATE_LORA_PAD, LANES), lambda bi, p, t: (0, p)),
                   pl.BlockSpec((1, 2, HEAD_DIM, HEAD_DIM), lambda bi, p, t: (bi, p, 0, 0))])
    return pl.pallas_call(
        functools.partial(_rwkv_kernel, t_real=t_real),
        grid=(b, n_pairs, s // tblk),
        in_specs=in_specs,
        out_specs=[pl.BlockSpec((1, tblk, LANES), lambda bi, p, t: (bi, t, p)),
                   pl.BlockSpec((1, 2, HEAD_DIM, HEAD_DIM), lambda bi, p, t: (bi, p, 0, 0))],
        out_shape=[jax.ShapeDtypeStruct((b, s, RWKV_WIDTH), BF16),
                   jax.ShapeDtypeStruct((b, N_RWKV_HEADS, HEAD_DIM, HEAD_DIM), F32)],
        scratch_shapes=[pltpu.VMEM((LANES, LANES), F32),
                        pltpu.VMEM((SUBLANES, LANES), F32),
                        pltpu.VMEM((SUBLANES, LANES), F32),
                        pltpu.VMEM((SUBLANES, LANES), F32),
                        pltpu.VMEM((SUBLANES, LORA_BLOCK), F32)],
        compiler_params=_params(("parallel", "parallel", "arbitrary")),
        name="rwkv7",
    )(proj, proj, proj, proj, shift0, shift0, shift0, shift0,
      lw["mu_rkv"], lw["mu_lora"], lw["rwkv_par"], lw["w2"], lw["a2"], lw["g2"], wkv0)


def _out_ln_kernel(att_ref, rw_ref, x_ref, wa_ref, wr_ref, g_ref, b_ref, o_ref, *, alpha):
    mix = _dot(att_ref[...], wa_ref[...]) + _dot(rw_ref[...], wr_ref[...])
    o_ref[...] = _layer_norm(alpha * x_ref[...] + mix, g_ref[...], b_ref[...])


def _out_ln(att, rw, x, lw, *, tm, alpha):
    m, d = x.shape
    assert m % tm == 0
    row = lambda w: pl.BlockSpec((tm, w), lambda i: (i, 0))
    full = lambda r, w: pl.BlockSpec((r, w), lambda i: (0, 0))
    return pl.pallas_call(
        functools.partial(_out_ln_kernel, alpha=alpha),
        grid=(m // tm,),
        in_specs=[row(ATT_WIDTH), row(RWKV_WIDTH), row(d),
                  full(ATT_WIDTH, d), full(RWKV_WIDTH, d), full(1, d), full(1, d)],
        out_specs=row(d),
        out_shape=jax.ShapeDtypeStruct((m, d), F32),
        compiler_params=_params(("parallel",)),
        name="out_proj_ln",
    )(att, rw, x, lw["w_out_att"], lw["w_out_rw"], lw["ln1_g"], lw["ln1_b"])


def _ffn_kernel(x_ref, wg_ref, wu_ref, wd_ref, cw_ref, cb_ref, g_ref, b_ref, *rest,
                alpha, tiles_per_seq, dec_seq):
    decode = dec_seq is not None
    if decode:
        h1_ref, h2_ref, y_ref, gp_ref, xb_ref, acc_ref = rest
    else:
        y_ref, gp_ref, xb_ref, acc_ref, tail_ref = rest
    i = pl.program_id(0)
    j = pl.program_id(1)
    tm = x_ref.shape[0]
    keep = SUBLANES

    @pl.when(j == 0)
    def _():
        xb_ref[...] = x_ref[...].astype(BF16)
        acc_ref[...] = jnp.zeros_like(acc_ref)

    xb = xb_ref[...]
    gp = _dot(xb, wg_ref[...])
    up = _dot(xb, wu_ref[...])
    row = lax.broadcasted_iota(jnp.int32, (tm, 1), 0)
    r1 = pltpu.roll(gp, 1, 0)
    r2 = pltpu.roll(gp, 2, 0)
    if decode:
        t = row % dec_seq
        g1 = jnp.where(t >= 1, r1, 0.0) + h1_ref[...]
        g2 = jnp.where(t >= 2, r2, 0.0) + h2_ref[...]
        gp_ref[...] = gp
    else:
        tail = tail_ref[j]
        seq_start = (i % tiles_per_seq) == 0
        t6 = jnp.where(seq_start, 0.0, tail[keep - 2:keep - 1, :])
        t7 = jnp.where(seq_start, 0.0, tail[keep - 1:keep, :])
        g1 = jnp.where(row == 0, t7, r1)
        g2 = jnp.where(row == 0, t6, jnp.where(row == 1, t7, r2))
        tail_ref[j] = gp[tm - keep:, :]
        gp_ref[0] = gp[tm - keep:, :]
    cw = cw_ref[...]
    conv = cw[0:1, :] * g2 + cw[1:2, :] * g1 + cw[2:3, :] * gp + cb_ref[...]
    h = conv * _sigmoid(conv) * up
    acc_ref[...] += _dot(h.astype(BF16), wd_ref[...])

    @pl.when(j == pl.num_programs(1) - 1)
    def _():
        y_ref[...] = _layer_norm(alpha * x_ref[...] + acc_ref[...], g_ref[...], b_ref[...])


def _ffn(x1, lw, *, tm, tf, alpha, seq_len=None, hist=None, dec_seq=None):
    m, d = x1.shape
    d_ff = lw["w_down"].shape[0]
    assert m % tm == 0 and d_ff % tf == 0
    nf = d_ff // tf
    decode = hist is not None
    in_specs = [pl.BlockSpec((tm, d), lambda i, j: (i, 0)),
                pl.BlockSpec((d, tf), lambda i, j: (0, j)),
                pl.BlockSpec((d, tf), lambda i, j: (0, nf + j)),
                pl.BlockSpec((tf, d), lambda i, j: (j, 0)),
                pl.BlockSpec((SUBLANES, tf), lambda i, j: (0, j)),
                pl.BlockSpec((1, tf), lambda i, j: (0, j)),
                pl.BlockSpec((1, d), lambda i, j: (0, 0)),
                pl.BlockSpec((1, d), lambda i, j: (0, 0))]
    args = [x1, lw["w_up"], lw["w_up"], lw["w_down"], lw["conv_w"], lw["conv_b"], lw["ln2_g"], lw["ln2_b"]]
    scratch = [pltpu.VMEM((tm, d), BF16), pltpu.VMEM((tm, d), F32)]
    if decode:
        assert m == tm
        in_specs += [pl.BlockSpec((tm, tf), lambda i, j: (0, j))] * 2
        args += list(hist)
        gp_shape = jax.ShapeDtypeStruct((m, d_ff), F32)
        gp_spec = pl.BlockSpec((tm, tf), lambda i, j: (0, j))
        tiles_per_seq = None
    else:
        assert seq_len % tm == 0 and tm % SUBLANES == 0
        tiles_per_seq = seq_len // tm
        gp_shape = jax.ShapeDtypeStruct((m // tm, SUBLANES, d_ff), F32)
        gp_spec = pl.BlockSpec((1, SUBLANES, tf), lambda i, j: (i, 0, j))
        scratch += [pltpu.VMEM((nf, SUBLANES, tf), F32)]
    return pl.pallas_call(
        functools.partial(_ffn_kernel, alpha=alpha, tiles_per_seq=tiles_per_seq, dec_seq=dec_seq),
        grid=(m // tm, nf),
        in_specs=in_specs,
        out_specs=[pl.BlockSpec((tm, d), lambda i, j: (i, 0)), gp_spec],
        out_shape=[jax.ShapeDtypeStruct((m, d), F32), gp_shape],
        scratch_shapes=scratch,
        compiler_params=_params(("arbitrary", "arbitrary")),
        name="conv_ffn_ln",
    )(*args)


def _prep_layer(w_in, w_out, shift_mu, w0, w2, a0, a2, g2, k_k, k_a, r_k, gn_g, gn_b,
                ln1_g, ln1_b, ln2_g, ln2_b, w_up, conv_w, conv_b, w_down):
    row = lambda v: v.reshape(1, -1)
    zpad = Z_PAD - RWKV_PROJ
    w_in_r = jnp.concatenate([jnp.pad(w_in[:, ATT_END:], ((0, 0), (0, zpad))), w_in[:, :ATT_END]], axis=1)
    mu = jnp.pad(shift_mu, (0, zpad))
    zero_row = jnp.zeros((RWKV_WIDTH,), F32)
    return {
        "w_in": w_in_r.astype(BF16),
        "w_out_att": w_out[:ATT_WIDTH].astype(BF16),
        "w_out_rw": w_out[ATT_WIDTH:].astype(BF16),
        "mu_rkv": jnp.pad(mu[:LORA_COL].reshape(3, RWKV_WIDTH), ((0, SUBLANES - 3), (0, 0))),
        "mu_lora": row(mu[LORA_COL:LORA_COL + LORA_BLOCK]),
        "rwkv_par": jnp.stack([w0, a0, k_k, k_a, r_k, gn_g, gn_b, zero_row]),
        "w2": jnp.pad(w2, ((0, LANES - D_DECAY_LORA), (0, 0))).astype(BF16),
        "a2": jnp.pad(a2, ((D_DECAY_LORA, LANES - D_DECAY_LORA - D_AAA_LORA), (0, 0))).astype(BF16),
        "g2": jnp.pad(g2, ((0, GATE_LORA_PAD - D_GATE_LORA), (0, 0))).astype(BF16),
        "ln1_g": row(ln1_g), "ln1_b": row(ln1_b), "ln2_g": row(ln2_g), "ln2_b": row(ln2_b),
        "w_up": w_up.astype(BF16),
        "conv_w": jnp.pad(conv_w, ((0, SUBLANES - CONV_W), (0, 0))),
        "conv_b": row(conv_b),
        "w_down": w_down.astype(BF16),
    }


def _pick_tile(n, cap):
    t = min(n, cap)
    while n % t:
        t //= 2
    return t


def _prompt_layer(x, lw, alpha):
    b, s, d = x.shape
    d_ff = lw["w_down"].shape[0]
    pos = jnp.arange(s, dtype=jnp.int32)
    proj, kh, vt = _in_proj(x, lw["w_in"], pos, tm=_pick_tile(s, 1024), attn_extras=True)
    att = _moba_prompt(proj, kh, vt)
    rw, wkv = _rwkv(proj, jnp.zeros((b, 1, Z_PAD), F32),
                    jnp.zeros((b, N_RWKV_HEADS, HEAD_DIM, HEAD_DIM), F32), lw,
                    tblk=_pick_tile(s, 512), t_real=None)
    x1 = _out_ln(att.reshape(b * s, ATT_WIDTH), rw.reshape(b * s, RWKV_WIDTH), x.reshape(b * s, d), lw,
                 tm=_pick_tile(s, 512), alpha=alpha)
    tm = _pick_tile(s, 512)
    y, gp_tail = _ffn(x1, lw, tm=tm, tf=512, alpha=alpha, seq_len=s)
    k_new = proj[:, :, K_COL:V_COL].reshape(b, s, N_KV_HEADS, HEAD_DIM)
    v_new = proj[:, :, V_COL:].reshape(b, s, N_KV_HEADS, HEAD_DIM)
    shift_new = proj[:, s - 1, :RWKV_PROJ]
    conv_new = gp_tail.reshape(b, s // tm, SUBLANES, d_ff)[:, -1, SUBLANES - (CONV_W - 1):, :]
    return y.reshape(b, s, d), k_new, v_new, wkv, shift_new, conv_new


def _decode_layer(x, cache_k, cache_v, page_table, shift0, wkv0, conv0, lw, alpha):
    bd, t, d = x.shape
    m = bd * t
    past = page_table.shape[1] * PAGE_SIZE
    pos = past + (jnp.arange(m, dtype=jnp.int32) % t)
    (proj,) = _in_proj(x.reshape(1, m, d), lw["w_in"], pos, tm=m, attn_extras=False)
    proj = proj.reshape(bd, t, PROJ_W)
    att = _moba_decode(proj, cache_k, cache_v, page_table, t)
    proj_pad = jnp.pad(proj, ((0, 0), (0, RWKV_CHUNK - t), (0, 0)))
    shift_pad = jnp.pad(shift0, ((0, 0), (0, Z_PAD - RWKV_PROJ))).reshape(bd, 1, Z_PAD)
    rw, wkv = _rwkv(proj_pad, shift_pad, wkv0, lw, tblk=RWKV_CHUNK, t_real=t)
    rw = rw[:, :t]
    x1 = _out_ln(att.reshape(m, ATT_WIDTH), rw.reshape(m, RWKV_WIDTH), x.reshape(m, d), lw, tm=m, alpha=alpha)
    d_ff = conv0.shape[-1]
    zeros = lambda n: jnp.zeros((bd, n, d_ff), F32)
    h1 = jnp.concatenate([conv0[:, 1:2], zeros(t - 1)], axis=1).reshape(m, d_ff)
    h2 = jnp.concatenate([conv0[:, 0:1], conv0[:, 1:2], zeros(t - 2)], axis=1).reshape(m, d_ff)
    y, gp = _ffn(x1, lw, tm=m, tf=512, alpha=alpha, hist=(h1, h2), dec_seq=t)
    k_new = proj[:, :, K_COL:V_COL].reshape(bd, t, N_KV_HEADS, HEAD_DIM)
    v_new = proj[:, :, V_COL:].reshape(bd, t, N_KV_HEADS, HEAD_DIM)
    shift_new = proj[:, t - 1, :RWKV_PROJ]
    conv_new = gp.reshape(bd, t, d_ff)[:, t - (CONV_W - 1):]
    return y.reshape(bd, t, d), k_new, v_new, wkv, shift_new, conv_new


def kernel(x_prompt, x_sample, cache_k, cache_v, page_table, state_wkv, state_shift, state_conv,
           w_in, w_out, shift_mu, w0, w2, a0, a2, g2, k_k, k_a, r_k, gn_g, gn_b,
           ln1_g, ln1_b, ln2_g, ln2_b, w_up, conv_w, conv_b, w_down):
    depth = w_in.shape[0]
    alpha = (2.0 * depth) ** 0.25
    assert x_sample.shape[1] >= CONV_W - 1
    yp, ys = x_prompt, x_sample
    outs_p, outs_s = [], []
    for l in range(depth):
        lw = _prep_layer(w_in[l], w_out[l], shift_mu[l], w0[l], w2[l], a0[l], a2[l], g2[l], k_k[l], k_a[l],
                         r_k[l], gn_g[l], gn_b[l], ln1_g[l], ln1_b[l], ln2_g[l], ln2_b[l],
                         w_up[l], conv_w[l], conv_b[l], w_down[l])
        yp, *rest_p = _prompt_layer(yp, lw, alpha)
        outs_p.append(rest_p)
        ys, *rest_s = _decode_layer(ys, cache_k[l], cache_v[l], page_table, state_shift[l], state_wkv[l],
                                    state_conv[l], lw, alpha)
        outs_s.append(rest_s)
    stack = lambda outs, n: jnp.stack([o[n] for o in outs])
    return (yp, ys,
            stack(outs_p, 0), stack(outs_p, 1), stack(outs_p, 2), stack(outs_p, 3), stack(outs_p, 4),
            stack(outs_s, 0), stack(outs_s, 1), stack(outs_s, 2), stack(outs_s, 3), stack(outs_s, 4))
```

```python
import functools

import jax
import jax.numpy as jnp
from jax import lax
from jax.experimental import pallas as pl
from jax.experimental.pallas import tpu as pltpu

F32 = jnp.float32
BF16 = jnp.bfloat16

HEAD_DIM = 64
N_ATT_HEADS = 16
N_KV_HEADS = 4
KV_GROUP = N_ATT_HEADS // N_KV_HEADS
ATT_WIDTH = N_ATT_HEADS * HEAD_DIM
KV_WIDTH = N_KV_HEADS * HEAD_DIM
ATT_END = ATT_WIDTH + 2 * KV_WIDTH
N_RWKV_HEADS = 16
RWKV_WIDTH = N_RWKV_HEADS * HEAD_DIM
ROPE_DIM = HEAD_DIM // 4
ROPE_THETA = 500000.0
MOBA_BLOCK = 256
MOBA_TOPK = 3
MOBA_QUERY_PARTS = 1
D_DECAY_LORA = 64
D_AAA_LORA = 64
D_GATE_LORA = 160
LORA_WIDTH = D_DECAY_LORA + D_AAA_LORA + D_GATE_LORA
RWKV_PROJ = 3 * RWKV_WIDTH + LORA_WIDTH
CONV_W = 3
LN_EPS = 1e-5
GN_EPS = 64e-5
NEG_INF = -1e30
PAGE_SIZE = 128

LANES = 128
SUBLANES = 8
VMEM_LIMIT_BYTES = 56 * 1024 * 1024

PROJ_TN = 512
Z_PAD = 3584
Q_COL = Z_PAD
K_COL = Q_COL + ATT_WIDTH
V_COL = K_COL + KV_WIDTH
PROJ_W = V_COL + KV_WIDTH
LORA_COL = 3 * RWKV_WIDTH
LORA_BLOCK = 384
GATE_LORA_PAD = 256

RWKV_CHUNK = HEAD_DIM
RWKV_SEQS_PER_STEP = 8
RWKV_STAGE_GROUP = 8
PAGES_PER_STEP = 16

_NN = (((1,), (0,)), ((), ()))
_NT = (((1,), (1,)), ((), ()))


def _dot(a, b, dims=_NN):
    return lax.dot_general(a, b, dims, preferred_element_type=F32)


def _split_bf16(x):
    hi = x.astype(BF16)
    lo = (x - hi.astype(F32)).astype(BF16)
    return hi, lo


def _dot3(a, b, dims=_NN):
    ah, al = _split_bf16(a)
    bh, bl = _split_bf16(b)
    return _dot(ah, bh, dims) + (_dot(ah, bl, dims) + _dot(al, bh, dims))


def _sigmoid(x):
    return 1.0 / (1.0 + jnp.exp(-x))


def _layer_norm(x, g, b):
    mu = jnp.mean(x, axis=-1, keepdims=True)
    d = x - mu
    var = jnp.mean(d * d, axis=-1, keepdims=True)
    return d * lax.rsqrt(var + LN_EPS) * g + b


def _params(sem):
    return pltpu.CompilerParams(dimension_semantics=sem, vmem_limit_bytes=VMEM_LIMIT_BYTES)


def _rope_tables(pos):
    half = ROPE_DIM // 2
    inv = jnp.power(ROPE_THETA, -(jnp.arange(half, dtype=F32) * 2.0 / ROPE_DIM))
    ang = pos.astype(F32)[:, None] * inv[None, :]
    cos = jnp.cos(ang)
    sin = jnp.sin(ang)
    t = pos.shape[0]
    one = jnp.ones((t, HEAD_DIM - ROPE_DIM), F32)
    z8 = jnp.zeros((t, half), F32)
    zr = jnp.zeros((t, HEAD_DIM - ROPE_DIM), F32)
    c64 = jnp.concatenate([cos, cos, one], axis=1)
    a64 = jnp.concatenate([-sin, z8, zr], axis=1)
    b64 = jnp.concatenate([z8, sin, zr], axis=1)
    rep = LANES // HEAD_DIM
    return tuple(jnp.tile(m, (1, rep)) for m in (c64, a64, b64))


def _in_proj_kernel(x_ref, w_ref, cos_ref, sa_ref, sb_ref, proj_ref, *rest, attn_extras):
    if attn_extras:
        kh_ref, vt_ref, xb_ref = rest
    else:
        (xb_ref,) = rest
    j = pl.program_id(2)
    n_z = Z_PAD // PROJ_TN
    n_q = ATT_WIDTH // PROJ_TN
    half = ROPE_DIM // 2

    @pl.when(j == 0)
    def _():
        xb_ref[...] = x_ref[0].astype(BF16)

    acc = _dot(xb_ref[...], w_ref[...])

    def rope(slab):
        return (slab * cos_ref[...] + pltpu.roll(slab, LANES - half, 1) * sa_ref[...]
                + pltpu.roll(slab, half, 1) * sb_ref[...])

    @pl.when(j < n_z)
    def _():
        proj_ref[0] = acc

    @pl.when((j >= n_z) & (j < n_z + n_q))
    def _():
        for s in range(PROJ_TN // LANES):
            proj_ref[0, :, s * LANES:(s + 1) * LANES] = rope(acc[:, s * LANES:(s + 1) * LANES])

    @pl.when(j == n_z + n_q)
    def _():
        k_slabs = []
        for s in range(KV_WIDTH // LANES):
            ks = rope(acc[:, s * LANES:(s + 1) * LANES])
            proj_ref[0, :, s * LANES:(s + 1) * LANES] = ks
            k_slabs.append(ks)
        v = acc[:, KV_WIDTH:]
        proj_ref[0, :, KV_WIDTH:] = v
        if attn_extras:
            k = jnp.concatenate(k_slabs, axis=1)
            for c in range(N_KV_HEADS):
                kh_ref[0, c] = k[:, c * HEAD_DIM:(c + 1) * HEAD_DIM].astype(BF16)
            vt_ref[0] = v.T.astype(BF16)


def _in_proj(x, w_b, pos, *, tm, attn_extras):
    b, s, d = x.shape
    assert s % tm == 0 and PROJ_W % PROJ_TN == 0
    cos, sa, sb = _rope_tables(pos)
    tab_spec = pl.BlockSpec((tm, LANES), lambda bi, i, j: (i, 0))
    out_shape = [jax.ShapeDtypeStruct((b, s, PROJ_W), F32)]
    out_specs = [pl.BlockSpec((1, tm, PROJ_TN), lambda bi, i, j: (bi, i, j))]
    if attn_extras:
        out_shape += [jax.ShapeDtypeStruct((b, N_KV_HEADS, s, HEAD_DIM), BF16),
                      jax.ShapeDtypeStruct((b, KV_WIDTH, s), BF16)]
        out_specs += [pl.BlockSpec((1, N_KV_HEADS, tm, HEAD_DIM), lambda bi, i, j: (bi, 0, i, 0)),
                      pl.BlockSpec((1, KV_WIDTH, tm), lambda bi, i, j: (bi, 0, i))]
    return pl.pallas_call(
        functools.partial(_in_proj_kernel, attn_extras=attn_extras),
        grid=(b, s // tm, PROJ_W // PROJ_TN),
        in_specs=[pl.BlockSpec((1, tm, d), lambda bi, i, j: (bi, i, 0)),
                  pl.BlockSpec((d, PROJ_TN), lambda bi, i, j: (0, j)),
                  tab_spec, tab_spec, tab_spec],
        out_specs=out_specs,
        out_shape=out_shape,
        scratch_shapes=[pltpu.VMEM((tm, d), BF16)],
        compiler_params=_params(("parallel", "parallel", "arbitrary")),
        name="in_proj",
    )(x, w_b, cos, sa, sb)


def _moba_prompt_kernel(q_ref, kh_ref, vt_ref, o_ref, kmean_ref, sel_ref, *, n_blk, topk):
    i = pl.program_id(2)
    blk = MOBA_BLOCK
    n_cand = n_blk - 1
    scale = HEAD_DIM ** -0.5

    @pl.when(i == 0)
    def _():
        kmean_ref[...] = jnp.zeros_like(kmean_ref)
        for n in range(n_blk):
            kmean_ref[n:n + 1, :] = jnp.mean(
                kh_ref[0, 0, n * blk:(n + 1) * blk, :].astype(F32), axis=0, keepdims=True)

    q_t = q_ref[0].T
    qs_t = jnp.concatenate([q_t[g * HEAD_DIM:(g + 1) * HEAD_DIM, :] for g in range(KV_GROUP)],
                           axis=1)

    gate = _dot3(kmean_ref[...], qs_t)
    n_idx = lax.broadcasted_iota(jnp.int32, gate.shape, 0)
    gate = jnp.where(n_idx < i, gate, NEG_INF)
    rank = jnp.zeros(gate.shape, F32)
    for m in range(n_cand):
        gm = gate[m:m + 1, :]
        beats = (gm > gate) | ((gm == gate) & (m < n_idx))
        rank = rank + beats.astype(F32)
    sel_ref[...] = ((rank < topk) & (n_idx < i)).astype(F32)

    qb = (qs_t * scale).astype(BF16)
    n_parts = MOBA_QUERY_PARTS
    pw = KV_GROUP * blk // n_parts
    qb_parts = [qb[:, h * pw:(h + 1) * pw] for h in range(n_parts)]

    row0 = pl.multiple_of(i * blk, blk)
    k_own = kh_ref[0, 0, pl.ds(row0, blk), :]
    v_own = vt_ref[0, :, pl.ds(row0, blk)]
    kpos = lax.broadcasted_iota(jnp.int32, (blk, pw), 0)
    qpos = lax.broadcasted_iota(jnp.int32, (blk, pw), 1) & (blk - 1)
    causal = kpos <= qpos
    carry0 = []
    for h in range(n_parts):
        s_t = jnp.where(causal, _dot(k_own, qb_parts[h]), NEG_INF)
        m0 = jnp.max(s_t, axis=0, keepdims=True)
        p = jnp.exp(s_t - m0)
        carry0 += [m0, jnp.sum(p, axis=0, keepdims=True), _dot(v_own, p.astype(BF16))]

    def past_block(n, carry):
        r0 = pl.multiple_of(n * blk, blk)
        k_n = kh_ref[0, 0, pl.ds(r0, blk), :]
        v_n = vt_ref[0, :, pl.ds(r0, blk)]
        out = []
        for h in range(n_parts):
            m, l, acc = carry[3 * h:3 * h + 3]
            s_n = _dot(k_n, qb_parts[h])
            s_n = jnp.where(sel_ref[pl.ds(n, 1), h * pw:(h + 1) * pw] > 0.5, s_n, NEG_INF)
            m_new = jnp.maximum(m, jnp.max(s_n, axis=0, keepdims=True))
            alpha = jnp.exp(m - m_new)
            p_n = jnp.exp(s_n - m_new)
            l = l * alpha + jnp.sum(p_n, axis=0, keepdims=True)
            acc = acc * alpha + _dot(v_n, p_n.astype(BF16))
            out += [m_new, l, acc]
        return tuple(out)

    carry = lax.fori_loop(0, i, past_block, tuple(carry0))
    o = jnp.concatenate([carry[3 * h + 2] / carry[3 * h + 1] for h in range(n_parts)], axis=1)
    o4 = jnp.concatenate([o[:, g * blk:(g + 1) * blk] for g in range(KV_GROUP)], axis=0)
    o_ref[0] = o4.T.astype(o_ref.dtype)


def _moba_prompt(proj, kh, vt):
    b, s, _ = proj.shape
    assert s % MOBA_BLOCK == 0
    n_blk = s // MOBA_BLOCK
    topk = min(MOBA_TOPK, n_blk - 1)
    gw = KV_GROUP * HEAD_DIM
    n_rows = -(-n_blk // SUBLANES) * SUBLANES
    return pl.pallas_call(
        functools.partial(_moba_prompt_kernel, n_blk=n_blk, topk=topk),
        grid=(b, N_KV_HEADS, n_blk),
        in_specs=[pl.BlockSpec((1, MOBA_BLOCK, gw), lambda bi, c, i: (bi, i, Q_COL // gw + c)),
                  pl.BlockSpec((1, 1, s, HEAD_DIM), lambda bi, c, i: (bi, c, 0, 0)),
                  pl.BlockSpec((1, HEAD_DIM, s), lambda bi, c, i: (bi, c, 0))],
        out_specs=pl.BlockSpec((1, MOBA_BLOCK, gw), lambda bi, c, i: (bi, i, c)),
        out_shape=jax.ShapeDtypeStruct((b, s, ATT_WIDTH), BF16),
        scratch_shapes=[pltpu.VMEM((n_rows, HEAD_DIM), F32),
                        pltpu.VMEM((n_rows, KV_GROUP * MOBA_BLOCK), F32)],
        compiler_params=_params(("parallel", "parallel", "arbitrary")),
        name="moba_prompt",
    )(proj, kh, vt)


def _moba_decode_kernel(pt_ref, wt_ref, kvn_ref, *rest, n_steps, n_cand, n_new):
    del pt_ref
    r_pages = PAGES_PER_STEP
    k_refs = rest[:r_pages]
    v_refs = rest[r_pages:2 * r_pages]
    o_ref = rest[2 * r_pages]
    s_ref, gate_ref, bmax_ref, sel_ref, acc_ref, l_ref, m_ref = rest[2 * r_pages + 1:]
    n_cand_pad = gate_ref.shape[1]
    j = pl.program_id(1)
    scale = HEAD_DIM ** -0.5
    nq = N_ATT_HEADS * n_new
    step_keys = r_pages * PAGE_SIZE
    pages_per_blk = MOBA_BLOCK // PAGE_SIZE
    blks_per_step = r_pages // pages_per_blk
    wt_hi, wt_lo = _split_bf16(wt_ref[0])
    wt_b = (wt_ref[0] * scale).astype(BF16)

    lane = lax.broadcasted_iota(jnp.int32, gate_ref.shape, 1)
    lane_f = lane.astype(F32)

    @pl.when(j == 0)
    def _():
        gate_ref[...] = jnp.zeros_like(gate_ref)
        bmax_ref[...] = jnp.full(bmax_ref.shape, NEG_INF, F32)

    @pl.when(j < n_steps)
    def _():
        base = pl.multiple_of(j * step_keys, step_keys)
        wt2 = jnp.concatenate([wt_hi, wt_lo], axis=0)
        gate = gate_ref[...]
        bmax = bmax_ref[...]
        for r in range(r_pages):
            s2 = _dot(wt2, k_refs[r][0].astype(BF16))
            raw = s2[:nq] + s2[nq:]
            s_ref[:, pl.ds(base + r * PAGE_SIZE, PAGE_SIZE)] = raw * scale
            first = r % pages_per_blk == 0
            blk_sum = raw if first else blk_sum + raw
            blk_max = raw if first else jnp.maximum(blk_max, raw)
            if r % pages_per_blk == pages_per_blk - 1:
                here = lane == j * blks_per_step + r // pages_per_blk
                gate = jnp.where(here, jnp.sum(blk_sum, axis=1, keepdims=True), gate)
                bmax = jnp.where(here, jnp.max(blk_max, axis=1, keepdims=True) * scale, bmax)
        gate_ref[...] = gate
        bmax_ref[...] = bmax

    @pl.when(j == n_steps)
    def _():
        gate = jnp.where(lane < n_cand, gate_ref[...] * (1.0 / MOBA_BLOCK), NEG_INF)
        sel = lane < 0
        for _ in range(min(MOBA_TOPK, n_cand)):
            top = jnp.max(gate, axis=1, keepdims=True)
            first_top = jnp.min(jnp.where(gate == top, lane_f, float(n_cand_pad)), axis=1, keepdims=True)
            pick = lane_f == first_top
            sel = sel | pick
            gate = jnp.where(pick, -jnp.inf, gate)
        sel = sel & (lane < n_cand)
        sel_ref[...] = sel.astype(F32)

        kvn = kvn_ref[0]
        k_new = kvn[:, :KV_WIDTH].astype(BF16)
        v_new = kvn[:, KV_WIDTH:].astype(BF16)
        s_new = _dot(wt_b, k_new, _NT)
        t_q = lax.broadcasted_iota(jnp.int32, s_new.shape, 0) % n_new
        t_k = lax.broadcasted_iota(jnp.int32, s_new.shape, 1)
        ok_new = (t_k <= t_q) & (t_k < n_new)
        m = jnp.maximum(jnp.max(jnp.where(ok_new, s_new, NEG_INF), axis=1, keepdims=True),
                        jnp.max(jnp.where(sel, bmax_ref[...], NEG_INF), axis=1, keepdims=True))
        p_new = jnp.where(ok_new, jnp.exp(s_new - m), 0.0)
        m_ref[...] = jnp.broadcast_to(m, m_ref.shape)
        l_lane = lax.broadcasted_iota(jnp.int32, l_ref.shape, 1)
        l_ref[...] = jnp.where(l_lane == 0, jnp.sum(p_new, axis=1, keepdims=True), 0.0)
        acc_ref[...] = _dot(p_new.astype(BF16), v_new)

    @pl.when(j >= n_steps)
    def _():
        jv = j - n_steps
        base = pl.multiple_of(jv * step_keys, step_keys)
        src = lax.broadcasted_iota(jnp.int32, (n_cand_pad, LANES), 0)
        dst = lax.broadcasted_iota(jnp.int32, (n_cand_pad, LANES), 1)
        onehot = ((src == jv * blks_per_step + dst) & (dst < blks_per_step)).astype(BF16)
        sel_here = _dot(sel_ref[...].astype(BF16), onehot)
        m = m_ref[:, 0:1]
        acc = acc_ref[...]
        l_part = l_ref[...]
        for r in range(r_pages):
            blk = r // pages_per_blk
            sb = s_ref[:, pl.ds(base + r * PAGE_SIZE, PAGE_SIZE)]
            pb = jnp.where(sel_here[:, blk:blk + 1] > 0.5, jnp.exp(sb - m), 0.0)
            l_part = l_part + pb
            acc = acc + _dot(pb.astype(BF16), v_refs[r][0].astype(BF16), _NT)
        acc_ref[...] = acc
        l_ref[...] = l_part

    @pl.when(j == 2 * n_steps - 1)
    def _():
        o_ref[0] = acc_ref[...] / jnp.sum(l_ref[...], axis=1, keepdims=True)


def _moba_decode(proj_d, cache_k, cache_v, page_table, n_new):
    bd = proj_d.shape[0]
    n_pages = page_table.shape[1]
    past = n_pages * PAGE_SIZE
    assert past % MOBA_BLOCK == 0 and n_pages % PAGES_PER_STEP == 0 and n_new <= SUBLANES
    n_cand = past // MOBA_BLOCK
    n_steps = n_pages // PAGES_PER_STEP
    n_phys = cache_k.shape[0]
    nq = N_ATT_HEADS * n_new
    ck = jnp.transpose(cache_k, (0, 2, 3, 1)).reshape(n_phys, KV_WIDTH, PAGE_SIZE)
    cv = jnp.transpose(cache_v, (0, 2, 3, 1)).reshape(n_phys, KV_WIDTH, PAGE_SIZE)

    q = proj_d[:, :, Q_COL:Q_COL + ATT_WIDTH].reshape(bd, n_new, N_KV_HEADS, KV_GROUP, HEAD_DIM)
    q = jnp.transpose(q, (0, 2, 3, 1, 4)).reshape(bd, N_KV_HEADS, KV_GROUP * n_new, 1, HEAD_DIM)
    eye = jnp.eye(N_KV_HEADS, dtype=F32)[None, :, None, :, None]
    wt = (q * eye).reshape(bd, nq, KV_WIDTH)
    kvn = jnp.pad(proj_d[:, :, K_COL:], ((0, 0), (0, SUBLANES - n_new), (0, 0)))

    def k_map(r):
        return lambda b, j, pt: (pt[b, jnp.minimum(j, n_steps - 1) * PAGES_PER_STEP + r], 0, 0)

    def v_map(r):
        return lambda b, j, pt: (pt[b, jnp.maximum(j - n_steps, 0) * PAGES_PER_STEP + r], 0, 0)

    page_block = (1, KV_WIDTH, PAGE_SIZE)
    in_specs = ([pl.BlockSpec((1, nq, KV_WIDTH), lambda b, j, pt: (b, 0, 0)),
                 pl.BlockSpec((1, SUBLANES, 2 * KV_WIDTH), lambda b, j, pt: (b, 0, 0))]
                + [pl.BlockSpec(page_block, k_map(r)) for r in range(PAGES_PER_STEP)]
                + [pl.BlockSpec(page_block, v_map(r)) for r in range(PAGES_PER_STEP)])
    n_cand_pad = -(-n_cand // LANES) * LANES
    out = pl.pallas_call(
        functools.partial(_moba_decode_kernel, n_steps=n_steps, n_cand=n_cand, n_new=n_new),
        grid_spec=pltpu.PrefetchScalarGridSpec(
            num_scalar_prefetch=1,
            grid=(bd, 2 * n_steps),
            in_specs=in_specs,
            out_specs=pl.BlockSpec((1, nq, KV_WIDTH), lambda b, j, pt: (b, 0, 0)),
            scratch_shapes=[pltpu.VMEM((nq, past), F32),
                            pltpu.VMEM((nq, n_cand_pad), F32),
                            pltpu.VMEM((nq, n_cand_pad), F32),
                            pltpu.VMEM((nq, n_cand_pad), F32),
                            pltpu.VMEM((nq, KV_WIDTH), F32),
                            pltpu.VMEM((nq, LANES), F32),
                            pltpu.VMEM((nq, LANES), F32)]),
        out_shape=jax.ShapeDtypeStruct((bd, nq, KV_WIDTH), F32),
        compiler_params=_params(("parallel", "arbitrary")),
        name="moba_decode",
    )(page_table, wt, kvn, *([ck] * PAGES_PER_STEP), *([cv] * PAGES_PER_STEP))
    o = out.reshape(bd, N_KV_HEADS, KV_GROUP, n_new, N_KV_HEADS, HEAD_DIM)
    o = jnp.stack([o[:, c, :, :, c, :] for c in range(N_KV_HEADS)], axis=1)
    o = jnp.transpose(o, (0, 3, 1, 2, 4)).reshape(bd, n_new, ATT_WIDTH)
    return o.astype(BF16)


def _rwkv_kernel(zr_ref, zk_ref, zv_ref, zl_ref, sr_ref, sk_ref, sv_ref, sl_ref,
                 mu_ref, mul_ref, par_ref, w2_ref, a2_ref, g2_ref, s0_ref,
                 rw_ref, wkv_ref,
                 state_ref, pr_ref, pk_ref, pv_ref, plo_ref, *, t_real, independent):
    tb = pl.program_id(2)
    n_tb = pl.num_programs(2)
    tblk = zr_ref.shape[1]
    c_len = RWKV_CHUNK
    width = LANES
    lane = lax.broadcasted_iota(jnp.int32, (1, width), 1)
    head_a = lane < HEAD_DIM
    row = lax.broadcasted_iota(jnp.int32, (tblk, 1), 0)

    def block_diag(sa, sb):
        z = jnp.zeros((HEAD_DIM, HEAD_DIM), F32)
        return jnp.concatenate([jnp.concatenate([sa, z], axis=1),
                                jnp.concatenate([z, sb], axis=1)], axis=0)

    if independent:
        def token_shift(z, first_ref, prev_ref, mu):
            z_prev = pltpu.roll(z, 1, 0)
            first = first_ref[0]
            for c in range(tblk // c_len):
                z_prev = jnp.where(row == c * c_len, first[c:c + 1, :], z_prev)
            return z + (z_prev - z) * mu
    else:
        @pl.when(tb == 0)
        def _():
            state_ref[...] = block_diag(s0_ref[0, 0], s0_ref[0, 1])
            pr_ref[0:1, :] = sr_ref[0]
            pk_ref[0:1, :] = sk_ref[0]
            pv_ref[0:1, :] = sv_ref[0]
            plo_ref[0:1, :] = sl_ref[0]

        def token_shift(z, first_ref, prev_ref, mu):
            z_prev = jnp.where(row == 0, prev_ref[0:1, :], pltpu.roll(z, 1, 0))
            prev_ref[0:1, :] = z[tblk - 1:tblk, :]
            return z + (z_prev - z) * mu

    def seg_sum(x):
        sa = jnp.sum(jnp.where(head_a, x, 0.0), axis=1, keepdims=True)
        sb = jnp.sum(jnp.where(head_a, 0.0, x), axis=1, keepdims=True)
        return jnp.where(head_a, sa, sb)

    r = token_shift(zr_ref[0], sr_ref, pr_ref, mu_ref[0:1, :])
    k = token_shift(zk_ref[0], sk_ref, pk_ref, mu_ref[1:2, :])
    v = token_shift(zv_ref[0], sv_ref, pv_ref, mu_ref[2:3, :])
    lo = token_shift(zl_ref[0], sl_ref, plo_ref, mul_ref[...])
    par = par_ref[...]
    w0, a0, k_k, k_a, r_k, gn_g, gn_b = (par[n:n + 1, :] for n in range(7))

    lo_wa = lo[:, :LANES]
    nx = -(w0 + _dot(jnp.tanh(lo_wa).astype(BF16), w2_ref[...]))
    softplus = jnp.maximum(nx, 0.0) + jnp.log(1.0 + jnp.exp(-jnp.abs(nx)))
    logw = -jnp.exp(-softplus - 0.5)
    a = _sigmoid(a0 + _dot(lo_wa.astype(BF16), a2_ref[...]))
    g = _dot(_sigmoid(lo[:, LANES:]).astype(BF16), g2_ref[...])
    kk = k * k_k
    kk = kk / jnp.maximum(jnp.sqrt(seg_sum(kk * kk)), 1e-12)
    k = k * (1.0 + (a - 1.0) * k_a)
    bonus = seg_sum(r * k * r_k) * v

    if t_real is not None:
        valid = ((row & (c_len - 1)) if independent else (tb * tblk + row)) < t_real
        logw = jnp.where(valid, logw, 0.0)
        kk = jnp.where(valid, kk, 0.0)
        k = jnp.where(valid, k, 0.0)
        v_in = jnp.where(valid, v, 0.0)
    else:
        v_in = v

    def pair_rows(x):
        return jnp.concatenate([jnp.where(head_a, x, 0.0), jnp.where(head_a, 0.0, x)], axis=0)

    ri = lax.broadcasted_iota(jnp.int32, (c_len, 2 * c_len), 0)
    ci = lax.broadcasted_iota(jnp.int32, (c_len, 2 * c_len), 1)
    first = ci < c_len
    tcol = ci & (c_len - 1)
    strict = tcol < ri
    incl = tcol <= ri
    eye2 = (tcol == ri).astype(F32)
    bi = lax.broadcasted_iota(jnp.int32, (width, width), 0) < HEAD_DIM
    bj = lax.broadcasted_iota(jnp.int32, (width, width), 1) < HEAD_DIM
    same_head = bi == bj
    live = c_len if t_real is None else min(t_real, c_len)
    n_levels = max((live - 1).bit_length() - 1, 0)
    shift = c_len.bit_length() - 1
    chunks = [slice(c * c_len, (c + 1) * c_len) for c in range(tblk // c_len)]
    bf = lambda x: x.astype(BF16)

    tr = lax.broadcasted_iota(jnp.int32, (tblk, tblk), 0)
    tc = lax.broadcasted_iota(jnp.int32, (tblk, tblk), 1)
    tri = ((tr >= tc) & ((tr >> shift) == (tc >> shift))).astype(BF16)
    lh = logw.astype(BF16)
    l1 = logw - lh.astype(F32)
    lm = l1.astype(BF16)
    ll = (l1 - lm.astype(F32)).astype(BF16)
    lc = _dot(tri, lh) + (_dot(tri, lm) + _dot(tri, ll))
    g_in = jnp.exp(lc)
    g_inv = jnp.exp(-lc)
    a_all = -kk * jnp.exp(lc - logw)
    b_all = kk * a * g_inv
    k_all = k * g_inv
    r_all = r * g_in

    def group_terms(sls):
        a_t = [a_all[sl] for sl in sls]
        r_t = [r_all[sl] for sl in sls]
        b_t = [b_all[sl] for sl in sls]
        k_t = [k_all[sl] for sl in sls]
        v_c = [v_in[sl] for sl in sls]
        g_end = [g_in[sl.stop - 1:sl.stop, :] for sl in sls]
        a_pair = [pair_rows(bf(x)) for x in a_t]
        s4 = [_dot(jnp.concatenate([ap, pair_rows(bf(rt))], axis=0),
                   bf(jnp.concatenate([bt, kt], axis=0)), _NT)
              for ap, rt, bt, kt in zip(a_pair, r_t, b_t, k_t)]
        m_ab, m_ak, m_rb, m_rk = [], [], [], []
        for x in s4:
            sa_, sb_ = x[0:c_len], x[c_len:2 * c_len]
            ra_, rb_ = x[2 * c_len:3 * c_len], x[3 * c_len:4 * c_len]
            m_ab.append(jnp.where(strict, jnp.where(first, sa_, pltpu.roll(sb_, c_len, 1)), 0.0))
            m_ak.append(jnp.where(strict, jnp.where(first, pltpu.roll(sa_, c_len, 1), sb_), 0.0))
            m_rb.append(jnp.where(incl, jnp.where(first, ra_, pltpu.roll(rb_, c_len, 1)), 0.0))
            m_rk.append(jnp.where(incl, jnp.where(first, pltpu.roll(ra_, c_len, 1), rb_), 0.0))

        t_inv = [m + eye2 for m in m_ab]
        if n_levels > 0:
            l_pow = [bf(m) for m in m_ab]
            l_pow = [bf(_dot(lp, pair_rows(lp))) for lp in l_pow]
            for level in range(n_levels):
                qp = [_dot(lp, jnp.concatenate([pair_rows(lp), pair_rows(bf(ti))], axis=1))
                      for lp, ti in zip(l_pow, t_inv)]
                t_inv = [ti + x[:, width:] for ti, x in zip(t_inv, qp)]
                if level + 1 < n_levels:
                    l_pow = [bf(x[:, :width]) for x in qp]

        v_pair = [pair_rows(bf(x)) for x in v_c]
        mak_v = [_dot(bf(m), vp) for m, vp in zip(m_ak, v_pair)]
        tw = [_dot(bf(ti), jnp.concatenate([ap, pair_rows(bf(mv))], axis=1))
              for ti, ap, mv in zip(t_inv, a_pair, mak_v)]
        w_t = [x[:, :width] for x in tw]
        v_hat = [x[:, width:] for x in tw]
        mw = [_dot(bf(jnp.concatenate([mb, mk], axis=1)),
                   jnp.concatenate(
                       [jnp.concatenate([pair_rows(bf(w)), pair_rows(bf(vh))], axis=1),
                        jnp.concatenate([jnp.zeros_like(vp), vp], axis=1)], axis=0))
              for mb, mk, w, vh, vp in zip(m_rb, m_rk, w_t, v_hat, v_pair)]
        p_m = [rt + x[:, :width] for rt, x in zip(r_t, mw)]
        q_m = [x[:, width:] for x in mw]
        phi, psi = [], []
        for w, vh, vc, bt, kt, ge in zip(w_t, v_hat, v_c, b_t, k_t, g_end):
            bk = bf(jnp.concatenate([bt * ge, kt * ge], axis=0))
            left = jnp.concatenate(
                [jnp.concatenate([w, jnp.zeros_like(w)], axis=0),
                 jnp.concatenate([vh, vc], axis=0)], axis=1)
            pp = _dot(bf(left.T), bk)
            phi.append(bf(jnp.where(same_head, pp[:width], 0.0)))
            psi.append(jnp.where(same_head, pp[width:], 0.0))
        return list(zip(p_m, q_m, phi, psi, g_end))

    def advance(s, term):
        pm, qm, ph, ps, ge = term
        sb16 = bf(s)
        return _dot(bf(pm), sb16, _NT) + qm, s * ge + _dot(sb16, ph) + ps

    outs = []
    s = None if independent else state_ref[...]
    for g0 in range(0, len(chunks), RWKV_STAGE_GROUP):
        for c, term in enumerate(group_terms(chunks[g0:g0 + RWKV_STAGE_GROUP]), start=g0):
            if independent:
                o_c, s_c = advance(block_diag(s0_ref[c, 0], s0_ref[c, 1]), term)
                wkv_ref[c, 0] = s_c[:HEAD_DIM, :HEAD_DIM]
                wkv_ref[c, 1] = s_c[HEAD_DIM:, HEAD_DIM:]
            else:
                o_c, s = advance(s, term)
            outs.append(o_c)
    if not independent:
        state_ref[...] = s

        @pl.when(tb == n_tb - 1)
        def _():
            wkv_ref[0, 0] = s[:HEAD_DIM, :HEAD_DIM]
            wkv_ref[0, 1] = s[HEAD_DIM:, HEAD_DIM:]

    o = jnp.concatenate(outs, axis=0) if len(outs) > 1 else outs[0]
    inv_n = 1.0 / HEAD_DIM
    mu_o = seg_sum(o) * inv_n
    d = o - mu_o
    var = seg_sum(d * d) * inv_n
    on = d * lax.rsqrt(var + GN_EPS) * gn_g + gn_b
    rw_ref[0] = ((on + bonus) * g).astype(rw_ref.dtype)


def _rwkv(proj, shift0, wkv0, lw, *, tblk, t_real, independent=False):
    b, s, _ = proj.shape
    assert s % tblk == 0 and tblk % RWKV_CHUNK == 0
    n_pairs = RWKV_WIDTH // LANES
    rkv = RWKV_WIDTH // LANES
    lora_blk = LORA_COL // LORA_BLOCK
    n_first = shift0.shape[1]
    n_state = tblk // RWKV_CHUNK if independent else 1
    assert (not independent) or (s == tblk and n_first == n_state)

    def z_spec(width, col0):
        return pl.BlockSpec((1, tblk, width), lambda bi, p, t: (bi, t, col0(p)))

    def s_spec(width, col0):
        return pl.BlockSpec((1, n_first, width), lambda bi, p, t: (bi, 0, col0(p)))

    state_spec = pl.BlockSpec((n_state, 2, HEAD_DIM, HEAD_DIM), lambda bi, p, t: (bi, p, 0, 0))

    cols = [lambda p: p, lambda p: rkv + p, lambda p: 2 * rkv + p]
    in_specs = ([z_spec(LANES, c) for c in cols] + [z_spec(LORA_BLOCK, lambda p: lora_blk)]
                + [s_spec(LANES, c) for c in cols] + [s_spec(LORA_BLOCK, lambda p: lora_blk)]
                + [pl.BlockSpec((SUBLANES, LANES), lambda bi, p, t: (0, p)),
                   pl.BlockSpec((1, LORA_BLOCK), lambda bi, p, t: (0, 0)),
                   pl.BlockSpec((SUBLANES, LANES), lambda bi, p, t: (0, p)),
                   pl.BlockSpec((LANES, LANES), lambda bi, p, t: (0, p)),
                   pl.BlockSpec((LANES, LANES), lambda bi, p, t: (0, p)),
                   pl.BlockSpec((GATE_LORA_PAD, LANES), lambda bi, p, t: (0, p)),
                   state_spec])
    return pl.pallas_call(
        functools.partial(_rwkv_kernel, t_real=t_real, independent=independent),
        grid=(b, n_pairs, s // tblk),
        in_specs=in_specs,
        out_specs=[pl.BlockSpec((1, tblk, LANES), lambda bi, p, t: (bi, t, p)), state_spec],
        out_shape=[jax.ShapeDtypeStruct((b, s, RWKV_WIDTH), BF16),
                   jax.ShapeDtypeStruct((b * n_state, N_RWKV_HEADS, HEAD_DIM, HEAD_DIM), F32)],
        scratch_shapes=[pltpu.VMEM((LANES, LANES), F32),
                        pltpu.VMEM((SUBLANES, LANES), F32),
                        pltpu.VMEM((SUBLANES, LANES), F32),
                        pltpu.VMEM((SUBLANES, LANES), F32),
                        pltpu.VMEM((SUBLANES, LORA_BLOCK), F32)],
        compiler_params=_params(("parallel", "parallel", "arbitrary")),
        name="rwkv7",
    )(proj, proj, proj, proj, shift0, shift0, shift0, shift0,
      lw["mu_rkv"], lw["mu_lora"], lw["rwkv_par"], lw["w2"], lw["a2"], lw["g2"], wkv0)


def _out_ln_kernel(att_ref, rw_ref, x_ref, wa_ref, wr_ref, g_ref, b_ref, o_ref, *, alpha):
    mix = _dot(att_ref[...], wa_ref[...]) + _dot(rw_ref[...], wr_ref[...])
    o_ref[...] = _layer_norm(alpha * x_ref[...] + mix, g_ref[...], b_ref[...])


def _out_ln(att, rw, x, lw, *, tm, alpha):
    m, d = x.shape
    assert m % tm == 0
    row = lambda w: pl.BlockSpec((tm, w), lambda i: (i, 0))
    full = lambda r, w: pl.BlockSpec((r, w), lambda i: (0, 0))
    return pl.pallas_call(
        functools.partial(_out_ln_kernel, alpha=alpha),
        grid=(m // tm,),
        in_specs=[row(ATT_WIDTH), row(RWKV_WIDTH), row(d),
                  full(ATT_WIDTH, d), full(RWKV_WIDTH, d), full(1, d), full(1, d)],
        out_specs=row(d),
        out_shape=jax.ShapeDtypeStruct((m, d), F32),
        compiler_params=_params(("parallel",)),
        name="out_proj_ln",
    )(att, rw, x, lw["w_out_att"], lw["w_out_rw"], lw["ln1_g"], lw["ln1_b"])


def _ffn_kernel(x_ref, wg_ref, wu_ref, wd_ref, cw_ref, cb_ref, g_ref, b_ref, *rest,
                alpha, tiles_per_seq, dec_seq):
    decode = dec_seq is not None
    if decode:
        h1_ref, h2_ref, y_ref, gp_ref, xb_ref, acc_ref = rest
    else:
        y_ref, gp_ref, xb_ref, acc_ref, tail_ref = rest
    i = pl.program_id(0)
    j = pl.program_id(1)
    tm = x_ref.shape[0]
    keep = SUBLANES

    @pl.when(j == 0)
    def _():
        xb_ref[...] = x_ref[...].astype(BF16)
        acc_ref[...] = jnp.zeros_like(acc_ref)

    xb = xb_ref[...]
    gp = _dot(xb, wg_ref[...])
    up = _dot(xb, wu_ref[...])
    row = lax.broadcasted_iota(jnp.int32, (tm, 1), 0)
    r1 = pltpu.roll(gp, 1, 0)
    r2 = pltpu.roll(gp, 2, 0)
    if decode:
        t = row % dec_seq
        g1 = jnp.where(t >= 1, r1, 0.0) + h1_ref[...]
        g2 = jnp.where(t >= 2, r2, 0.0) + h2_ref[...]
        gp_ref[...] = gp
    else:
        tail = tail_ref[j]
        seq_start = (i % tiles_per_seq) == 0
        t6 = jnp.where(seq_start, 0.0, tail[keep - 2:keep - 1, :])
        t7 = jnp.where(seq_start, 0.0, tail[keep - 1:keep, :])
        g1 = jnp.where(row == 0, t7, r1)
        g2 = jnp.where(row == 0, t6, jnp.where(row == 1, t7, r2))
        tail_ref[j] = gp[tm - keep:, :]
        gp_ref[0] = gp[tm - keep:, :]
    cw = cw_ref[...]
    conv = cw[0:1, :] * g2 + cw[1:2, :] * g1 + cw[2:3, :] * gp + cb_ref[...]
    h = conv * _sigmoid(conv) * up
    acc_ref[...] += _dot(h.astype(BF16), wd_ref[...])

    @pl.when(j == pl.num_programs(1) - 1)
    def _():
        y_ref[...] = _layer_norm(alpha * x_ref[...] + acc_ref[...], g_ref[...], b_ref[...])


def _ffn(x1, lw, *, tm, tf, alpha, seq_len=None, hist=None, dec_seq=None):
    m, d = x1.shape
    d_ff = lw["w_down"].shape[0]
    assert m % tm == 0 and d_ff % tf == 0
    nf = d_ff // tf
    decode = hist is not None
    in_specs = [pl.BlockSpec((tm, d), lambda i, j: (i, 0)),
                pl.BlockSpec((d, tf), lambda i, j: (0, j)),
                pl.BlockSpec((d, tf), lambda i, j: (0, nf + j)),
                pl.BlockSpec((tf, d), lambda i, j: (j, 0)),
                pl.BlockSpec((SUBLANES, tf), lambda i, j: (0, j)),
                pl.BlockSpec((1, tf), lambda i, j: (0, j)),
                pl.BlockSpec((1, d), lambda i, j: (0, 0)),
                pl.BlockSpec((1, d), lambda i, j: (0, 0))]
    args = [x1, lw["w_up"], lw["w_up"], lw["w_down"], lw["conv_w"], lw["conv_b"], lw["ln2_g"], lw["ln2_b"]]
    scratch = [pltpu.VMEM((tm, d), BF16), pltpu.VMEM((tm, d), F32)]
    if decode:
        assert m == tm
        in_specs += [pl.BlockSpec((tm, tf), lambda i, j: (0, j))] * 2
        args += list(hist)
        gp_shape = jax.ShapeDtypeStruct((m, d_ff), F32)
        gp_spec = pl.BlockSpec((tm, tf), lambda i, j: (0, j))
        tiles_per_seq = None
    else:
        assert seq_len % tm == 0 and tm % SUBLANES == 0
        tiles_per_seq = seq_len // tm
        gp_shape = jax.ShapeDtypeStruct((m // tm, SUBLANES, d_ff), F32)
        gp_spec = pl.BlockSpec((1, SUBLANES, tf), lambda i, j: (i, 0, j))
        scratch += [pltpu.VMEM((nf, SUBLANES, tf), F32)]
    return pl.pallas_call(
        functools.partial(_ffn_kernel, alpha=alpha, tiles_per_seq=tiles_per_seq, dec_seq=dec_seq),
        grid=(m // tm, nf),
        in_specs=in_specs,
        out_specs=[pl.BlockSpec((tm, d), lambda i, j: (i, 0)), gp_spec],
        out_shape=[jax.ShapeDtypeStruct((m, d), F32), gp_shape],
        scratch_shapes=scratch,
        compiler_params=_params(("arbitrary", "arbitrary")),
        name="conv_ffn_ln",
    )(*args)


def _prep_layer(w_in, w_out, shift_mu, w0, w2, a0, a2, g2, k_k, k_a, r_k, gn_g, gn_b,
                ln1_g, ln1_b, ln2_g, ln2_b, w_up, conv_w, conv_b, w_down):
    row = lambda v: v.reshape(1, -1)
    zpad = Z_PAD - RWKV_PROJ
    w_in_r = jnp.concatenate([jnp.pad(w_in[:, ATT_END:], ((0, 0), (0, zpad))), w_in[:, :ATT_END]], axis=1)
    mu = jnp.pad(shift_mu, (0, zpad))
    zero_row = jnp.zeros((RWKV_WIDTH,), F32)
    return {
        "w_in": w_in_r.astype(BF16),
        "w_out_att": w_out[:ATT_WIDTH].astype(BF16),
        "w_out_rw": w_out[ATT_WIDTH:].astype(BF16),
        "mu_rkv": jnp.pad(mu[:LORA_COL].reshape(3, RWKV_WIDTH), ((0, SUBLANES - 3), (0, 0))),
        "mu_lora": row(mu[LORA_COL:LORA_COL + LORA_BLOCK]),
        "rwkv_par": jnp.stack([w0, a0, k_k, k_a, r_k, gn_g, gn_b, zero_row]),
        "w2": jnp.pad(w2, ((0, LANES - D_DECAY_LORA), (0, 0))).astype(BF16),
        "a2": jnp.pad(a2, ((D_DECAY_LORA, LANES - D_DECAY_LORA - D_AAA_LORA), (0, 0))).astype(BF16),
        "g2": jnp.pad(g2, ((0, GATE_LORA_PAD - D_GATE_LORA), (0, 0))).astype(BF16),
        "ln1_g": row(ln1_g), "ln1_b": row(ln1_b), "ln2_g": row(ln2_g), "ln2_b": row(ln2_b),
        "w_up": w_up.astype(BF16),
        "conv_w": jnp.pad(conv_w, ((0, SUBLANES - CONV_W), (0, 0))),
        "conv_b": row(conv_b),
        "w_down": w_down.astype(BF16),
    }


def _pick_tile(n, cap):
    t = min(n, cap)
    while n % t:
        t //= 2
    return t


def _prompt_layer(x, lw, alpha):
    b, s, d = x.shape
    d_ff = lw["w_down"].shape[0]
    pos = jnp.arange(s, dtype=jnp.int32)
    proj, kh, vt = _in_proj(x, lw["w_in"], pos, tm=_pick_tile(s, 1024), attn_extras=True)
    att = _moba_prompt(proj, kh, vt)
    rw, wkv = _rwkv(proj, jnp.zeros((b, 1, Z_PAD), F32),
                    jnp.zeros((b, N_RWKV_HEADS, HEAD_DIM, HEAD_DIM), F32), lw,
                    tblk=_pick_tile(s, 512), t_real=None)
    x1 = _out_ln(att.reshape(b * s, ATT_WIDTH), rw.reshape(b * s, RWKV_WIDTH), x.reshape(b * s, d), lw,
                 tm=_pick_tile(s, 512), alpha=alpha)
    tm = _pick_tile(s, 512)
    y, gp_tail = _ffn(x1, lw, tm=tm, tf=512, alpha=alpha, seq_len=s)
    k_new = proj[:, :, K_COL:V_COL].reshape(b, s, N_KV_HEADS, HEAD_DIM)
    v_new = proj[:, :, V_COL:].reshape(b, s, N_KV_HEADS, HEAD_DIM)
    shift_new = proj[:, s - 1, :RWKV_PROJ]
    conv_new = gp_tail.reshape(b, s // tm, SUBLANES, d_ff)[:, -1, SUBLANES - (CONV_W - 1):, :]
    return y.reshape(b, s, d), k_new, v_new, wkv, shift_new, conv_new


def _decode_layer(x, cache_k, cache_v, page_table, shift0, wkv0, conv0, lw, alpha):
    bd, t, d = x.shape
    m = bd * t
    past = page_table.shape[1] * PAGE_SIZE
    pos = past + (jnp.arange(m, dtype=jnp.int32) % t)
    (proj,) = _in_proj(x.reshape(1, m, d), lw["w_in"], pos, tm=m, attn_extras=False)
    proj = proj.reshape(bd, t, PROJ_W)
    att = _moba_decode(proj, cache_k, cache_v, page_table, t)
    grp = _pick_tile(bd, RWKV_SEQS_PER_STEP)
    proj_pad = jnp.pad(proj, ((0, 0), (0, RWKV_CHUNK - t), (0, 0))).reshape(bd // grp, grp * RWKV_CHUNK, PROJ_W)
    shift_pad = jnp.pad(shift0, ((0, 0), (0, Z_PAD - RWKV_PROJ))).reshape(bd // grp, grp, Z_PAD)
    rw, wkv = _rwkv(proj_pad, shift_pad, wkv0, lw, tblk=grp * RWKV_CHUNK, t_real=t, independent=True)
    rw = rw.reshape(bd, RWKV_CHUNK, RWKV_WIDTH)[:, :t]
    x1 = _out_ln(att.reshape(m, ATT_WIDTH), rw.reshape(m, RWKV_WIDTH), x.reshape(m, d), lw, tm=m, alpha=alpha)
    d_ff = conv0.shape[-1]
    zeros = lambda n: jnp.zeros((bd, n, d_ff), F32)
    h1 = jnp.concatenate([conv0[:, 1:2], zeros(t - 1)], axis=1).reshape(m, d_ff)
    h2 = jnp.concatenate([conv0[:, 0:1], conv0[:, 1:2], zeros(t - 2)], axis=1).reshape(m, d_ff)
    y, gp = _ffn(x1, lw, tm=m, tf=512, alpha=alpha, hist=(h1, h2), dec_seq=t)
    k_new = proj[:, :, K_COL:V_COL].reshape(bd, t, N_KV_HEADS, HEAD_DIM)
    v_new = proj[:, :, V_COL:].reshape(bd, t, N_KV_HEADS, HEAD_DIM)
    shift_new = proj[:, t - 1, :RWKV_PROJ]
    conv_new = gp.reshape(bd, t, d_ff)[:, t - (CONV_W - 1):]
    return y.reshape(bd, t, d), k_new, v_new, wkv, shift_new, conv_new


def kernel(x_prompt, x_sample, cache_k, cache_v, page_table, state_wkv, state_shift, state_conv,
           w_in, w_out, shift_mu, w0, w2, a0, a2, g2, k_k, k_a, r_k, gn_g, gn_b,
           ln1_g, ln1_b, ln2_g, ln2_b, w_up, conv_w, conv_b, w_down):
    depth = w_in.shape[0]
    alpha = (2.0 * depth) ** 0.25
    assert x_sample.shape[1] >= CONV_W - 1
    yp, ys = x_prompt, x_sample
    outs_p, outs_s = [], []
    for l in range(depth):
        lw = _prep_layer(w_in[l], w_out[l], shift_mu[l], w0[l], w2[l], a0[l], a2[l], g2[l], k_k[l], k_a[l],
                         r_k[l], gn_g[l], gn_b[l], ln1_g[l], ln1_b[l], ln2_g[l], ln2_b[l],
                         w_up[l], conv_w[l], conv_b[l], w_down[l])
        yp, *rest_p = _prompt_layer(yp, lw, alpha)
        outs_p.append(rest_p)
        ys, *rest_s = _decode_layer(ys, cache_k[l], cache_v[l], page_table, state_shift[l], state_wkv[l],
                                    state_conv[l], lw, alpha)
        outs_s.append(rest_s)
    stack = lambda outs, n: jnp.stack([o[n] for o in outs])
    return (yp, ys,
            stack(outs_p, 0), stack(outs_p, 1), stack(outs_p, 2), stack(outs_p, 3), stack(outs_p, 4),
            stack(outs_s, 0), stack(outs_s, 1), stack(outs_s, 2), stack(outs_s, 3), stack(outs_s, 4))
```

```python
import functools

import jax
import jax.numpy as jnp
from jax import lax
from jax.experimental import pallas as pl
from jax.experimental.pallas import tpu as pltpu

F32 = jnp.float32
BF16 = jnp.bfloat16

HEAD_DIM = 64
N_ATT_HEADS = 16
N_KV_HEADS = 4
KV_GROUP = N_ATT_HEADS // N_KV_HEADS
ATT_WIDTH = N_ATT_HEADS * HEAD_DIM
KV_WIDTH = N_KV_HEADS * HEAD_DIM
ATT_END = ATT_WIDTH + 2 * KV_WIDTH
N_RWKV_HEADS = 16
RWKV_WIDTH = N_RWKV_HEADS * HEAD_DIM
ROPE_DIM = HEAD_DIM // 4
ROPE_THETA = 500000.0
MOBA_BLOCK = 256
MOBA_TOPK = 3
MOBA_QUERY_PARTS = 1
D_DECAY_LORA = 64
D_AAA_LORA = 64
D_GATE_LORA = 160
LORA_WIDTH = D_DECAY_LORA + D_AAA_LORA + D_GATE_LORA
RWKV_PROJ = 3 * RWKV_WIDTH + LORA_WIDTH
CONV_W = 3
LN_EPS = 1e-5
GN_EPS = 64e-5
NEG_INF = -1e30
PAGE_SIZE = 128

LANES = 128
SUBLANES = 8
VMEM_LIMIT_BYTES = 56 * 1024 * 1024

PROJ_TN = 512
Z_PAD = 3584
Q_COL = Z_PAD
K_COL = Q_COL + ATT_WIDTH
V_COL = K_COL + KV_WIDTH
PROJ_W = V_COL + KV_WIDTH
LORA_COL = 3 * RWKV_WIDTH
LORA_BLOCK = 384
GATE_LORA_PAD = 256

RWKV_CHUNK = HEAD_DIM
RWKV_SEQS_PER_STEP = 8
RWKV_STAGE_GROUP = 8
PAGES_PER_STEP = 16

_NN = (((1,), (0,)), ((), ()))
_NT = (((1,), (1,)), ((), ()))


def _dot(a, b, dims=_NN):
    return lax.dot_general(a, b, dims, preferred_element_type=F32)


def _split_bf16(x):
    hi = x.astype(BF16)
    lo = (x - hi.astype(F32)).astype(BF16)
    return hi, lo


def _dot3(a, b, dims=_NN):
    ah, al = _split_bf16(a)
    bh, bl = _split_bf16(b)
    return _dot(ah, bh, dims) + (_dot(ah, bl, dims) + _dot(al, bh, dims))


def _sigmoid(x):
    return 1.0 / (1.0 + jnp.exp(-x))


def _layer_norm(x, g, b):
    mu = jnp.mean(x, axis=-1, keepdims=True)
    d = x - mu
    var = jnp.mean(d * d, axis=-1, keepdims=True)
    return d * lax.rsqrt(var + LN_EPS) * g + b


def _params(sem):
    return pltpu.CompilerParams(dimension_semantics=sem, vmem_limit_bytes=VMEM_LIMIT_BYTES)


def _rope_tables(pos):
    half = ROPE_DIM // 2
    inv = jnp.power(ROPE_THETA, -(jnp.arange(half, dtype=F32) * 2.0 / ROPE_DIM))
    ang = pos.astype(F32)[:, None] * inv[None, :]
    cos = jnp.cos(ang)
    sin = jnp.sin(ang)
    t = pos.shape[0]
    one = jnp.ones((t, HEAD_DIM - ROPE_DIM), F32)
    z8 = jnp.zeros((t, half), F32)
    zr = jnp.zeros((t, HEAD_DIM - ROPE_DIM), F32)
    c64 = jnp.concatenate([cos, cos, one], axis=1)
    a64 = jnp.concatenate([-sin, z8, zr], axis=1)
    b64 = jnp.concatenate([z8, sin, zr], axis=1)
    rep = LANES // HEAD_DIM
    return tuple(jnp.tile(m, (1, rep)) for m in (c64, a64, b64))


def _in_proj_kernel(x_ref, w_ref, cos_ref, sa_ref, sb_ref, proj_ref, *rest, attn_extras):
    if attn_extras:
        kh_ref, vt_ref, xb_ref = rest
    else:
        (xb_ref,) = rest
    j = pl.program_id(2)
    n_z = Z_PAD // PROJ_TN
    n_q = ATT_WIDTH // PROJ_TN
    half = ROPE_DIM // 2

    @pl.when(j == 0)
    def _():
        xb_ref[...] = x_ref[0].astype(BF16)

    acc = _dot(xb_ref[...], w_ref[...])

    def rope(slab):
        return (slab * cos_ref[...] + pltpu.roll(slab, LANES - half, 1) * sa_ref[...]
                + pltpu.roll(slab, half, 1) * sb_ref[...])

    @pl.when(j < n_z)
    def _():
        proj_ref[0] = acc

    @pl.when((j >= n_z) & (j < n_z + n_q))
    def _():
        for s in range(PROJ_TN // LANES):
            proj_ref[0, :, s * LANES:(s + 1) * LANES] = rope(acc[:, s * LANES:(s + 1) * LANES])

    @pl.when(j == n_z + n_q)
    def _():
        k_slabs = []
        for s in range(KV_WIDTH // LANES):
            ks = rope(acc[:, s * LANES:(s + 1) * LANES])
            proj_ref[0, :, s * LANES:(s + 1) * LANES] = ks
            k_slabs.append(ks)
        v = acc[:, KV_WIDTH:]
        proj_ref[0, :, KV_WIDTH:] = v
        if attn_extras:
            k = jnp.concatenate(k_slabs, axis=1)
            for c in range(N_KV_HEADS):
                kh_ref[0, c] = k[:, c * HEAD_DIM:(c + 1) * HEAD_DIM].astype(BF16)
            vt_ref[0] = v.T.astype(BF16)


def _in_proj(x, w_b, pos, *, tm, attn_extras):
    b, s, d = x.shape
    assert s % tm == 0 and PROJ_W % PROJ_TN == 0
    cos, sa, sb = _rope_tables(pos)
    tab_spec = pl.BlockSpec((tm, LANES), lambda bi, i, j: (i, 0))
    out_shape = [jax.ShapeDtypeStruct((b, s, PROJ_W), F32)]
    out_specs = [pl.BlockSpec((1, tm, PROJ_TN), lambda bi, i, j: (bi, i, j))]
    if attn_extras:
        out_shape += [jax.ShapeDtypeStruct((b, N_KV_HEADS, s, HEAD_DIM), BF16),
                      jax.ShapeDtypeStruct((b, KV_WIDTH, s), BF16)]
        out_specs += [pl.BlockSpec((1, N_KV_HEADS, tm, HEAD_DIM), lambda bi, i, j: (bi, 0, i, 0)),
                      pl.BlockSpec((1, KV_WIDTH, tm), lambda bi, i, j: (bi, 0, i))]
    return pl.pallas_call(
        functools.partial(_in_proj_kernel, attn_extras=attn_extras),
        grid=(b, s // tm, PROJ_W // PROJ_TN),
        in_specs=[pl.BlockSpec((1, tm, d), lambda bi, i, j: (bi, i, 0)),
                  pl.BlockSpec((d, PROJ_TN), lambda bi, i, j: (0, j)),
                  tab_spec, tab_spec, tab_spec],
        out_specs=out_specs,
        out_shape=out_shape,
        scratch_shapes=[pltpu.VMEM((tm, d), BF16)],
        compiler_params=_params(("parallel", "parallel", "arbitrary")),
        name="in_proj",
    )(x, w_b, cos, sa, sb)


def _moba_prompt_kernel(q_ref, kh_ref, vt_ref, o_ref, kmean_ref, sel_ref, *, n_blk, topk):
    i = pl.program_id(2)
    blk = MOBA_BLOCK
    n_cand = n_blk - 1
    scale = HEAD_DIM ** -0.5

    @pl.when(i == 0)
    def _():
        kmean_ref[...] = jnp.zeros_like(kmean_ref)
        for n in range(n_blk):
            kmean_ref[n:n + 1, :] = jnp.mean(
                kh_ref[0, 0, n * blk:(n + 1) * blk, :].astype(F32), axis=0, keepdims=True)

    q_t = q_ref[0].T
    qs_t = jnp.concatenate([q_t[g * HEAD_DIM:(g + 1) * HEAD_DIM, :] for g in range(KV_GROUP)],
                           axis=1)

    gate = _dot3(kmean_ref[...], qs_t)
    n_idx = lax.broadcasted_iota(jnp.int32, gate.shape, 0)
    gate = jnp.where(n_idx < i, gate, NEG_INF)
    rank = jnp.zeros(gate.shape, F32)
    for m in range(n_cand):
        gm = gate[m:m + 1, :]
        beats = (gm > gate) | ((gm == gate) & (m < n_idx))
        rank = rank + beats.astype(F32)
    sel_ref[...] = ((rank < topk) & (n_idx < i)).astype(F32)

    qb = (qs_t * scale).astype(BF16)
    n_parts = MOBA_QUERY_PARTS
    pw = KV_GROUP * blk // n_parts
    qb_parts = [qb[:, h * pw:(h + 1) * pw] for h in range(n_parts)]

    row0 = pl.multiple_of(i * blk, blk)
    k_own = kh_ref[0, 0, pl.ds(row0, blk), :]
    v_own = vt_ref[0, :, pl.ds(row0, blk)]
    kpos = lax.broadcasted_iota(jnp.int32, (blk, pw), 0)
    qpos = lax.broadcasted_iota(jnp.int32, (blk, pw), 1) & (blk - 1)
    causal = kpos <= qpos
    carry0 = []
    for h in range(n_parts):
        s_t = jnp.where(causal, _dot(k_own, qb_parts[h]), NEG_INF)
        m0 = jnp.max(s_t, axis=0, keepdims=True)
        p = jnp.exp(s_t - m0)
        carry0 += [m0, jnp.sum(p, axis=0, keepdims=True), _dot(v_own, p.astype(BF16))]

    def past_block(n, carry):
        r0 = pl.multiple_of(n * blk, blk)
        k_n = kh_ref[0, 0, pl.ds(r0, blk), :]
        v_n = vt_ref[0, :, pl.ds(r0, blk)]
        sel_row = sel_ref[pl.ds(n, 1), :]
        scores = [_dot(k_n, qb_parts[h]) for h in range(n_parts)]
        out = []
        for h in range(n_parts):
            m, l, acc = carry[3 * h:3 * h + 3]
            s_n = jnp.where(sel_row[:, h * pw:(h + 1) * pw] > 0.5, scores[h], NEG_INF)
            m_new = jnp.maximum(m, jnp.max(s_n, axis=0, keepdims=True))
            alpha = jnp.exp(m - m_new)
            p_n = jnp.exp(s_n - m_new)
            l = l * alpha + jnp.sum(p_n, axis=0, keepdims=True)
            acc = acc * alpha + _dot(v_n, p_n.astype(BF16))
            out += [m_new, l, acc]
        return tuple(out)

    carry = lax.fori_loop(0, i, past_block, tuple(carry0))
    o = jnp.concatenate([carry[3 * h + 2] / carry[3 * h + 1] for h in range(n_parts)], axis=1)
    o4 = jnp.concatenate([o[:, g * blk:(g + 1) * blk] for g in range(KV_GROUP)], axis=0)
    o_ref[0] = o4.T.astype(o_ref.dtype)


def _moba_prompt(proj, kh, vt):
    b, s, _ = proj.shape
    assert s % MOBA_BLOCK == 0
    n_blk = s // MOBA_BLOCK
    topk = min(MOBA_TOPK, n_blk - 1)
    gw = KV_GROUP * HEAD_DIM
    n_rows = -(-n_blk // SUBLANES) * SUBLANES
    return pl.pallas_call(
        functools.partial(_moba_prompt_kernel, n_blk=n_blk, topk=topk),
        grid=(b, N_KV_HEADS, n_blk),
        in_specs=[pl.BlockSpec((1, MOBA_BLOCK, gw), lambda bi, c, i: (bi, i, Q_COL // gw + c)),
                  pl.BlockSpec((1, 1, s, HEAD_DIM), lambda bi, c, i: (bi, c, 0, 0)),
                  pl.BlockSpec((1, HEAD_DIM, s), lambda bi, c, i: (bi, c, 0))],
        out_specs=pl.BlockSpec((1, MOBA_BLOCK, gw), lambda bi, c, i: (bi, i, c)),
        out_shape=jax.ShapeDtypeStruct((b, s, ATT_WIDTH), BF16),
        scratch_shapes=[pltpu.VMEM((n_rows, HEAD_DIM), F32),
                        pltpu.VMEM((n_rows, KV_GROUP * MOBA_BLOCK), F32)],
        compiler_params=_params(("parallel", "parallel", "arbitrary")),
        name="moba_prompt",
    )(proj, kh, vt)


def _moba_decode_scores_kernel(pt_ref, wt_ref, *rest, n_new):
    del pt_ref
    r_pages = PAGES_PER_STEP
    k_refs = rest[:r_pages]
    s_ref, gate_ref, bmax_ref = rest[r_pages:]
    j = pl.program_id(1)
    scale = HEAD_DIM ** -0.5
    nq = N_ATT_HEADS * n_new
    pages_per_blk = MOBA_BLOCK // PAGE_SIZE
    blks_per_step = r_pages // pages_per_blk
    lane = lax.broadcasted_iota(jnp.int32, gate_ref.shape[1:], 1)

    @pl.when(j == 0)
    def _():
        gate_ref[0] = jnp.zeros(gate_ref.shape[1:], F32)
        bmax_ref[0] = jnp.full(bmax_ref.shape[1:], NEG_INF, F32)

    wt2 = jnp.concatenate(_split_bf16(wt_ref[0]), axis=0)
    gate = gate_ref[0]
    bmax = bmax_ref[0]
    for r in range(r_pages):
        s2 = _dot(wt2, k_refs[r][0].astype(BF16))
        raw = s2[:nq] + s2[nq:]
        s_ref[0, :, r * PAGE_SIZE:(r + 1) * PAGE_SIZE] = raw * scale
        first = r % pages_per_blk == 0
        blk_sum = raw if first else blk_sum + raw
        blk_max = raw if first else jnp.maximum(blk_max, raw)
        if r % pages_per_blk == pages_per_blk - 1:
            here = lane == j * blks_per_step + r // pages_per_blk
            gate = jnp.where(here, jnp.sum(blk_sum, axis=1, keepdims=True), gate)
            bmax = jnp.where(here, jnp.max(blk_max, axis=1, keepdims=True) * scale, bmax)
    gate_ref[0] = gate
    bmax_ref[0] = bmax


def _moba_decode_values_kernel(pt_ref, wt_ref, kvn_ref, gate_ref, bmax_ref, s_ref, *rest,
                               n_cand, n_new):
    del pt_ref
    r_pages = PAGES_PER_STEP
    v_refs = rest[:r_pages]
    o_ref = rest[r_pages]
    sel_ref, acc_ref, l_ref, m_ref = rest[r_pages + 1:]
    n_cand_pad = gate_ref.shape[2]
    j = pl.program_id(1)
    scale = HEAD_DIM ** -0.5
    pages_per_blk = MOBA_BLOCK // PAGE_SIZE
    blks_per_step = r_pages // pages_per_blk

    @pl.when(j == 0)
    def _():
        lane = lax.broadcasted_iota(jnp.int32, gate_ref.shape[1:], 1)
        lane_f = lane.astype(F32)
        gate = jnp.where(lane < n_cand, gate_ref[0] * (1.0 / MOBA_BLOCK), NEG_INF)
        sel = lane < 0
        for _ in range(min(MOBA_TOPK, n_cand)):
            top = jnp.max(gate, axis=1, keepdims=True)
            first_top = jnp.min(jnp.where(gate == top, lane_f, float(n_cand_pad)), axis=1, keepdims=True)
            pick = lane_f == first_top
            sel = sel | pick
            gate = jnp.where(pick, -jnp.inf, gate)
        sel = sel & (lane < n_cand)
        sel_ref[...] = sel.astype(F32)

        kvn = kvn_ref[0]
        k_new = kvn[:, :KV_WIDTH].astype(BF16)
        v_new = kvn[:, KV_WIDTH:].astype(BF16)
        s_new = _dot((wt_ref[0] * scale).astype(BF16), k_new, _NT)
        t_q = lax.broadcasted_iota(jnp.int32, s_new.shape, 0) % n_new
        t_k = lax.broadcasted_iota(jnp.int32, s_new.shape, 1)
        ok_new = (t_k <= t_q) & (t_k < n_new)
        m = jnp.maximum(jnp.max(jnp.where(ok_new, s_new, NEG_INF), axis=1, keepdims=True),
                        jnp.max(jnp.where(sel, bmax_ref[0], NEG_INF), axis=1, keepdims=True))
        p_new = jnp.where(ok_new, jnp.exp(s_new - m), 0.0)
        m_ref[...] = jnp.broadcast_to(m, m_ref.shape)
        l_lane = lax.broadcasted_iota(jnp.int32, l_ref.shape, 1)
        l_ref[...] = jnp.where(l_lane == 0, jnp.sum(p_new, axis=1, keepdims=True), 0.0)
        acc_ref[...] = _dot(p_new.astype(BF16), v_new)

    src = lax.broadcasted_iota(jnp.int32, (n_cand_pad, LANES), 0)
    dst = lax.broadcasted_iota(jnp.int32, (n_cand_pad, LANES), 1)
    onehot = ((src == j * blks_per_step + dst) & (dst < blks_per_step)).astype(BF16)
    sel_here = _dot(sel_ref[...].astype(BF16), onehot)
    m = m_ref[:, 0:1]
    acc = acc_ref[...]
    l_part = l_ref[...]
    for r in range(r_pages):
        blk = r // pages_per_blk
        sb = s_ref[0, :, r * PAGE_SIZE:(r + 1) * PAGE_SIZE]
        pb = jnp.where(sel_here[:, blk:blk + 1] > 0.5, jnp.exp(sb - m), 0.0)
        l_part = l_part + pb
        acc = acc + _dot(pb.astype(BF16), v_refs[r][0].astype(BF16), _NT)
    acc_ref[...] = acc
    l_ref[...] = l_part

    @pl.when(j == pl.num_programs(1) - 1)
    def _():
        o_ref[0] = acc / jnp.sum(l_part, axis=1, keepdims=True)


def _moba_decode(proj_d, cache_k, cache_v, page_table, n_new):
    bd = proj_d.shape[0]
    n_pages = page_table.shape[1]
    past = n_pages * PAGE_SIZE
    assert past % MOBA_BLOCK == 0 and n_pages % PAGES_PER_STEP == 0 and n_new <= SUBLANES
    n_cand = past // MOBA_BLOCK
    n_steps = n_pages // PAGES_PER_STEP
    n_phys = cache_k.shape[0]
    nq = N_ATT_HEADS * n_new
    ck = jnp.transpose(cache_k, (0, 2, 3, 1)).reshape(n_phys, KV_WIDTH, PAGE_SIZE)
    cv = jnp.transpose(cache_v, (0, 2, 3, 1)).reshape(n_phys, KV_WIDTH, PAGE_SIZE)

    q = proj_d[:, :, Q_COL:Q_COL + ATT_WIDTH].reshape(bd, n_new, N_KV_HEADS, KV_GROUP, HEAD_DIM)
    q = jnp.transpose(q, (0, 2, 3, 1, 4)).reshape(bd, N_KV_HEADS, KV_GROUP * n_new, 1, HEAD_DIM)
    eye = jnp.eye(N_KV_HEADS, dtype=F32)[None, :, None, :, None]
    wt = (q * eye).reshape(bd, nq, KV_WIDTH)
    kvn = jnp.pad(proj_d[:, :, K_COL:], ((0, 0), (0, SUBLANES - n_new), (0, 0)))

    def page_map(r):
        return lambda b, j, pt: (pt[b, j * PAGES_PER_STEP + r], 0, 0)

    step_keys = PAGES_PER_STEP * PAGE_SIZE
    n_cand_pad = -(-n_cand // LANES) * LANES
    per_seq = lambda w: pl.BlockSpec((1, nq, w), lambda b, j, pt: (b, 0, 0))
    page_specs = [pl.BlockSpec((1, KV_WIDTH, PAGE_SIZE), page_map(r)) for r in range(PAGES_PER_STEP)]
    score_spec = pl.BlockSpec((1, nq, step_keys), lambda b, j, pt: (b, 0, j))
    scores, gate, bmax = pl.pallas_call(
        functools.partial(_moba_decode_scores_kernel, n_new=n_new),
        grid_spec=pltpu.PrefetchScalarGridSpec(
            num_scalar_prefetch=1,
            grid=(bd, n_steps),
            in_specs=[per_seq(KV_WIDTH)] + page_specs,
            out_specs=[score_spec, per_seq(n_cand_pad), per_seq(n_cand_pad)]),
        out_shape=[jax.ShapeDtypeStruct((bd, nq, past), F32),
                   jax.ShapeDtypeStruct((bd, nq, n_cand_pad), F32),
                   jax.ShapeDtypeStruct((bd, nq, n_cand_pad), F32)],
        compiler_params=_params(("parallel", "arbitrary")),
        name="moba_decode_scores",
    )(page_table, wt, *([ck] * PAGES_PER_STEP))
    out = pl.pallas_call(
        functools.partial(_moba_decode_values_kernel, n_cand=n_cand, n_new=n_new),
        grid_spec=pltpu.PrefetchScalarGridSpec(
            num_scalar_prefetch=1,
            grid=(bd, n_steps),
            in_specs=[per_seq(KV_WIDTH),
                      pl.BlockSpec((1, SUBLANES, 2 * KV_WIDTH), lambda b, j, pt: (b, 0, 0)),
                      per_seq(n_cand_pad), per_seq(n_cand_pad), score_spec] + page_specs,
            out_specs=per_seq(KV_WIDTH),
            scratch_shapes=[pltpu.VMEM((nq, n_cand_pad), F32),
                            pltpu.VMEM((nq, KV_WIDTH), F32),
                            pltpu.VMEM((nq, LANES), F32),
                            pltpu.VMEM((nq, LANES), F32)]),
        out_shape=jax.ShapeDtypeStruct((bd, nq, KV_WIDTH), F32),
        compiler_params=_params(("parallel", "arbitrary")),
        name="moba_decode_values",
    )(page_table, wt, kvn, gate, bmax, scores, *([cv] * PAGES_PER_STEP))
    o = out.reshape(bd, N_KV_HEADS, KV_GROUP, n_new, N_KV_HEADS, HEAD_DIM)
    o = jnp.stack([o[:, c, :, :, c, :] for c in range(N_KV_HEADS)], axis=1)
    o = jnp.transpose(o, (0, 3, 1, 2, 4)).reshape(bd, n_new, ATT_WIDTH)
    return o.astype(BF16)


def _rwkv_kernel(zr_ref, zk_ref, zv_ref, zl_ref, sr_ref, sk_ref, sv_ref, sl_ref,
                 mu_ref, mul_ref, par_ref, w2_ref, a2_ref, g2_ref, s0_ref,
                 rw_ref, wkv_ref,
                 state_ref, pr_ref, pk_ref, pv_ref, plo_ref, *, t_real, independent):
    tb = pl.program_id(2)
    n_tb = pl.num_programs(2)
    tblk = zr_ref.shape[1]
    c_len = RWKV_CHUNK
    width = LANES
    lane = lax.broadcasted_iota(jnp.int32, (1, width), 1)
    head_a = lane < HEAD_DIM
    row = lax.broadcasted_iota(jnp.int32, (tblk, 1), 0)

    def block_diag(sa, sb):
        z = jnp.zeros((HEAD_DIM, HEAD_DIM), F32)
        return jnp.concatenate([jnp.concatenate([sa, z], axis=1),
                                jnp.concatenate([z, sb], axis=1)], axis=0)

    if independent:
        def token_shift(z, first_ref, prev_ref, mu):
            z_prev = pltpu.roll(z, 1, 0)
            first = first_ref[0]
            for c in range(tblk // c_len):
                z_prev = jnp.where(row == c * c_len, first[c:c + 1, :], z_prev)
            return z + (z_prev - z) * mu
    else:
        @pl.when(tb == 0)
        def _():
            state_ref[...] = block_diag(s0_ref[0, 0], s0_ref[0, 1])
            pr_ref[0:1, :] = sr_ref[0]
            pk_ref[0:1, :] = sk_ref[0]
            pv_ref[0:1, :] = sv_ref[0]
            plo_ref[0:1, :] = sl_ref[0]

        def token_shift(z, first_ref, prev_ref, mu):
            z_prev = jnp.where(row == 0, prev_ref[0:1, :], pltpu.roll(z, 1, 0))
            prev_ref[0:1, :] = z[tblk - 1:tblk, :]
            return z + (z_prev - z) * mu

    def seg_sum(x):
        sa = jnp.sum(jnp.where(head_a, x, 0.0), axis=1, keepdims=True)
        sb = jnp.sum(jnp.where(head_a, 0.0, x), axis=1, keepdims=True)
        return jnp.where(head_a, sa, sb)

    r = token_shift(zr_ref[0], sr_ref, pr_ref, mu_ref[0:1, :])
    k = token_shift(zk_ref[0], sk_ref, pk_ref, mu_ref[1:2, :])
    v = token_shift(zv_ref[0], sv_ref, pv_ref, mu_ref[2:3, :])
    lo = token_shift(zl_ref[0], sl_ref, plo_ref, mul_ref[...])
    par = par_ref[...]
    w0, a0, k_k, k_a, r_k, gn_g, gn_b = (par[n:n + 1, :] for n in range(7))

    lo_wa = lo[:, :LANES]
    nx = -(w0 + _dot(jnp.tanh(lo_wa).astype(BF16), w2_ref[...]))
    softplus = jnp.maximum(nx, 0.0) + jnp.log(1.0 + jnp.exp(-jnp.abs(nx)))
    logw = -jnp.exp(-softplus - 0.5)
    a = _sigmoid(a0 + _dot(lo_wa.astype(BF16), a2_ref[...]))
    g = _dot(_sigmoid(lo[:, LANES:]).astype(BF16), g2_ref[...])
    kk = k * k_k
    kk = kk / jnp.maximum(jnp.sqrt(seg_sum(kk * kk)), 1e-12)
    k = k * (1.0 + (a - 1.0) * k_a)
    bonus = seg_sum(r * k * r_k) * v

    if t_real is not None:
        valid = ((row & (c_len - 1)) if independent else (tb * tblk + row)) < t_real
        logw = jnp.where(valid, logw, 0.0)
        kk = jnp.where(valid, kk, 0.0)
        k = jnp.where(valid, k, 0.0)
        v_in = jnp.where(valid, v, 0.0)
    else:
        v_in = v

    def pair_rows(x):
        return jnp.concatenate([jnp.where(head_a, x, 0.0), jnp.where(head_a, 0.0, x)], axis=0)

    ri = lax.broadcasted_iota(jnp.int32, (c_len, 2 * c_len), 0)
    ci = lax.broadcasted_iota(jnp.int32, (c_len, 2 * c_len), 1)
    first = ci < c_len
    tcol = ci & (c_len - 1)
    strict = tcol < ri
    incl = tcol <= ri
    eye2 = (tcol == ri).astype(F32)
    bi = lax.broadcasted_iota(jnp.int32, (width, width), 0) < HEAD_DIM
    bj = lax.broadcasted_iota(jnp.int32, (width, width), 1) < HEAD_DIM
    same_head = bi == bj
    live = c_len if t_real is None else min(t_real, c_len)
    n_levels = max((live - 1).bit_length() - 1, 0)
    shift = c_len.bit_length() - 1
    chunks = [slice(c * c_len, (c + 1) * c_len) for c in range(tblk // c_len)]
    bf = lambda x: x.astype(BF16)

    tr = lax.broadcasted_iota(jnp.int32, (tblk, tblk), 0)
    tc = lax.broadcasted_iota(jnp.int32, (tblk, tblk), 1)
    tri = ((tr >= tc) & ((tr >> shift) == (tc >> shift))).astype(BF16)
    lh = logw.astype(BF16)
    l1 = logw - lh.astype(F32)
    lm = l1.astype(BF16)
    ll = (l1 - lm.astype(F32)).astype(BF16)
    lc = _dot(tri, lh) + (_dot(tri, lm) + _dot(tri, ll))
    g_in = jnp.exp(lc)
    g_inv = jnp.exp(-lc)
    a_all = -kk * jnp.exp(lc - logw)
    b_all = kk * a * g_inv
    k_all = k * g_inv
    r_all = r * g_in

    def group_terms(sls):
        a_t = [a_all[sl] for sl in sls]
        r_t = [r_all[sl] for sl in sls]
        b_t = [b_all[sl] for sl in sls]
        k_t = [k_all[sl] for sl in sls]
        v_c = [v_in[sl] for sl in sls]
        g_end = [g_in[sl.stop - 1:sl.stop, :] for sl in sls]
        a_pair = [pair_rows(bf(x)) for x in a_t]
        s4 = [_dot(jnp.concatenate([ap, pair_rows(bf(rt))], axis=0),
                   bf(jnp.concatenate([bt, kt], axis=0)), _NT)
              for ap, rt, bt, kt in zip(a_pair, r_t, b_t, k_t)]
        m_ab, m_ak, m_rb, m_rk = [], [], [], []
        for x in s4:
            sa_, sb_ = x[0:c_len], x[c_len:2 * c_len]
            ra_, rb_ = x[2 * c_len:3 * c_len], x[3 * c_len:4 * c_len]
            m_ab.append(jnp.where(strict, jnp.where(first, sa_, pltpu.roll(sb_, c_len, 1)), 0.0))
            m_ak.append(jnp.where(strict, jnp.where(first, pltpu.roll(sa_, c_len, 1), sb_), 0.0))
            m_rb.append(jnp.where(incl, jnp.where(first, ra_, pltpu.roll(rb_, c_len, 1)), 0.0))
            m_rk.append(jnp.where(incl, jnp.where(first, pltpu.roll(ra_, c_len, 1), rb_), 0.0))

        t_inv = [m + eye2 for m in m_ab]
        if n_levels > 0:
            l_pow = [bf(m) for m in m_ab]
            l_pow = [bf(_dot(lp, pair_rows(lp))) for lp in l_pow]
            for level in range(n_levels):
                qp = [_dot(lp, jnp.concatenate([pair_rows(lp), pair_rows(bf(ti))], axis=1))
                      for lp, ti in zip(l_pow, t_inv)]
                t_inv = [ti + x[:, width:] for ti, x in zip(t_inv, qp)]
                if level + 1 < n_levels:
                    l_pow = [bf(x[:, :width]) for x in qp]

        v_pair = [pair_rows(bf(x)) for x in v_c]
        mak_v = [_dot(bf(m), vp) for m, vp in zip(m_ak, v_pair)]
        tw = [_dot(bf(ti), jnp.concatenate([ap, pair_rows(bf(mv))], axis=1))
              for ti, ap, mv in zip(t_inv, a_pair, mak_v)]
        w_t = [x[:, :width] for x in tw]
        v_hat = [x[:, width:] for x in tw]
        mw = [_dot(bf(jnp.concatenate([mb, mk], axis=1)),
                   jnp.concatenate(
                       [jnp.concatenate([pair_rows(bf(w)), pair_rows(bf(vh))], axis=1),
                        jnp.concatenate([jnp.zeros_like(vp), vp], axis=1)], axis=0))
              for mb, mk, w, vh, vp in zip(m_rb, m_rk, w_t, v_hat, v_pair)]
        p_m = [rt + x[:, :width] for rt, x in zip(r_t, mw)]
        q_m = [x[:, width:] for x in mw]
        phi, psi = [], []
        for w, vh, vc, bt, kt, ge in zip(w_t, v_hat, v_c, b_t, k_t, g_end):
            bk = bf(jnp.concatenate([bt * ge, kt * ge], axis=0))
            left = jnp.concatenate(
                [jnp.concatenate([w, jnp.zeros_like(w)], axis=0),
                 jnp.concatenate([vh, vc], axis=0)], axis=1)
            pp = _dot(bf(left.T), bk)
            phi.append(bf(jnp.where(same_head, pp[:width], 0.0)))
            psi.append(jnp.where(same_head, pp[width:], 0.0))
        return list(zip(p_m, q_m, phi, psi, g_end))

    def advance(s, term):
        pm, qm, ph, ps, ge = term
        sb16 = bf(s)
        return _dot(bf(pm), sb16, _NT) + qm, s * ge + _dot(sb16, ph) + ps

    outs = []
    s = None if independent else state_ref[...]
    for g0 in range(0, len(chunks), RWKV_STAGE_GROUP):
        for c, term in enumerate(group_terms(chunks[g0:g0 + RWKV_STAGE_GROUP]), start=g0):
            if independent:
                o_c, s_c = advance(block_diag(s0_ref[c, 0], s0_ref[c, 1]), term)
                wkv_ref[c, 0] = s_c[:HEAD_DIM, :HEAD_DIM]
                wkv_ref[c, 1] = s_c[HEAD_DIM:, HEAD_DIM:]
            else:
                o_c, s = advance(s, term)
            outs.append(o_c)
    if not independent:
        state_ref[...] = s

        @pl.when(tb == n_tb - 1)
        def _():
            wkv_ref[0, 0] = s[:HEAD_DIM, :HEAD_DIM]
            wkv_ref[0, 1] = s[HEAD_DIM:, HEAD_DIM:]

    o = jnp.concatenate(outs, axis=0) if len(outs) > 1 else outs[0]
    inv_n = 1.0 / HEAD_DIM
    mu_o = seg_sum(o) * inv_n
    d = o - mu_o
    var = seg_sum(d * d) * inv_n
    on = d * lax.rsqrt(var + GN_EPS) * gn_g + gn_b
    rw_ref[0] = ((on + bonus) * g).astype(rw_ref.dtype)


def _rwkv(proj, shift0, wkv0, lw, *, tblk, t_real, independent=False):
    b, s, _ = proj.shape
    assert s % tblk == 0 and tblk % RWKV_CHUNK == 0
    n_pairs = RWKV_WIDTH // LANES
    rkv = RWKV_WIDTH // LANES
    lora_blk = LORA_COL // LORA_BLOCK
    n_first = shift0.shape[1]
    n_state = tblk // RWKV_CHUNK if independent else 1
    assert (not independent) or (s == tblk and n_first == n_state)

    def z_spec(width, col0):
        return pl.BlockSpec((1, tblk, width), lambda bi, p, t: (bi, t, col0(p)))

    def s_spec(width, col0):
        return pl.BlockSpec((1, n_first, width), lambda bi, p, t: (bi, 0, col0(p)))

    state_spec = pl.BlockSpec((n_state, 2, HEAD_DIM, HEAD_DIM), lambda bi, p, t: (bi, p, 0, 0))

    cols = [lambda p: p, lambda p: rkv + p, lambda p: 2 * rkv + p]
    in_specs = ([z_spec(LANES, c) for c in cols] + [z_spec(LORA_BLOCK, lambda p: lora_blk)]
                + [s_spec(LANES, c) for c in cols] + [s_spec(LORA_BLOCK, lambda p: lora_blk)]
                + [pl.BlockSpec((SUBLANES, LANES), lambda bi, p, t: (0, p)),
                   pl.BlockSpec((1, LORA_BLOCK), lambda bi, p, t: (0, 0)),
                   pl.BlockSpec((SUBLANES, LANES), lambda bi, p, t: (0, p)),
                   pl.BlockSpec((LANES, LANES), lambda bi, p, t: (0, p)),
                   pl.BlockSpec((LANES, LANES), lambda bi, p, t: (0, p)),
                   pl.BlockSpec((GATE_LORA_PAD, LANES), lambda bi, p, t: (0, p)),
                   state_spec])
    return pl.pallas_call(
        functools.partial(_rwkv_kernel, t_real=t_real, independent=independent),
        grid=(b, n_pairs, s // tblk),
        in_specs=in_specs,
        out_specs=[pl.BlockSpec((1, tblk, LANES), lambda bi, p, t: (bi, t, p)), state_spec],
        out_shape=[jax.ShapeDtypeStruct((b, s, RWKV_WIDTH), BF16),
                   jax.ShapeDtypeStruct((b * n_state, N_RWKV_HEADS, HEAD_DIM, HEAD_DIM), F32)],
        scratch_shapes=[pltpu.VMEM((LANES, LANES), F32),
                        pltpu.VMEM((SUBLANES, LANES), F32),
                        pltpu.VMEM((SUBLANES, LANES), F32),
                        pltpu.VMEM((SUBLANES, LANES), F32),
                        pltpu.VMEM((SUBLANES, LORA_BLOCK), F32)],
        compiler_params=_params(("parallel", "parallel", "arbitrary")),
        name="rwkv7",
    )(proj, proj, proj, proj, shift0, shift0, shift0, shift0,
      lw["mu_rkv"], lw["mu_lora"], lw["rwkv_par"], lw["w2"], lw["a2"], lw["g2"], wkv0)


def _out_ln_kernel(att_ref, rw_ref, x_ref, wa_ref, wr_ref, g_ref, b_ref, o_ref, *, alpha):
    mix = _dot(att_ref[...], wa_ref[...]) + _dot(rw_ref[...], wr_ref[...])
    o_ref[...] = _layer_norm(alpha * x_ref[...] + mix, g_ref[...], b_ref[...])


def _out_ln(att, rw, x, lw, *, tm, alpha):
    m, d = x.shape
    assert m % tm == 0
    row = lambda w: pl.BlockSpec((tm, w), lambda i: (i, 0))
    full = lambda r, w: pl.BlockSpec((r, w), lambda i: (0, 0))
    return pl.pallas_call(
        functools.partial(_out_ln_kernel, alpha=alpha),
        grid=(m // tm,),
        in_specs=[row(ATT_WIDTH), row(RWKV_WIDTH), row(d),
                  full(ATT_WIDTH, d), full(RWKV_WIDTH, d), full(1, d), full(1, d)],
        out_specs=row(d),
        out_shape=jax.ShapeDtypeStruct((m, d), F32),
        compiler_params=_params(("parallel",)),
        name="out_proj_ln",
    )(att, rw, x, lw["w_out_att"], lw["w_out_rw"], lw["ln1_g"], lw["ln1_b"])


def _ffn_kernel(x_ref, wg_ref, wu_ref, wd_ref, cw_ref, cb_ref, g_ref, b_ref, *rest,
                alpha, tiles_per_seq, dec_seq):
    decode = dec_seq is not None
    if decode:
        h1_ref, h2_ref, y_ref, gp_ref, xb_ref, acc_ref = rest
    else:
        y_ref, gp_ref, xb_ref, acc_ref, tail_ref = rest
    i = pl.program_id(0)
    j = pl.program_id(1)
    tm = x_ref.shape[0]
    keep = SUBLANES

    @pl.when(j == 0)
    def _():
        xb_ref[...] = x_ref[...].astype(BF16)
        acc_ref[...] = jnp.zeros_like(acc_ref)

    xb = xb_ref[...]
    gp = _dot(xb, wg_ref[...])
    up = _dot(xb, wu_ref[...])
    row = lax.broadcasted_iota(jnp.int32, (tm, 1), 0)
    r1 = pltpu.roll(gp, 1, 0)
    r2 = pltpu.roll(gp, 2, 0)
    if decode:
        t = row % dec_seq
        g1 = jnp.where(t >= 1, r1, 0.0) + h1_ref[...]
        g2 = jnp.where(t >= 2, r2, 0.0) + h2_ref[...]
        gp_ref[...] = gp
    else:
        tail = tail_ref[j]
        seq_start = (i % tiles_per_seq) == 0
        t6 = jnp.where(seq_start, 0.0, tail[keep - 2:keep - 1, :])
        t7 = jnp.where(seq_start, 0.0, tail[keep - 1:keep, :])
        g1 = jnp.where(row == 0, t7, r1)
        g2 = jnp.where(row == 0, t6, jnp.where(row == 1, t7, r2))
        tail_ref[j] = gp[tm - keep:, :]
        gp_ref[0] = gp[tm - keep:, :]
    cw = cw_ref[...]
    conv = cw[0:1, :] * g2 + cw[1:2, :] * g1 + cw[2:3, :] * gp + cb_ref[...]
    h = conv * _sigmoid(conv) * up
    acc_ref[...] += _dot(h.astype(BF16), wd_ref[...])

    @pl.when(j == pl.num_programs(1) - 1)
    def _():
        y_ref[...] = _layer_norm(alpha * x_ref[...] + acc_ref[...], g_ref[...], b_ref[...])


def _ffn(x1, lw, *, tm, tf, alpha, seq_len=None, hist=None, dec_seq=None):
    m, d = x1.shape
    d_ff = lw["w_down"].shape[0]
    assert m % tm == 0 and d_ff % tf == 0
    nf = d_ff // tf
    decode = hist is not None
    in_specs = [pl.BlockSpec((tm, d), lambda i, j: (i, 0)),
                pl.BlockSpec((d, tf), lambda i, j: (0, j)),
                pl.BlockSpec((d, tf), lambda i, j: (0, nf + j)),
                pl.BlockSpec((tf, d), lambda i, j: (j, 0)),
                pl.BlockSpec((SUBLANES, tf), lambda i, j: (0, j)),
                pl.BlockSpec((1, tf), lambda i, j: (0, j)),
                pl.BlockSpec((1, d), lambda i, j: (0, 0)),
                pl.BlockSpec((1, d), lambda i, j: (0, 0))]
    args = [x1, lw["w_up"], lw["w_up"], lw["w_down"], lw["conv_w"], lw["conv_b"], lw["ln2_g"], lw["ln2_b"]]
    scratch = [pltpu.VMEM((tm, d), BF16), pltpu.VMEM((tm, d), F32)]
    if decode:
        assert m == tm
        in_specs += [pl.BlockSpec((tm, tf), lambda i, j: (0, j))] * 2
        args += list(hist)
        gp_shape = jax.ShapeDtypeStruct((m, d_ff), F32)
        gp_spec = pl.BlockSpec((tm, tf), lambda i, j: (0, j))
        tiles_per_seq = None
    else:
        assert seq_len % tm == 0 and tm % SUBLANES == 0
        tiles_per_seq = seq_len // tm
        gp_shape = jax.ShapeDtypeStruct((m // tm, SUBLANES, d_ff), F32)
        gp_spec = pl.BlockSpec((1, SUBLANES, tf), lambda i, j: (i, 0, j))
        scratch += [pltpu.VMEM((nf, SUBLANES, tf), F32)]
    return pl.pallas_call(
        functools.partial(_ffn_kernel, alpha=alpha, tiles_per_seq=tiles_per_seq, dec_seq=dec_seq),
        grid=(m // tm, nf),
        in_specs=in_specs,
        out_specs=[pl.BlockSpec((tm, d), lambda i, j: (i, 0)), gp_spec],
        out_shape=[jax.ShapeDtypeStruct((m, d), F32), gp_shape],
        scratch_shapes=scratch,
        compiler_params=_params(("arbitrary", "arbitrary")),
        name="conv_ffn_ln",
    )(*args)


def _prep_layer(w_in, w_out, shift_mu, w0, w2, a0, a2, g2, k_k, k_a, r_k, gn_g, gn_b,
                ln1_g, ln1_b, ln2_g, ln2_b, w_up, conv_w, conv_b, w_down):
    row = lambda v: v.reshape(1, -1)
    zpad = Z_PAD - RWKV_PROJ
    w_in_b = w_in.astype(BF16)
    w_in_r = jnp.concatenate([jnp.pad(w_in_b[:, ATT_END:], ((0, 0), (0, zpad))), w_in_b[:, :ATT_END]], axis=1)
    mu = jnp.pad(shift_mu, (0, zpad))
    zero_row = jnp.zeros((RWKV_WIDTH,), F32)
    return {
        "w_in": w_in_r,
        "w_out_att": w_out[:ATT_WIDTH].astype(BF16),
        "w_out_rw": w_out[ATT_WIDTH:].astype(BF16),
        "mu_rkv": jnp.pad(mu[:LORA_COL].reshape(3, RWKV_WIDTH), ((0, SUBLANES - 3), (0, 0))),
        "mu_lora": row(mu[LORA_COL:LORA_COL + LORA_BLOCK]),
        "rwkv_par": jnp.stack([w0, a0, k_k, k_a, r_k, gn_g, gn_b, zero_row]),
        "w2": jnp.pad(w2, ((0, LANES - D_DECAY_LORA), (0, 0))).astype(BF16),
        "a2": jnp.pad(a2, ((D_DECAY_LORA, LANES - D_DECAY_LORA - D_AAA_LORA), (0, 0))).astype(BF16),
        "g2": jnp.pad(g2, ((0, GATE_LORA_PAD - D_GATE_LORA), (0, 0))).astype(BF16),
        "ln1_g": row(ln1_g), "ln1_b": row(ln1_b), "ln2_g": row(ln2_g), "ln2_b": row(ln2_b),
        "w_up": w_up.astype(BF16),
        "conv_w": jnp.pad(conv_w, ((0, SUBLANES - CONV_W), (0, 0))),
        "conv_b": row(conv_b),
        "w_down": w_down.astype(BF16),
    }


def _pick_tile(n, cap):
    t = min(n, cap)
    while n % t:
        t //= 2
    return t


def _prompt_layer(x, lw, alpha):
    b, s, d = x.shape
    d_ff = lw["w_down"].shape[0]
    pos = jnp.arange(s, dtype=jnp.int32)
    proj, kh, vt = _in_proj(x, lw["w_in"], pos, tm=_pick_tile(s, 1024), attn_extras=True)
    att = _moba_prompt(proj, kh, vt)
    rw, wkv = _rwkv(proj, jnp.zeros((b, 1, Z_PAD), F32),
                    jnp.zeros((b, N_RWKV_HEADS, HEAD_DIM, HEAD_DIM), F32), lw,
                    tblk=_pick_tile(s, 512), t_real=None)
    x1 = _out_ln(att.reshape(b * s, ATT_WIDTH), rw.reshape(b * s, RWKV_WIDTH), x.reshape(b * s, d), lw,
                 tm=_pick_tile(s, 512), alpha=alpha)
    tm = _pick_tile(s, 512)
    y, gp_tail = _ffn(x1, lw, tm=tm, tf=512, alpha=alpha, seq_len=s)
    k_new = proj[:, :, K_COL:V_COL].reshape(b, s, N_KV_HEADS, HEAD_DIM)
    v_new = proj[:, :, V_COL:].reshape(b, s, N_KV_HEADS, HEAD_DIM)
    shift_new = proj[:, s - 1, :RWKV_PROJ]
    conv_new = gp_tail.reshape(b, s // tm, SUBLANES, d_ff)[:, -1, SUBLANES - (CONV_W - 1):, :]
    return y.reshape(b, s, d), k_new, v_new, wkv, shift_new, conv_new


def _decode_layer(x, cache_k, cache_v, page_table, shift0, wkv0, conv0, lw, alpha):
    bd, t, d = x.shape
    m = bd * t
    past = page_table.shape[1] * PAGE_SIZE
    pos = past + (jnp.arange(m, dtype=jnp.int32) % t)
    (proj,) = _in_proj(x.reshape(1, m, d), lw["w_in"], pos, tm=m, attn_extras=False)
    proj = proj.reshape(bd, t, PROJ_W)
    att = _moba_decode(proj, cache_k, cache_v, page_table, t)
    grp = _pick_tile(bd, RWKV_SEQS_PER_STEP)
    proj_pad = jnp.pad(proj, ((0, 0), (0, RWKV_CHUNK - t), (0, 0))).reshape(bd // grp, grp * RWKV_CHUNK, PROJ_W)
    shift_pad = jnp.pad(shift0, ((0, 0), (0, Z_PAD - RWKV_PROJ))).reshape(bd // grp, grp, Z_PAD)
    rw, wkv = _rwkv(proj_pad, shift_pad, wkv0, lw, tblk=grp * RWKV_CHUNK, t_real=t, independent=True)
    rw = rw.reshape(bd, RWKV_CHUNK, RWKV_WIDTH)[:, :t]
    x1 = _out_ln(att.reshape(m, ATT_WIDTH), rw.reshape(m, RWKV_WIDTH), x.reshape(m, d), lw, tm=m, alpha=alpha)
    d_ff = conv0.shape[-1]
    zeros = lambda n: jnp.zeros((bd, n, d_ff), F32)
    h1 = jnp.concatenate([conv0[:, 1:2], zeros(t - 1)], axis=1).reshape(m, d_ff)
    h2 = jnp.concatenate([conv0[:, 0:1], conv0[:, 1:2], zeros(t - 2)], axis=1).reshape(m, d_ff)
    y, gp = _ffn(x1, lw, tm=m, tf=512, alpha=alpha, hist=(h1, h2), dec_seq=t)
    k_new = proj[:, :, K_COL:V_COL].reshape(bd, t, N_KV_HEADS, HEAD_DIM)
    v_new = proj[:, :, V_COL:].reshape(bd, t, N_KV_HEADS, HEAD_DIM)
    shift_new = proj[:, t - 1, :RWKV_PROJ]
    conv_new = gp.reshape(bd, t, d_ff)[:, t - (CONV_W - 1):]
    return y.reshape(bd, t, d), k_new, v_new, wkv, shift_new, conv_new


def kernel(x_prompt, x_sample, cache_k, cache_v, page_table, state_wkv, state_shift, state_conv,
           w_in, w_out, shift_mu, w0, w2, a0, a2, g2, k_k, k_a, r_k, gn_g, gn_b,
           ln1_g, ln1_b, ln2_g, ln2_b, w_up, conv_w, conv_b, w_down):
    depth = w_in.shape[0]
    alpha = (2.0 * depth) ** 0.25
    assert x_sample.shape[1] >= CONV_W - 1
    yp, ys = x_prompt, x_sample
    outs_p, outs_s = [], []
    for l in range(depth):
        lw = _prep_layer(w_in[l], w_out[l], shift_mu[l], w0[l], w2[l], a0[l], a2[l], g2[l], k_k[l], k_a[l],
                         r_k[l], gn_g[l], gn_b[l], ln1_g[l], ln1_b[l], ln2_g[l], ln2_b[l],
                         w_up[l], conv_w[l], conv_b[l], w_down[l])
        yp, *rest_p = _prompt_layer(yp, lw, alpha)
        outs_p.append(rest_p)
        ys, *rest_s = _decode_layer(ys, cache_k[l], cache_v[l], page_table, state_shift[l], state_wkv[l],
                                    state_conv[l], lw, alpha)
        outs_s.append(rest_s)
    stack = lambda outs, n: jnp.stack([o[n] for o in outs])
    return (yp, ys,
            stack(outs_p, 0), stack(outs_p, 1), stack(outs_p, 2), stack(outs_p, 3), stack(outs_p, 4),
            stack(outs_s, 0), stack(outs_s, 1), stack(outs_s, 2), stack(outs_s, 3), stack(outs_s, 4))
```

```python
import functools

import jax
import jax.numpy as jnp
from jax import lax
from jax.experimental import pallas as pl
from jax.experimental.pallas import tpu as pltpu

F32 = jnp.float32
BF16 = jnp.bfloat16

HEAD_DIM = 64
N_ATT_HEADS = 16
N_KV_HEADS = 4
KV_GROUP = N_ATT_HEADS // N_KV_HEADS
ATT_WIDTH = N_ATT_HEADS * HEAD_DIM
KV_WIDTH = N_KV_HEADS * HEAD_DIM
ATT_END = ATT_WIDTH + 2 * KV_WIDTH
N_RWKV_HEADS = 16
RWKV_WIDTH = N_RWKV_HEADS * HEAD_DIM
ROPE_DIM = HEAD_DIM // 4
ROPE_THETA = 500000.0
MOBA_BLOCK = 256
MOBA_TOPK = 3
D_DECAY_LORA = 64
D_AAA_LORA = 64
D_GATE_LORA = 160
LORA_WIDTH = D_DECAY_LORA + D_AAA_LORA + D_GATE_LORA
RWKV_PROJ = 3 * RWKV_WIDTH + LORA_WIDTH
CONV_W = 3
LN_EPS = 1e-5
GN_EPS = 64e-5
NEG_INF = -1e30
PAGE_SIZE = 128

LANES = 128
SUBLANES = 8
VMEM_LIMIT_BYTES = 56 * 1024 * 1024

PROJ_TN = 512
Z_PAD = 3584
Q_COL = Z_PAD
K_COL = Q_COL + ATT_WIDTH
V_COL = K_COL + KV_WIDTH
PROJ_W = V_COL + KV_WIDTH
LORA_COL = 3 * RWKV_WIDTH
LORA_BLOCK = 384
GATE_LORA_PAD = 256

RWKV_CHUNK = HEAD_DIM
RWKV_SEQS_PER_STEP = 8
RWKV_STAGE_GROUP = 8
PAGES_PER_STEP = 32
MOBA_SPECIALISE_EVERY = 1

_NN = (((1,), (0,)), ((), ()))
_NT = (((1,), (1,)), ((), ()))


def _dot(a, b, dims=_NN):
    return lax.dot_general(a, b, dims, preferred_element_type=F32)


def _split_bf16(x):
    hi = x.astype(BF16)
    lo = (x - hi.astype(F32)).astype(BF16)
    return hi, lo


def _dot3(a, b, dims=_NN):
    ah, al = _split_bf16(a)
    bh, bl = _split_bf16(b)
    return _dot(ah, bh, dims) + (_dot(ah, bl, dims) + _dot(al, bh, dims))


def _sigmoid(x):
    return 1.0 / (1.0 + jnp.exp(-x))


def _layer_norm(x, g, b):
    mu = jnp.mean(x, axis=-1, keepdims=True)
    d = x - mu
    var = jnp.mean(d * d, axis=-1, keepdims=True)
    return d * lax.rsqrt(var + LN_EPS) * g + b


def _params(sem):
    return pltpu.CompilerParams(dimension_semantics=sem, vmem_limit_bytes=VMEM_LIMIT_BYTES)


def _rope_tables(pos):
    half = ROPE_DIM // 2
    inv = jnp.power(ROPE_THETA, -(jnp.arange(half, dtype=F32) * 2.0 / ROPE_DIM))
    ang = pos.astype(F32)[:, None] * inv[None, :]
    cos = jnp.cos(ang)
    sin = jnp.sin(ang)
    t = pos.shape[0]
    one = jnp.ones((t, HEAD_DIM - ROPE_DIM), F32)
    z8 = jnp.zeros((t, half), F32)
    zr = jnp.zeros((t, HEAD_DIM - ROPE_DIM), F32)
    c64 = jnp.concatenate([cos, cos, one], axis=1)
    a64 = jnp.concatenate([-sin, z8, zr], axis=1)
    b64 = jnp.concatenate([z8, sin, zr], axis=1)
    rep = LANES // HEAD_DIM
    return tuple(jnp.tile(m, (1, rep)) for m in (c64, a64, b64))


def _in_proj_kernel(x_ref, w_ref, cos_ref, sa_ref, sb_ref, proj_ref, *rest, attn_extras):
    if attn_extras:
        kh_ref, vt_ref, xb_ref = rest
    else:
        (xb_ref,) = rest
    j = pl.program_id(2)
    n_z = Z_PAD // PROJ_TN
    n_q = ATT_WIDTH // PROJ_TN
    half = ROPE_DIM // 2

    @pl.when(j == 0)
    def _():
        xb_ref[...] = x_ref[0].astype(BF16)

    acc = _dot(xb_ref[...], w_ref[...])

    def rope(slab):
        return (slab * cos_ref[...] + pltpu.roll(slab, LANES - half, 1) * sa_ref[...]
                + pltpu.roll(slab, half, 1) * sb_ref[...])

    @pl.when(j < n_z)
    def _():
        proj_ref[0] = acc

    @pl.when((j >= n_z) & (j < n_z + n_q))
    def _():
        for s in range(PROJ_TN // LANES):
            proj_ref[0, :, s * LANES:(s + 1) * LANES] = rope(acc[:, s * LANES:(s + 1) * LANES])

    @pl.when(j == n_z + n_q)
    def _():
        k_slabs = []
        for s in range(KV_WIDTH // LANES):
            ks = rope(acc[:, s * LANES:(s + 1) * LANES])
            proj_ref[0, :, s * LANES:(s + 1) * LANES] = ks
            k_slabs.append(ks)
        v = acc[:, KV_WIDTH:]
        proj_ref[0, :, KV_WIDTH:] = v
        if attn_extras:
            k = jnp.concatenate(k_slabs, axis=1)
            for c in range(N_KV_HEADS):
                kh_ref[0, c] = k[:, c * HEAD_DIM:(c + 1) * HEAD_DIM].astype(BF16)
            vt_ref[0] = v.T.astype(BF16)


def _in_proj(x, w_b, pos, *, tm, attn_extras):
    b, s, d = x.shape
    assert s % tm == 0 and PROJ_W % PROJ_TN == 0
    cos, sa, sb = _rope_tables(pos)
    tab_spec = pl.BlockSpec((tm, LANES), lambda bi, i, j: (i, 0))
    out_shape = [jax.ShapeDtypeStruct((b, s, PROJ_W), F32)]
    out_specs = [pl.BlockSpec((1, tm, PROJ_TN), lambda bi, i, j: (bi, i, j))]
    if attn_extras:
        out_shape += [jax.ShapeDtypeStruct((b, N_KV_HEADS, s, HEAD_DIM), BF16),
                      jax.ShapeDtypeStruct((b, KV_WIDTH, s), BF16)]
        out_specs += [pl.BlockSpec((1, N_KV_HEADS, tm, HEAD_DIM), lambda bi, i, j: (bi, 0, i, 0)),
                      pl.BlockSpec((1, KV_WIDTH, tm), lambda bi, i, j: (bi, 0, i))]
    return pl.pallas_call(
        functools.partial(_in_proj_kernel, attn_extras=attn_extras),
        grid=(b, s // tm, PROJ_W // PROJ_TN),
        in_specs=[pl.BlockSpec((1, tm, d), lambda bi, i, j: (bi, i, 0)),
                  pl.BlockSpec((d, PROJ_TN), lambda bi, i, j: (0, j)),
                  tab_spec, tab_spec, tab_spec],
        out_specs=out_specs,
        out_shape=out_shape,
        scratch_shapes=[pltpu.VMEM((tm, d), BF16)],
        compiler_params=_params(("parallel", "parallel", "arbitrary")),
        name="in_proj",
    )(x, w_b, cos, sa, sb)


def _moba_prompt_kernel(q_ref, kh_ref, vt_ref, o_ref, kmean_ref, *, n_blk, topk):
    i = pl.program_id(2)
    blk = MOBA_BLOCK
    n_cand = n_blk - 1
    scale = HEAD_DIM ** -0.5

    @pl.when(i == 0)
    def _():
        kmean_ref[...] = jnp.zeros_like(kmean_ref)
        for n in range(n_blk):
            kmean_ref[n:n + 1, :] = jnp.mean(
                kh_ref[0, 0, n * blk:(n + 1) * blk, :].astype(F32), axis=0, keepdims=True)

    q_t = q_ref[0].T
    qs_t = jnp.concatenate([q_t[g * HEAD_DIM:(g + 1) * HEAD_DIM, :] for g in range(KV_GROUP)],
                           axis=1)

    gate = _dot3(kmean_ref[...], qs_t)
    n_idx = lax.broadcasted_iota(jnp.int32, gate.shape, 0)
    gate = jnp.where(n_idx < i, gate, NEG_INF)
    rank = jnp.zeros(gate.shape, F32)
    for m in range(n_cand):
        gm = gate[m:m + 1, :]
        beats = (gm > gate) | ((gm == gate) & (m < n_idx))
        rank = rank + beats.astype(F32)
    sel_bias = jnp.where((rank < topk) & (n_idx < i), 0.0, NEG_INF)

    qb = (qs_t * scale).astype(BF16)
    nq = KV_GROUP * blk
    kpos = lax.broadcasted_iota(jnp.int32, (blk, nq), 0)
    qpos = lax.broadcasted_iota(jnp.int32, (blk, nq), 1) & (blk - 1)
    causal_bias = jnp.where(kpos <= qpos, 0.0, NEG_INF)

    def attend(first_own, nb):
        s = _dot(kh_ref[0, 0, 0:nb * blk, :], qb)
        parts = []
        for n in range(nb):
            if n < first_own:
                bias = sel_bias[n:n + 1, :]
            elif first_own == nb - 1:
                bias = causal_bias
            else:
                bias = jnp.where(n < i, sel_bias[n:n + 1, :], jnp.where(n == i, causal_bias, NEG_INF))
            parts.append(s[n * blk:(n + 1) * blk, :] + bias)
        s = jnp.concatenate(parts, axis=0) if nb > 1 else parts[0]
        p = jnp.exp(s - jnp.max(s, axis=0, keepdims=True))
        l = jnp.sum(p, axis=0, keepdims=True)
        o = _dot(vt_ref[0, :, 0:nb * blk], p.astype(BF16)) / l
        o4 = jnp.concatenate([o[:, g * blk:(g + 1) * blk] for g in range(KV_GROUP)], axis=0)
        o_ref[0] = o4.T.astype(o_ref.dtype)

    step = MOBA_SPECIALISE_EVERY
    for grp in range(-(-n_blk // step)):
        pl.when(i // step == grp)(functools.partial(attend, grp * step, min((grp + 1) * step, n_blk)))


def _moba_prompt(proj, kh, vt):
    b, s, _ = proj.shape
    assert s % MOBA_BLOCK == 0
    n_blk = s // MOBA_BLOCK
    topk = min(MOBA_TOPK, n_blk - 1)
    gw = KV_GROUP * HEAD_DIM
    n_rows = -(-n_blk // SUBLANES) * SUBLANES
    return pl.pallas_call(
        functools.partial(_moba_prompt_kernel, n_blk=n_blk, topk=topk),
        grid=(b, N_KV_HEADS, n_blk),
        in_specs=[pl.BlockSpec((1, MOBA_BLOCK, gw), lambda bi, c, i: (bi, i, Q_COL // gw + c)),
                  pl.BlockSpec((1, 1, s, HEAD_DIM), lambda bi, c, i: (bi, c, 0, 0)),
                  pl.BlockSpec((1, HEAD_DIM, s), lambda bi, c, i: (bi, c, 0))],
        out_specs=pl.BlockSpec((1, MOBA_BLOCK, gw), lambda bi, c, i: (bi, i, c)),
        out_shape=jax.ShapeDtypeStruct((b, s, ATT_WIDTH), BF16),
        scratch_shapes=[pltpu.VMEM((n_rows, HEAD_DIM), F32)],
        compiler_params=_params(("parallel", "parallel", "arbitrary")),
        name="moba_prompt",
    )(proj, kh, vt)


def _moba_decode_scores_kernel(pt_ref, wt_ref, *rest, n_new):
    del pt_ref
    r_pages = PAGES_PER_STEP
    k_refs = rest[:r_pages]
    s_ref, gate_ref, bmax_ref = rest[r_pages:]
    j = pl.program_id(1)
    scale = HEAD_DIM ** -0.5
    nq = N_ATT_HEADS * n_new
    pages_per_blk = MOBA_BLOCK // PAGE_SIZE
    blks_per_step = r_pages // pages_per_blk
    lane = lax.broadcasted_iota(jnp.int32, gate_ref.shape[1:], 1)

    @pl.when(j == 0)
    def _():
        gate_ref[0] = jnp.zeros(gate_ref.shape[1:], F32)
        bmax_ref[0] = jnp.full(bmax_ref.shape[1:], NEG_INF, F32)

    wt2 = jnp.concatenate(_split_bf16(wt_ref[0]), axis=0)
    gate = gate_ref[0]
    bmax = bmax_ref[0]
    for r in range(r_pages):
        s2 = _dot(wt2, k_refs[r][0].astype(BF16))
        raw = s2[:nq] + s2[nq:]
        s_ref[0, :, r * PAGE_SIZE:(r + 1) * PAGE_SIZE] = raw * scale
        first = r % pages_per_blk == 0
        blk_sum = raw if first else blk_sum + raw
        blk_max = raw if first else jnp.maximum(blk_max, raw)
        if r % pages_per_blk == pages_per_blk - 1:
            here = lane == j * blks_per_step + r // pages_per_blk
            gate = jnp.where(here, jnp.sum(blk_sum, axis=1, keepdims=True), gate)
            bmax = jnp.where(here, jnp.max(blk_max, axis=1, keepdims=True) * scale, bmax)
    gate_ref[0] = gate
    bmax_ref[0] = bmax


def _moba_decode_values_kernel(pt_ref, wt_ref, kvn_ref, gate_ref, bmax_ref, s_ref, *rest,
                               n_cand, n_new):
    del pt_ref
    r_pages = PAGES_PER_STEP
    v_refs = rest[:r_pages]
    o_ref = rest[r_pages]
    sel_ref, acc_ref, l_ref, m_ref = rest[r_pages + 1:]
    n_cand_pad = gate_ref.shape[2]
    j = pl.program_id(1)
    scale = HEAD_DIM ** -0.5
    pages_per_blk = MOBA_BLOCK // PAGE_SIZE
    blks_per_step = r_pages // pages_per_blk

    @pl.when(j == 0)
    def _():
        lane = lax.broadcasted_iota(jnp.int32, gate_ref.shape[1:], 1)
        lane_f = lane.astype(F32)
        gate = jnp.where(lane < n_cand, gate_ref[0] * (1.0 / MOBA_BLOCK), NEG_INF)
        sel = lane < 0
        for _ in range(min(MOBA_TOPK, n_cand)):
            top = jnp.max(gate, axis=1, keepdims=True)
            first_top = jnp.min(jnp.where(gate == top, lane_f, float(n_cand_pad)), axis=1, keepdims=True)
            pick = lane_f == first_top
            sel = sel | pick
            gate = jnp.where(pick, -jnp.inf, gate)
        sel = sel & (lane < n_cand)
        sel_ref[...] = sel.astype(F32)

        kvn = kvn_ref[0]
        k_new = kvn[:, :KV_WIDTH].astype(BF16)
        v_new = kvn[:, KV_WIDTH:].astype(BF16)
        s_new = _dot((wt_ref[0] * scale).astype(BF16), k_new, _NT)
        t_q = lax.broadcasted_iota(jnp.int32, s_new.shape, 0) % n_new
        t_k = lax.broadcasted_iota(jnp.int32, s_new.shape, 1)
        ok_new = (t_k <= t_q) & (t_k < n_new)
        m = jnp.maximum(jnp.max(jnp.where(ok_new, s_new, NEG_INF), axis=1, keepdims=True),
                        jnp.max(jnp.where(sel, bmax_ref[0], NEG_INF), axis=1, keepdims=True))
        p_new = jnp.where(ok_new, jnp.exp(s_new - m), 0.0)
        m_ref[...] = jnp.broadcast_to(m, m_ref.shape)
        l_lane = lax.broadcasted_iota(jnp.int32, l_ref.shape, 1)
        l_ref[...] = jnp.where(l_lane == 0, jnp.sum(p_new, axis=1, keepdims=True), 0.0)
        acc_ref[...] = _dot(p_new.astype(BF16), v_new)

    src = lax.broadcasted_iota(jnp.int32, (n_cand_pad, LANES), 0)
    dst = lax.broadcasted_iota(jnp.int32, (n_cand_pad, LANES), 1)
    onehot = ((src == j * blks_per_step + dst) & (dst < blks_per_step)).astype(BF16)
    sel_here = _dot(sel_ref[...].astype(BF16), onehot)
    m = m_ref[:, 0:1]
    acc = acc_ref[...]
    l_part = l_ref[...]
    for r in range(r_pages):
        blk = r // pages_per_blk
        sb = s_ref[0, :, r * PAGE_SIZE:(r + 1) * PAGE_SIZE]
        pb = jnp.where(sel_here[:, blk:blk + 1] > 0.5, jnp.exp(sb - m), 0.0)
        l_part = l_part + pb
        acc = acc + _dot(pb.astype(BF16), v_refs[r][0].astype(BF16), _NT)
    acc_ref[...] = acc
    l_ref[...] = l_part

    @pl.when(j == pl.num_programs(1) - 1)
    def _():
        o_ref[0] = acc / jnp.sum(l_part, axis=1, keepdims=True)


def _moba_decode(proj_d, cache_k, cache_v, page_table, n_new):
    bd = proj_d.shape[0]
    n_pages = page_table.shape[1]
    past = n_pages * PAGE_SIZE
    assert past % MOBA_BLOCK == 0 and n_pages % PAGES_PER_STEP == 0 and n_new <= SUBLANES
    n_cand = past // MOBA_BLOCK
    n_steps = n_pages // PAGES_PER_STEP
    n_phys = cache_k.shape[0]
    nq = N_ATT_HEADS * n_new
    ck = jnp.transpose(cache_k, (0, 2, 3, 1)).reshape(n_phys, KV_WIDTH, PAGE_SIZE)
    cv = jnp.transpose(cache_v, (0, 2, 3, 1)).reshape(n_phys, KV_WIDTH, PAGE_SIZE)

    q = proj_d[:, :, Q_COL:Q_COL + ATT_WIDTH].reshape(bd, n_new, N_KV_HEADS, KV_GROUP, HEAD_DIM)
    q = jnp.transpose(q, (0, 2, 3, 1, 4)).reshape(bd, N_KV_HEADS, KV_GROUP * n_new, 1, HEAD_DIM)
    eye = jnp.eye(N_KV_HEADS, dtype=F32)[None, :, None, :, None]
    wt = (q * eye).reshape(bd, nq, KV_WIDTH)
    kvn = jnp.pad(proj_d[:, :, K_COL:], ((0, 0), (0, SUBLANES - n_new), (0, 0)))

    def page_map(r):
        return lambda b, j, pt: (pt[b, j * PAGES_PER_STEP + r], 0, 0)

    step_keys = PAGES_PER_STEP * PAGE_SIZE
    n_cand_pad = -(-n_cand // LANES) * LANES
    per_seq = lambda w: pl.BlockSpec((1, nq, w), lambda b, j, pt: (b, 0, 0))
    page_specs = [pl.BlockSpec((1, KV_WIDTH, PAGE_SIZE), page_map(r)) for r in range(PAGES_PER_STEP)]
    score_spec = pl.BlockSpec((1, nq, step_keys), lambda b, j, pt: (b, 0, j))
    scores, gate, bmax = pl.pallas_call(
        functools.partial(_moba_decode_scores_kernel, n_new=n_new),
        grid_spec=pltpu.PrefetchScalarGridSpec(
            num_scalar_prefetch=1,
            grid=(bd, n_steps),
            in_specs=[per_seq(KV_WIDTH)] + page_specs,
            out_specs=[score_spec, per_seq(n_cand_pad), per_seq(n_cand_pad)]),
        out_shape=[jax.ShapeDtypeStruct((bd, nq, past), F32),
                   jax.ShapeDtypeStruct((bd, nq, n_cand_pad), F32),
                   jax.ShapeDtypeStruct((bd, nq, n_cand_pad), F32)],
        compiler_params=_params(("parallel", "arbitrary")),
        name="moba_decode_scores",
    )(page_table, wt, *([ck] * PAGES_PER_STEP))
    out = pl.pallas_call(
        functools.partial(_moba_decode_values_kernel, n_cand=n_cand, n_new=n_new),
        grid_spec=pltpu.PrefetchScalarGridSpec(
            num_scalar_prefetch=1,
            grid=(bd, n_steps),
            in_specs=[per_seq(KV_WIDTH),
                      pl.BlockSpec((1, SUBLANES, 2 * KV_WIDTH), lambda b, j, pt: (b, 0, 0)),
                      per_seq(n_cand_pad), per_seq(n_cand_pad), score_spec] + page_specs,
            out_specs=per_seq(KV_WIDTH),
            scratch_shapes=[pltpu.VMEM((nq, n_cand_pad), F32),
                            pltpu.VMEM((nq, KV_WIDTH), F32),
                            pltpu.VMEM((nq, LANES), F32),
                            pltpu.VMEM((nq, LANES), F32)]),
        out_shape=jax.ShapeDtypeStruct((bd, nq, KV_WIDTH), F32),
        compiler_params=_params(("parallel", "arbitrary")),
        name="moba_decode_values",
    )(page_table, wt, kvn, gate, bmax, scores, *([cv] * PAGES_PER_STEP))
    o = out.reshape(bd, N_KV_HEADS, KV_GROUP, n_new, N_KV_HEADS, HEAD_DIM)
    o = jnp.stack([o[:, c, :, :, c, :] for c in range(N_KV_HEADS)], axis=1)
    o = jnp.transpose(o, (0, 3, 1, 2, 4)).reshape(bd, n_new, ATT_WIDTH)
    return o.astype(BF16)


def _rwkv_kernel(zr_ref, zk_ref, zv_ref, zl_ref, sr_ref, sk_ref, sv_ref, sl_ref,
                 mu_ref, mul_ref, par_ref, w2_ref, a2_ref, g2_ref, s0_ref,
                 rw_ref, wkv_ref,
                 state_ref, pr_ref, pk_ref, pv_ref, plo_ref, *, t_real, independent):
    tb = pl.program_id(2)
    n_tb = pl.num_programs(2)
    tblk = zr_ref.shape[1]
    c_len = RWKV_CHUNK
    width = LANES
    lane = lax.broadcasted_iota(jnp.int32, (1, width), 1)
    head_a = lane < HEAD_DIM
    row = lax.broadcasted_iota(jnp.int32, (tblk, 1), 0)

    def block_diag(sa, sb):
        z = jnp.zeros((HEAD_DIM, HEAD_DIM), F32)
        return jnp.concatenate([jnp.concatenate([sa, z], axis=1),
                                jnp.concatenate([z, sb], axis=1)], axis=0)

    if independent:
        def token_shift(z, first_ref, prev_ref, mu):
            z_prev = pltpu.roll(z, 1, 0)
            first = first_ref[0]
            for c in range(tblk // c_len):
                z_prev = jnp.where(row == c * c_len, first[c:c + 1, :], z_prev)
            return z + (z_prev - z) * mu
    else:
        @pl.when(tb == 0)
        def _():
            state_ref[...] = block_diag(s0_ref[0, 0], s0_ref[0, 1])
            pr_ref[0:1, :] = sr_ref[0]
            pk_ref[0:1, :] = sk_ref[0]
            pv_ref[0:1, :] = sv_ref[0]
            plo_ref[0:1, :] = sl_ref[0]

        def token_shift(z, first_ref, prev_ref, mu):
            z_prev = jnp.where(row == 0, prev_ref[0:1, :], pltpu.roll(z, 1, 0))
            prev_ref[0:1, :] = z[tblk - 1:tblk, :]
            return z + (z_prev - z) * mu

    def seg_sum(x):
        sa = jnp.sum(jnp.where(head_a, x, 0.0), axis=1, keepdims=True)
        sb = jnp.sum(jnp.where(head_a, 0.0, x), axis=1, keepdims=True)
        return jnp.where(head_a, sa, sb)

    r = token_shift(zr_ref[0], sr_ref, pr_ref, mu_ref[0:1, :])
    k = token_shift(zk_ref[0], sk_ref, pk_ref, mu_ref[1:2, :])
    v = token_shift(zv_ref[0], sv_ref, pv_ref, mu_ref[2:3, :])
    lo = token_shift(zl_ref[0], sl_ref, plo_ref, mul_ref[...])
    par = par_ref[...]
    w0, a0, k_k, k_a, r_k, gn_g, gn_b = (par[n:n + 1, :] for n in range(7))

    lo_wa = lo[:, :LANES]
    nx = -(w0 + _dot(jnp.tanh(lo_wa).astype(BF16), w2_ref[...]))
    softplus = jnp.maximum(nx, 0.0) + jnp.log(1.0 + jnp.exp(-jnp.abs(nx)))
    logw = -jnp.exp(-softplus - 0.5)
    a = _sigmoid(a0 + _dot(lo_wa.astype(BF16), a2_ref[...]))
    g = _dot(_sigmoid(lo[:, LANES:]).astype(BF16), g2_ref[...])
    kk = k * k_k
    kk = kk / jnp.maximum(jnp.sqrt(seg_sum(kk * kk)), 1e-12)
    k = k * (1.0 + (a - 1.0) * k_a)
    bonus = seg_sum(r * k * r_k) * v

    if t_real is not None:
        valid = ((row & (c_len - 1)) if independent else (tb * tblk + row)) < t_real
        logw = jnp.where(valid, logw, 0.0)
        kk = jnp.where(valid, kk, 0.0)
        k = jnp.where(valid, k, 0.0)
        v_in = jnp.where(valid, v, 0.0)
    else:
        v_in = v

    def pair_rows(x):
        return jnp.concatenate([jnp.where(head_a, x, 0.0), jnp.where(head_a, 0.0, x)], axis=0)

    ri = lax.broadcasted_iota(jnp.int32, (c_len, 2 * c_len), 0)
    ci = lax.broadcasted_iota(jnp.int32, (c_len, 2 * c_len), 1)
    first = ci < c_len
    tcol = ci & (c_len - 1)
    strict = tcol < ri
    incl = tcol <= ri
    eye2 = (tcol == ri).astype(F32)
    bi = lax.broadcasted_iota(jnp.int32, (width, width), 0) < HEAD_DIM
    bj = lax.broadcasted_iota(jnp.int32, (width, width), 1) < HEAD_DIM
    same_head = bi == bj
    live = c_len if t_real is None else min(t_real, c_len)
    n_levels = max((live - 1).bit_length() - 1, 0)
    shift = c_len.bit_length() - 1
    chunks = [slice(c * c_len, (c + 1) * c_len) for c in range(tblk // c_len)]
    bf = lambda x: x.astype(BF16)

    tr = lax.broadcasted_iota(jnp.int32, (tblk, tblk), 0)
    tc = lax.broadcasted_iota(jnp.int32, (tblk, tblk), 1)
    tri = ((tr >= tc) & ((tr >> shift) == (tc >> shift))).astype(BF16)
    lh = logw.astype(BF16)
    l1 = logw - lh.astype(F32)
    lm = l1.astype(BF16)
    ll = (l1 - lm.astype(F32)).astype(BF16)
    lc = _dot(tri, lh) + (_dot(tri, lm) + _dot(tri, ll))
    g_in = jnp.exp(lc)
    g_inv = jnp.exp(-lc)
    a_all = -kk * jnp.exp(lc - logw)
    b_all = kk * a * g_inv
    k_all = k * g_inv
    r_all = r * g_in

    def group_terms(sls):
        a_t = [a_all[sl] for sl in sls]
        r_t = [r_all[sl] for sl in sls]
        b_t = [b_all[sl] for sl in sls]
        k_t = [k_all[sl] for sl in sls]
        v_c = [v_in[sl] for sl in sls]
        g_end = [g_in[sl.stop - 1:sl.stop, :] for sl in sls]
        a_pair = [pair_rows(bf(x)) for x in a_t]
        s4 = [_dot(jnp.concatenate([ap, pair_rows(bf(rt))], axis=0),
                   bf(jnp.concatenate([bt, kt], axis=0)), _NT)
              for ap, rt, bt, kt in zip(a_pair, r_t, b_t, k_t)]
        m_ab, m_ak, m_rb, m_rk = [], [], [], []
        for x in s4:
            sa_, sb_ = x[0:c_len], x[c_len:2 * c_len]
            ra_, rb_ = x[2 * c_len:3 * c_len], x[3 * c_len:4 * c_len]
            m_ab.append(jnp.where(strict, jnp.where(first, sa_, pltpu.roll(sb_, c_len, 1)), 0.0))
            m_ak.append(jnp.where(strict, jnp.where(first, pltpu.roll(sa_, c_len, 1), sb_), 0.0))
            m_rb.append(jnp.where(incl, jnp.where(first, ra_, pltpu.roll(rb_, c_len, 1)), 0.0))
            m_rk.append(jnp.where(incl, jnp.where(first, pltpu.roll(ra_, c_len, 1), rb_), 0.0))

        t_inv = [m + eye2 for m in m_ab]
        if n_levels > 0:
            l_pow = [bf(m) for m in m_ab]
            l_pow = [bf(_dot(lp, pair_rows(lp))) for lp in l_pow]
            for level in range(n_levels):
                qp = [_dot(lp, jnp.concatenate([pair_rows(lp), pair_rows(bf(ti))], axis=1))
                      for lp, ti in zip(l_pow, t_inv)]
                t_inv = [ti + x[:, width:] for ti, x in zip(t_inv, qp)]
                if level + 1 < n_levels:
                    l_pow = [bf(x[:, :width]) for x in qp]

        v_pair = [pair_rows(bf(x)) for x in v_c]
        mak_v = [_dot(bf(m), vp) for m, vp in zip(m_ak, v_pair)]
        tw = [_dot(bf(ti), jnp.concatenate([ap, pair_rows(bf(mv))], axis=1))
              for ti, ap, mv in zip(t_inv, a_pair, mak_v)]
        w_t = [x[:, :width] for x in tw]
        v_hat = [x[:, width:] for x in tw]
        mw = [_dot(bf(jnp.concatenate([mb, mk], axis=1)),
                   jnp.concatenate(
                       [jnp.concatenate([pair_rows(bf(w)), pair_rows(bf(vh))], axis=1),
                        jnp.concatenate([jnp.zeros_like(vp), vp], axis=1)], axis=0))
              for mb, mk, w, vh, vp in zip(m_rb, m_rk, w_t, v_hat, v_pair)]
        p_m = [rt + x[:, :width] for rt, x in zip(r_t, mw)]
        q_m = [x[:, width:] for x in mw]
        phi, psi = [], []
        for w, vh, vc, bt, kt, ge in zip(w_t, v_hat, v_c, b_t, k_t, g_end):
            bk = bf(jnp.concatenate([bt * ge, kt * ge], axis=0))
            left = jnp.concatenate(
                [jnp.concatenate([w, jnp.zeros_like(w)], axis=0),
                 jnp.concatenate([vh, vc], axis=0)], axis=1)
            pp = _dot(bf(left.T), bk)
            phi.append(bf(jnp.where(same_head, pp[:width], 0.0)))
            psi.append(jnp.where(same_head, pp[width:], 0.0))
        return list(zip(p_m, q_m, phi, psi, g_end))

    def advance(s, term):
        pm, qm, ph, ps, ge = term
        sb16 = bf(s)
        return _dot(bf(pm), sb16, _NT) + qm, s * ge + _dot(sb16, ph) + ps

    outs = []
    s = None if independent else state_ref[...]
    for g0 in range(0, len(chunks), RWKV_STAGE_GROUP):
        for c, term in enumerate(group_terms(chunks[g0:g0 + RWKV_STAGE_GROUP]), start=g0):
            if independent:
                o_c, s_c = advance(block_diag(s0_ref[c, 0], s0_ref[c, 1]), term)
                wkv_ref[c, 0] = s_c[:HEAD_DIM, :HEAD_DIM]
                wkv_ref[c, 1] = s_c[HEAD_DIM:, HEAD_DIM:]
            else:
                o_c, s = advance(s, term)
            outs.append(o_c)
    if not independent:
        state_ref[...] = s

        @pl.when(tb == n_tb - 1)
        def _():
            wkv_ref[0, 0] = s[:HEAD_DIM, :HEAD_DIM]
            wkv_ref[0, 1] = s[HEAD_DIM:, HEAD_DIM:]

    o = jnp.concatenate(outs, axis=0) if len(outs) > 1 else outs[0]
    inv_n = 1.0 / HEAD_DIM
    mu_o = seg_sum(o) * inv_n
    d = o - mu_o
    var = seg_sum(d * d) * inv_n
    on = d * lax.rsqrt(var + GN_EPS) * gn_g + gn_b
    rw_ref[0] = ((on + bonus) * g).astype(rw_ref.dtype)


def _rwkv(proj, shift0, wkv0, lw, *, tblk, t_real, independent=False):
    b, s, _ = proj.shape
    assert s % tblk == 0 and tblk % RWKV_CHUNK == 0
    n_pairs = RWKV_WIDTH // LANES
    rkv = RWKV_WIDTH // LANES
    lora_blk = LORA_COL // LORA_BLOCK
    n_first = shift0.shape[1]
    n_state = tblk // RWKV_CHUNK if independent else 1
    assert (not independent) or (s == tblk and n_first == n_state)

    def z_spec(width, col0):
        return pl.BlockSpec((1, tblk, width), lambda bi, p, t: (bi, t, col0(p)))

    def s_spec(width, col0):
        return pl.BlockSpec((1, n_first, width), lambda bi, p, t: (bi, 0, col0(p)))

    state_spec = pl.BlockSpec((n_state, 2, HEAD_DIM, HEAD_DIM), lambda bi, p, t: (bi, p, 0, 0))

    cols = [lambda p: p, lambda p: rkv + p, lambda p: 2 * rkv + p]
    in_specs = ([z_spec(LANES, c) for c in cols] + [z_spec(LORA_BLOCK, lambda p: lora_blk)]
                + [s_spec(LANES, c) for c in cols] + [s_spec(LORA_BLOCK, lambda p: lora_blk)]
                + [pl.BlockSpec((SUBLANES, LANES), lambda bi, p, t: (0, p)),
                   pl.BlockSpec((1, LORA_BLOCK), lambda bi, p, t: (0, 0)),
                   pl.BlockSpec((SUBLANES, LANES), lambda bi, p, t: (0, p)),
                   pl.BlockSpec((LANES, LANES), lambda bi, p, t: (0, p)),
                   pl.BlockSpec((LANES, LANES), lambda bi, p, t: (0, p)),
                   pl.BlockSpec((GATE_LORA_PAD, LANES), lambda bi, p, t: (0, p)),
                   state_spec])
    return pl.pallas_call(
        functools.partial(_rwkv_kernel, t_real=t_real, independent=independent),
        grid=(b, n_pairs, s // tblk),
        in_specs=in_specs,
        out_specs=[pl.BlockSpec((1, tblk, LANES), lambda bi, p, t: (bi, t, p)), state_spec],
        out_shape=[jax.ShapeDtypeStruct((b, s, RWKV_WIDTH), BF16),
                   jax.ShapeDtypeStruct((b * n_state, N_RWKV_HEADS, HEAD_DIM, HEAD_DIM), F32)],
        scratch_shapes=[pltpu.VMEM((LANES, LANES), F32),
                        pltpu.VMEM((SUBLANES, LANES), F32),
                        pltpu.VMEM((SUBLANES, LANES), F32),
                        pltpu.VMEM((SUBLANES, LANES), F32),
                        pltpu.VMEM((SUBLANES, LORA_BLOCK), F32)],
        compiler_params=_params(("parallel", "parallel", "arbitrary")),
        name="rwkv7",
    )(proj, proj, proj, proj, shift0, shift0, shift0, shift0,
      lw["mu_rkv"], lw["mu_lora"], lw["rwkv_par"], lw["w2"], lw["a2"], lw["g2"], wkv0)


def _out_ln_kernel(att_ref, rw_ref, x_ref, wa_ref, wr_ref, g_ref, b_ref, o_ref, *, alpha):
    mix = _dot(att_ref[...], wa_ref[...]) + _dot(rw_ref[...], wr_ref[...])
    o_ref[...] = _layer_norm(alpha * x_ref[...] + mix, g_ref[...], b_ref[...])


def _out_ln(att, rw, x, lw, *, tm, alpha):
    m, d = x.shape
    assert m % tm == 0
    row = lambda w: pl.BlockSpec((tm, w), lambda i: (i, 0))
    full = lambda r, w: pl.BlockSpec((r, w), lambda i: (0, 0))
    return pl.pallas_call(
        functools.partial(_out_ln_kernel, alpha=alpha),
        grid=(m // tm,),
        in_specs=[row(ATT_WIDTH), row(RWKV_WIDTH), row(d),
                  full(ATT_WIDTH, d), full(RWKV_WIDTH, d), full(1, d), full(1, d)],
        out_specs=row(d),
        out_shape=jax.ShapeDtypeStruct((m, d), F32),
        compiler_params=_params(("parallel",)),
        name="out_proj_ln",
    )(att, rw, x, lw["w_out_att"], lw["w_out_rw"], lw["ln1_g"], lw["ln1_b"])


def _ffn_kernel(x_ref, wg_ref, wu_ref, wd_ref, cw_ref, cb_ref, g_ref, b_ref, *rest,
                alpha, tiles_per_seq, dec_seq):
    decode = dec_seq is not None
    if decode:
        h1_ref, h2_ref, y_ref, gp_ref, xb_ref, acc_ref = rest
    else:
        y_ref, gp_ref, xb_ref, acc_ref, tail_ref = rest
    i = pl.program_id(0)
    j = pl.program_id(1)
    tm = x_ref.shape[0]
    keep = SUBLANES

    @pl.when(j == 0)
    def _():
        xb_ref[...] = x_ref[...].astype(BF16)
        acc_ref[...] = jnp.zeros_like(acc_ref)

    xb = xb_ref[...]
    gp = _dot(xb, wg_ref[...])
    up = _dot(xb, wu_ref[...])
    row = lax.broadcasted_iota(jnp.int32, (tm, 1), 0)
    r1 = pltpu.roll(gp, 1, 0)
    r2 = pltpu.roll(gp, 2, 0)
    if decode:
        t = row % dec_seq
        g1 = jnp.where(t >= 1, r1, 0.0) + h1_ref[...]
        g2 = jnp.where(t >= 2, r2, 0.0) + h2_ref[...]
        gp_ref[...] = gp
    else:
        tail = tail_ref[j]
        seq_start = (i % tiles_per_seq) == 0
        t6 = jnp.where(seq_start, 0.0, tail[keep - 2:keep - 1, :])
        t7 = jnp.where(seq_start, 0.0, tail[keep - 1:keep, :])
        g1 = jnp.where(row == 0, t7, r1)
        g2 = jnp.where(row == 0, t6, jnp.where(row == 1, t7, r2))
        tail_ref[j] = gp[tm - keep:, :]
        gp_ref[0] = gp[tm - keep:, :]
    cw = cw_ref[...]
    conv = cw[0:1, :] * g2 + cw[1:2, :] * g1 + cw[2:3, :] * gp + cb_ref[...]
    h = conv * _sigmoid(conv) * up
    acc_ref[...] += _dot(h.astype(BF16), wd_ref[...])

    @pl.when(j == pl.num_programs(1) - 1)
    def _():
        y_ref[...] = _layer_norm(alpha * x_ref[...] + acc_ref[...], g_ref[...], b_ref[...])


def _ffn(x1, lw, *, tm, tf, alpha, seq_len=None, hist=None, dec_seq=None):
    m, d = x1.shape
    d_ff = lw["w_down"].shape[0]
    assert m % tm == 0 and d_ff % tf == 0
    nf = d_ff // tf
    decode = hist is not None
    in_specs = [pl.BlockSpec((tm, d), lambda i, j: (i, 0)),
                pl.BlockSpec((d, tf), lambda i, j: (0, j)),
                pl.BlockSpec((d, tf), lambda i, j: (0, nf + j)),
                pl.BlockSpec((tf, d), lambda i, j: (j, 0)),
                pl.BlockSpec((SUBLANES, tf), lambda i, j: (0, j)),
                pl.BlockSpec((1, tf), lambda i, j: (0, j)),
                pl.BlockSpec((1, d), lambda i, j: (0, 0)),
                pl.BlockSpec((1, d), lambda i, j: (0, 0))]
    args = [x1, lw["w_up"], lw["w_up"], lw["w_down"], lw["conv_w"], lw["conv_b"], lw["ln2_g"], lw["ln2_b"]]
    scratch = [pltpu.VMEM((tm, d), BF16), pltpu.VMEM((tm, d), F32)]
    if decode:
        assert m == tm
        in_specs += [pl.BlockSpec((tm, tf), lambda i, j: (0, j))] * 2
        args += list(hist)
        gp_shape = jax.ShapeDtypeStruct((m, d_ff), F32)
        gp_spec = pl.BlockSpec((tm, tf), lambda i, j: (0, j))
        tiles_per_seq = None
    else:
        assert seq_len % tm == 0 and tm % SUBLANES == 0
        tiles_per_seq = seq_len // tm
        gp_shape = jax.ShapeDtypeStruct((m // tm, SUBLANES, d_ff), F32)
        gp_spec = pl.BlockSpec((1, SUBLANES, tf), lambda i, j: (i, 0, j))
        scratch += [pltpu.VMEM((nf, SUBLANES, tf), F32)]
    return pl.pallas_call(
        functools.partial(_ffn_kernel, alpha=alpha, tiles_per_seq=tiles_per_seq, dec_seq=dec_seq),
        grid=(m // tm, nf),
        in_specs=in_specs,
        out_specs=[pl.BlockSpec((tm, d), lambda i, j: (i, 0)), gp_spec],
        out_shape=[jax.ShapeDtypeStruct((m, d), F32), gp_shape],
        scratch_shapes=scratch,
        compiler_params=_params(("arbitrary", "arbitrary")),
        name="conv_ffn_ln",
    )(*args)


def _prep_layer(w_in, w_out, shift_mu, w0, w2, a0, a2, g2, k_k, k_a, r_k, gn_g, gn_b,
                ln1_g, ln1_b, ln2_g, ln2_b, w_up, conv_w, conv_b, w_down):
    row = lambda v: v.reshape(1, -1)
    zpad = Z_PAD - RWKV_PROJ
    w_in_b = w_in.astype(BF16)
    w_in_r = jnp.concatenate([jnp.pad(w_in_b[:, ATT_END:], ((0, 0), (0, zpad))), w_in_b[:, :ATT_END]], axis=1)
    mu = jnp.pad(shift_mu, (0, zpad))
    zero_row = jnp.zeros((RWKV_WIDTH,), F32)
    return {
        "w_in": w_in_r,
        "w_out_att": w_out[:ATT_WIDTH].astype(BF16),
        "w_out_rw": w_out[ATT_WIDTH:].astype(BF16),
        "mu_rkv": jnp.pad(mu[:LORA_COL].reshape(3, RWKV_WIDTH), ((0, SUBLANES - 3), (0, 0))),
        "mu_lora": row(mu[LORA_COL:LORA_COL + LORA_BLOCK]),
        "rwkv_par": jnp.stack([w0, a0, k_k, k_a, r_k, gn_g, gn_b, zero_row]),
        "w2": jnp.pad(w2, ((0, LANES - D_DECAY_LORA), (0, 0))).astype(BF16),
        "a2": jnp.pad(a2, ((D_DECAY_LORA, LANES - D_DECAY_LORA - D_AAA_LORA), (0, 0))).astype(BF16),
        "g2": jnp.pad(g2, ((0, GATE_LORA_PAD - D_GATE_LORA), (0, 0))).astype(BF16),
        "ln1_g": row(ln1_g), "ln1_b": row(ln1_b), "ln2_g": row(ln2_g), "ln2_b": row(ln2_b),
        "w_up": w_up.astype(BF16),
        "conv_w": jnp.pad(conv_w, ((0, SUBLANES - CONV_W), (0, 0))),
        "conv_b": row(conv_b),
        "w_down": w_down.astype(BF16),
    }


def _pick_tile(n, cap):
    t = min(n, cap)
    while n % t:
        t //= 2
    return t


def _prompt_layer(x, lw, alpha):
    b, s, d = x.shape
    d_ff = lw["w_down"].shape[0]
    pos = jnp.arange(s, dtype=jnp.int32)
    proj, kh, vt = _in_proj(x, lw["w_in"], pos, tm=_pick_tile(s, 1024), attn_extras=True)
    att = _moba_prompt(proj, kh, vt)
    rw, wkv = _rwkv(proj, jnp.zeros((b, 1, Z_PAD), F32),
                    jnp.zeros((b, N_RWKV_HEADS, HEAD_DIM, HEAD_DIM), F32), lw,
                    tblk=_pick_tile(s, 512), t_real=None)
    x1 = _out_ln(att.reshape(b * s, ATT_WIDTH), rw.reshape(b * s, RWKV_WIDTH), x.reshape(b * s, d), lw,
                 tm=_pick_tile(s, 512), alpha=alpha)
    tm = _pick_tile(s, 512)
    y, gp_tail = _ffn(x1, lw, tm=tm, tf=512, alpha=alpha, seq_len=s)
    k_new = proj[:, :, K_COL:V_COL].reshape(b, s, N_KV_HEADS, HEAD_DIM)
    v_new = proj[:, :, V_COL:].reshape(b, s, N_KV_HEADS, HEAD_DIM)
    shift_new = proj[:, s - 1, :RWKV_PROJ]
    conv_new = gp_tail.reshape(b, s // tm, SUBLANES, d_ff)[:, -1, SUBLANES - (CONV_W - 1):, :]
    return y.reshape(b, s, d), k_new, v_new, wkv, shift_new, conv_new


def _decode_layer(x, cache_k, cache_v, page_table, shift0, wkv0, conv0, lw, alpha):
    bd, t, d = x.shape
    m = bd * t
    past = page_table.shape[1] * PAGE_SIZE
    pos = past + (jnp.arange(m, dtype=jnp.int32) % t)
    (proj,) = _in_proj(x.reshape(1, m, d), lw["w_in"], pos, tm=m, attn_extras=False)
    proj = proj.reshape(bd, t, PROJ_W)
    att = _moba_decode(proj, cache_k, cache_v, page_table, t)
    grp = _pick_tile(bd, RWKV_SEQS_PER_STEP)
    proj_pad = jnp.pad(proj, ((0, 0), (0, RWKV_CHUNK - t), (0, 0))).reshape(bd // grp, grp * RWKV_CHUNK, PROJ_W)
    shift_pad = jnp.pad(shift0, ((0, 0), (0, Z_PAD - RWKV_PROJ))).reshape(bd // grp, grp, Z_PAD)
    rw, wkv = _rwkv(proj_pad, shift_pad, wkv0, lw, tblk=grp * RWKV_CHUNK, t_real=t, independent=True)
    rw = rw.reshape(bd, RWKV_CHUNK, RWKV_WIDTH)[:, :t]
    x1 = _out_ln(att.reshape(m, ATT_WIDTH), rw.reshape(m, RWKV_WIDTH), x.reshape(m, d), lw, tm=m, alpha=alpha)
    d_ff = conv0.shape[-1]
    zeros = lambda n: jnp.zeros((bd, n, d_ff), F32)
    h1 = jnp.concatenate([conv0[:, 1:2], zeros(t - 1)], axis=1).reshape(m, d_ff)
    h2 = jnp.concatenate([conv0[:, 0:1], conv0[:, 1:2], zeros(t - 2)], axis=1).reshape(m, d_ff)
    y, gp = _ffn(x1, lw, tm=m, tf=512, alpha=alpha, hist=(h1, h2), dec_seq=t)
    k_new = proj[:, :, K_COL:V_COL].reshape(bd, t, N_KV_HEADS, HEAD_DIM)
    v_new = proj[:, :, V_COL:].reshape(bd, t, N_KV_HEADS, HEAD_DIM)
    shift_new = proj[:, t - 1, :RWKV_PROJ]
    conv_new = gp.reshape(bd, t, d_ff)[:, t - (CONV_W - 1):]
    return y.reshape(bd, t, d), k_new, v_new, wkv, shift_new, conv_new


def kernel(x_prompt, x_sample, cache_k, cache_v, page_table, state_wkv, state_shift, state_conv,
           w_in, w_out, shift_mu, w0, w2, a0, a2, g2, k_k, k_a, r_k, gn_g, gn_b,
           ln1_g, ln1_b, ln2_g, ln2_b, w_up, conv_w, conv_b, w_down):
    depth = w_in.shape[0]
    alpha = (2.0 * depth) ** 0.25
    assert x_sample.shape[1] >= CONV_W - 1
    yp, ys = x_prompt, x_sample
    outs_p, outs_s = [], []
    for l in range(depth):
        lw = _prep_layer(w_in[l], w_out[l], shift_mu[l], w0[l], w2[l], a0[l], a2[l], g2[l], k_k[l], k_a[l],
                         r_k[l], gn_g[l], gn_b[l], ln1_g[l], ln1_b[l], ln2_g[l], ln2_b[l],
                         w_up[l], conv_w[l], conv_b[l], w_down[l])
        yp, *rest_p = _prompt_layer(yp, lw, alpha)
        outs_p.append(rest_p)
        ys, *rest_s = _decode_layer(ys, cache_k[l], cache_v[l], page_table, state_shift[l], state_wkv[l],
                                    state_conv[l], lw, alpha)
        outs_s.append(rest_s)
    stack = lambda outs, n: jnp.stack([o[n] for o in outs])
    return (yp, ys,
            stack(outs_p, 0), stack(outs_p, 1), stack(outs_p, 2), stack(outs_p, 3), stack(outs_p, 4),
            stack(outs_s, 0), stack(outs_s, 1), stack(outs_s, 2), stack(outs_s, 3), stack(outs_s, 4))
```

```python
import functools

import jax
import jax.numpy as jnp
from jax import lax
from jax.experimental import pallas as pl
from jax.experimental.pallas import tpu as pltpu

F32 = jnp.float32
BF16 = jnp.bfloat16

HEAD_DIM = 64
N_ATT_HEADS = 16
N_KV_HEADS = 4
KV_GROUP = N_ATT_HEADS // N_KV_HEADS
ATT_WIDTH = N_ATT_HEADS * HEAD_DIM
KV_WIDTH = N_KV_HEADS * HEAD_DIM
N_RWKV_HEADS = 16
RWKV_WIDTH = N_RWKV_HEADS * HEAD_DIM
ROPE_DIM = HEAD_DIM // 4
ROPE_THETA = 500000.0
MOBA_BLOCK = 256
MOBA_TOPK = 3
D_DECAY_LORA = 64
D_AAA_LORA = 64
D_GATE_LORA = 160
LORA_WIDTH = D_DECAY_LORA + D_AAA_LORA + D_GATE_LORA
RWKV_PROJ = 3 * RWKV_WIDTH + LORA_WIDTH
CONV_W = 3
LN_EPS = 1e-5
GN_EPS = 64e-5
NEG_INF = -1e30
LOG2_E = 1.4426950408889634
PAGE_SIZE = 128

LANES = 128
SUBLANES = 8
VMEM_LIMIT_BYTES = 56 * 1024 * 1024

PROJ_TN = 512
Z_PAD = 3584
Q_COL = 0
K_COL = Q_COL + ATT_WIDTH
V_COL = K_COL + KV_WIDTH
Z_COL = V_COL + KV_WIDTH
PROJ_W = Z_COL + Z_PAD
LORA_COL = 3 * RWKV_WIDTH
LORA_BLOCK = 384
GATE_LORA_PAD = 256

RWKV_CHUNK = HEAD_DIM
RWKV_SEQS_PER_STEP = 8
RWKV_STAGE_GROUP = 8
PAGES_PER_STEP = 64

_NN = (((1,), (0,)), ((), ()))
_NT = (((1,), (1,)), ((), ()))


def _dot(a, b, dims=_NN):
    return lax.dot_general(a, b, dims, preferred_element_type=F32)


def _split_bf16(x):
    hi = x.astype(BF16)
    lo = (x - hi.astype(F32)).astype(BF16)
    return hi, lo


def _dot3(a, b, dims=_NN):
    ah, al = _split_bf16(a)
    bh, bl = _split_bf16(b)
    return _dot(ah, bh, dims) + (_dot(ah, bl, dims) + _dot(al, bh, dims))


def _sigmoid(x):
    return 1.0 / (1.0 + jnp.exp(-x))


def _layer_norm(x, g, b):
    mu = jnp.mean(x, axis=-1, keepdims=True)
    d = x - mu
    var = jnp.mean(d * d, axis=-1, keepdims=True)
    return d * lax.rsqrt(var + LN_EPS) * g + b


def _params(sem):
    return pltpu.CompilerParams(dimension_semantics=sem, vmem_limit_bytes=VMEM_LIMIT_BYTES)


def _rope_tables(pos):
    half = ROPE_DIM // 2
    inv = jnp.power(ROPE_THETA, -(jnp.arange(half, dtype=F32) * 2.0 / ROPE_DIM))
    ang = pos.astype(F32)[:, None] * inv[None, :]
    cos = jnp.cos(ang)
    sin = jnp.sin(ang)
    t = pos.shape[0]
    one = jnp.ones((t, HEAD_DIM - ROPE_DIM), F32)
    z8 = jnp.zeros((t, half), F32)
    zr = jnp.zeros((t, HEAD_DIM - ROPE_DIM), F32)
    c64 = jnp.concatenate([cos, cos, one], axis=1)
    a64 = jnp.concatenate([-sin, z8, zr], axis=1)
    b64 = jnp.concatenate([z8, sin, zr], axis=1)
    rep = LANES // HEAD_DIM
    return tuple(jnp.tile(m, (1, rep)) for m in (c64, a64, b64))


def _in_proj_kernel(x_ref, w_ref, cos_ref, sa_ref, sb_ref, proj_ref, *rest, attn_extras):
    if attn_extras:
        k_ref, v_ref, kh_ref, vt_ref, xb_ref = rest
    else:
        (xb_ref,) = rest
    j = pl.program_id(2)
    n_q = ATT_WIDTH // PROJ_TN
    half = ROPE_DIM // 2

    @pl.when(j == 0)
    def _():
        xb_ref[...] = x_ref[0].astype(BF16)

    acc = _dot(xb_ref[...], w_ref[...])

    def rope(slab):
        return (slab * cos_ref[...] + pltpu.roll(slab, LANES - half, 1) * sa_ref[...]
                + pltpu.roll(slab, half, 1) * sb_ref[...])

    @pl.when(j > n_q)
    def _():
        proj_ref[0] = acc

    @pl.when(j < n_q)
    def _():
        for s in range(PROJ_TN // LANES):
            proj_ref[0, :, s * LANES:(s + 1) * LANES] = rope(acc[:, s * LANES:(s + 1) * LANES])

    @pl.when(j == n_q)
    def _():
        k_slabs = []
        for s in range(KV_WIDTH // LANES):
            ks = rope(acc[:, s * LANES:(s + 1) * LANES])
            proj_ref[0, :, s * LANES:(s + 1) * LANES] = ks
            k_slabs.append(ks)
        v = acc[:, KV_WIDTH:]
        proj_ref[0, :, KV_WIDTH:] = v
        if attn_extras:
            k = jnp.concatenate(k_slabs, axis=1)
            k_ref[0] = k
            v_ref[0] = v
            tm = acc.shape[0]
            row_blk = (pl.program_id(1) * tm + lax.broadcasted_iota(jnp.int32, (tm, HEAD_DIM), 0)) // MOBA_BLOCK
            one_hot = (row_blk == lax.broadcasted_iota(jnp.int32, (tm, HEAD_DIM), 1)).astype(BF16)
            for c in range(N_KV_HEADS):
                kh_ref[0, c] = jnp.concatenate(
                    [k[:, c * HEAD_DIM:(c + 1) * HEAD_DIM].astype(BF16), one_hot], axis=1)
            vt_ref[0] = v.T.astype(BF16)


def _in_proj(x, w_b, pos, *, tm, attn_extras):
    b, s, d = x.shape
    assert s % tm == 0 and PROJ_W % PROJ_TN == 0
    cos, sa, sb = _rope_tables(pos)
    tab_spec = pl.BlockSpec((tm, LANES), lambda bi, i, j: (i, 0))
    out_shape = [jax.ShapeDtypeStruct((b, s, PROJ_W), F32)]
    out_specs = [pl.BlockSpec((1, tm, PROJ_TN), lambda bi, i, j: (bi, i, j))]
    if attn_extras:
        assert s // MOBA_BLOCK <= HEAD_DIM
        kv_rows = pl.BlockSpec((1, tm, KV_WIDTH), lambda bi, i, j: (bi, i, 0))
        out_shape += [jax.ShapeDtypeStruct((b, s, KV_WIDTH), F32),
                      jax.ShapeDtypeStruct((b, s, KV_WIDTH), F32),
                      jax.ShapeDtypeStruct((b, N_KV_HEADS, s, 2 * HEAD_DIM), BF16),
                      jax.ShapeDtypeStruct((b, KV_WIDTH, s), BF16)]
        out_specs += [kv_rows, kv_rows,
                      pl.BlockSpec((1, N_KV_HEADS, tm, 2 * HEAD_DIM), lambda bi, i, j: (bi, 0, i, 0)),
                      pl.BlockSpec((1, KV_WIDTH, tm), lambda bi, i, j: (bi, 0, i))]
    return pl.pallas_call(
        functools.partial(_in_proj_kernel, attn_extras=attn_extras),
        grid=(b, s // tm, PROJ_W // PROJ_TN),
        in_specs=[pl.BlockSpec((1, tm, d), lambda bi, i, j: (bi, i, 0)),
                  pl.BlockSpec((d, PROJ_TN), lambda bi, i, j: (0, j)),
                  tab_spec, tab_spec, tab_spec],
        out_specs=out_specs,
        out_shape=out_shape,
        scratch_shapes=[pltpu.VMEM((tm, d), BF16)],
        compiler_params=_params(("parallel", "parallel", "arbitrary")),
        name="in_proj",
    )(x, w_b, cos, sa, sb)


def _moba_prompt_kernel(q_ref, kh_ref, vt_ref, o_ref, kmean_ref, *, n_blk, topk):
    i = pl.program_id(2)
    blk = MOBA_BLOCK
    n_cand = n_blk - 1
    scale = HEAD_DIM ** -0.5

    @pl.when(i == 0)
    def _():
        kmean_ref[...] = jnp.zeros_like(kmean_ref)
        for n in range(n_blk):
            kmean_ref[n:n + 1, :] = jnp.mean(
                kh_ref[0, 0, n * blk:(n + 1) * blk, :].astype(F32), axis=0, keepdims=True)

    q_t = q_ref[0].T
    qs_t = jnp.concatenate([q_t[g * HEAD_DIM:(g + 1) * HEAD_DIM, :] for g in range(KV_GROUP)],
                           axis=1)
    nq = KV_GROUP * blk

    gate = _dot3(kmean_ref[...], jnp.concatenate([qs_t, jnp.zeros((HEAD_DIM, nq), F32)], axis=0))
    n_idx = lax.broadcasted_iota(jnp.int32, gate.shape, 0)
    gate = jnp.where(n_idx < i, gate, NEG_INF)
    rank = jnp.zeros(gate.shape, F32)
    for m in range(n_cand):
        gm = gate[m:m + 1, :]
        beats = (gm > gate) | ((gm == gate) & (m < n_idx))
        rank = rank + beats.astype(F32)
    blk_bias = jnp.where(((rank < topk) & (n_idx < i)) | (n_idx == i), 0.0, NEG_INF)
    q_aug = jnp.concatenate(
        [qs_t * (scale * LOG2_E), blk_bias, jnp.zeros((HEAD_DIM - blk_bias.shape[0], nq), F32)],
        axis=0).astype(BF16)

    kpos = lax.broadcasted_iota(jnp.int32, (blk, nq), 0)
    qpos = lax.broadcasted_iota(jnp.int32, (blk, nq), 1) & (blk - 1)
    causal_bias = jnp.where(kpos <= qpos, 0.0, NEG_INF)

    def attend(nb):
        s = _dot(kh_ref[0, 0, 0:nb * blk, :], q_aug)
        own = s[(nb - 1) * blk:, :] + causal_bias
        s = jnp.concatenate([s[:(nb - 1) * blk, :], own], axis=0) if nb > 1 else own
        p = jnp.exp2(s - jnp.max(s, axis=0, keepdims=True))
        l = jnp.sum(p, axis=0, keepdims=True)
        o = _dot(vt_ref[0, :, 0:nb * blk], p.astype(BF16)) / l
        o4 = jnp.concatenate([o[:, g * blk:(g + 1) * blk] for g in range(KV_GROUP)], axis=0)
        o_ref[0] = o4.T.astype(o_ref.dtype)

    for nb in range(1, n_blk + 1):
        pl.when(i == nb - 1)(functools.partial(attend, nb))


def _moba_prompt(proj, kh, vt):
    b, s, _ = proj.shape
    assert s % MOBA_BLOCK == 0
    n_blk = s // MOBA_BLOCK
    topk = min(MOBA_TOPK, n_blk - 1)
    gw = KV_GROUP * HEAD_DIM
    n_rows = -(-n_blk // SUBLANES) * SUBLANES
    assert n_rows <= HEAD_DIM
    return pl.pallas_call(
        functools.partial(_moba_prompt_kernel, n_blk=n_blk, topk=topk),
        grid=(b, N_KV_HEADS, n_blk),
        in_specs=[pl.BlockSpec((1, MOBA_BLOCK, gw), lambda bi, c, i: (bi, i, Q_COL // gw + c)),
                  pl.BlockSpec((1, 1, s, 2 * HEAD_DIM), lambda bi, c, i: (bi, c, 0, 0)),
                  pl.BlockSpec((1, HEAD_DIM, s), lambda bi, c, i: (bi, c, 0))],
        out_specs=pl.BlockSpec((1, MOBA_BLOCK, gw), lambda bi, c, i: (bi, i, c)),
        out_shape=jax.ShapeDtypeStruct((b, s, ATT_WIDTH), BF16),
        scratch_shapes=[pltpu.VMEM((n_rows, 2 * HEAD_DIM), F32)],
        compiler_params=_params(("parallel", "parallel", "arbitrary")),
        name="moba_prompt",
    )(proj, kh, vt)


def _moba_decode_scores_kernel(pt_ref, wt_ref, *rest, n_new):
    del pt_ref
    r_pages = PAGES_PER_STEP
    k_refs = rest[:r_pages]
    s_ref, gate_ref, bmax_ref = rest[r_pages:]
    j = pl.program_id(1)
    scale = HEAD_DIM ** -0.5
    nq = N_ATT_HEADS * n_new
    pages_per_blk = MOBA_BLOCK // PAGE_SIZE
    blks_per_step = r_pages // pages_per_blk
    lane = lax.broadcasted_iota(jnp.int32, gate_ref.shape[1:], 1)

    @pl.when(j == 0)
    def _():
        gate_ref[0] = jnp.zeros(gate_ref.shape[1:], F32)
        bmax_ref[0] = jnp.full(bmax_ref.shape[1:], NEG_INF, F32)

    wt2 = jnp.concatenate(_split_bf16(wt_ref[0]), axis=0)
    gate = gate_ref[0]
    bmax = bmax_ref[0]
    for r in range(r_pages):
        s2 = _dot(wt2, k_refs[r][0].astype(BF16))
        raw = s2[:nq] + s2[nq:]
        s_ref[0, :, r * PAGE_SIZE:(r + 1) * PAGE_SIZE] = raw * scale
        first = r % pages_per_blk == 0
        blk_sum = raw if first else blk_sum + raw
        blk_max = raw if first else jnp.maximum(blk_max, raw)
        if r % pages_per_blk == pages_per_blk - 1:
            here = lane == j * blks_per_step + r // pages_per_blk
            gate = jnp.where(here, jnp.sum(blk_sum, axis=1, keepdims=True), gate)
            bmax = jnp.where(here, jnp.max(blk_max, axis=1, keepdims=True) * scale, bmax)
    gate_ref[0] = gate
    bmax_ref[0] = bmax


def _moba_decode_values_kernel(pt_ref, wt_ref, kvn_ref, gate_ref, bmax_ref, s_ref, *rest,
                               n_cand, n_new):
    del pt_ref
    r_pages = PAGES_PER_STEP
    v_refs = rest[:r_pages]
    o_ref = rest[r_pages]
    sel_ref, acc_ref, l_ref, m_ref = rest[r_pages + 1:]
    n_cand_pad = gate_ref.shape[2]
    j = pl.program_id(1)
    scale = HEAD_DIM ** -0.5
    pages_per_blk = MOBA_BLOCK // PAGE_SIZE
    blks_per_step = r_pages // pages_per_blk

    @pl.when(j == 0)
    def _():
        lane = lax.broadcasted_iota(jnp.int32, gate_ref.shape[1:], 1)
        lane_f = lane.astype(F32)
        gate = jnp.where(lane < n_cand, gate_ref[0] * (1.0 / MOBA_BLOCK), NEG_INF)
        sel = lane < 0
        for _ in range(min(MOBA_TOPK, n_cand)):
            top = jnp.max(gate, axis=1, keepdims=True)
            first_top = jnp.min(jnp.where(gate == top, lane_f, float(n_cand_pad)), axis=1, keepdims=True)
            pick = lane_f == first_top
            sel = sel | pick
            gate = jnp.where(pick, -jnp.inf, gate)
        sel = sel & (lane < n_cand)
        sel_ref[...] = sel.astype(F32)

        kvn = kvn_ref[0]
        k_new = kvn[:, :KV_WIDTH].astype(BF16)
        v_new = kvn[:, KV_WIDTH:].astype(BF16)
        s_new = _dot((wt_ref[0] * scale).astype(BF16), k_new, _NT)
        t_q = lax.broadcasted_iota(jnp.int32, s_new.shape, 0) % n_new
        t_k = lax.broadcasted_iota(jnp.int32, s_new.shape, 1)
        ok_new = (t_k <= t_q) & (t_k < n_new)
        m = jnp.maximum(jnp.max(jnp.where(ok_new, s_new, NEG_INF), axis=1, keepdims=True),
                        jnp.max(jnp.where(sel, bmax_ref[0], NEG_INF), axis=1, keepdims=True))
        p_new = jnp.where(ok_new, jnp.exp(s_new - m), 0.0)
        m_ref[...] = jnp.broadcast_to(m, m_ref.shape)
        l_lane = lax.broadcasted_iota(jnp.int32, l_ref.shape, 1)
        l_ref[...] = jnp.where(l_lane == 0, jnp.sum(p_new, axis=1, keepdims=True), 0.0)
        acc_ref[...] = _dot(p_new.astype(BF16), v_new)

    src = lax.broadcasted_iota(jnp.int32, (n_cand_pad, LANES), 0)
    dst = lax.broadcasted_iota(jnp.int32, (n_cand_pad, LANES), 1)
    onehot = ((src == j * blks_per_step + dst) & (dst < blks_per_step)).astype(BF16)
    sel_here = _dot(sel_ref[...].astype(BF16), onehot)
    m = m_ref[:, 0:1]
    acc = acc_ref[...]
    l_part = l_ref[...]
    for r in range(r_pages):
        blk = r // pages_per_blk
        sb = s_ref[0, :, r * PAGE_SIZE:(r + 1) * PAGE_SIZE]
        pb = jnp.where(sel_here[:, blk:blk + 1] > 0.5, jnp.exp(sb - m), 0.0)
        l_part = l_part + pb
        acc = acc + _dot(pb.astype(BF16), v_refs[r][0].astype(BF16), _NT)
    acc_ref[...] = acc
    l_ref[...] = l_part

    @pl.when(j == pl.num_programs(1) - 1)
    def _():
        o_ref[0] = acc / jnp.sum(l_part, axis=1, keepdims=True)


def _moba_decode(proj_d, cache_k, cache_v, page_table, n_new):
    bd = proj_d.shape[0]
    n_pages = page_table.shape[1]
    past = n_pages * PAGE_SIZE
    assert past % MOBA_BLOCK == 0 and n_pages % PAGES_PER_STEP == 0 and n_new <= SUBLANES
    n_cand = past // MOBA_BLOCK
    n_steps = n_pages // PAGES_PER_STEP
    n_phys = cache_k.shape[0]
    nq = N_ATT_HEADS * n_new
    ck = jnp.transpose(cache_k, (0, 2, 3, 1)).reshape(n_phys, KV_WIDTH, PAGE_SIZE)
    cv = jnp.transpose(cache_v, (0, 2, 3, 1)).reshape(n_phys, KV_WIDTH, PAGE_SIZE)

    q = proj_d[:, :, Q_COL:Q_COL + ATT_WIDTH].reshape(bd, n_new, N_KV_HEADS, KV_GROUP, HEAD_DIM)
    q = jnp.transpose(q, (0, 2, 3, 1, 4)).reshape(bd, N_KV_HEADS, KV_GROUP * n_new, 1, HEAD_DIM)
    eye = jnp.eye(N_KV_HEADS, dtype=F32)[None, :, None, :, None]
    wt = (q * eye).reshape(bd, nq, KV_WIDTH)
    kvn = jnp.pad(proj_d[:, :, K_COL:K_COL + 2 * KV_WIDTH], ((0, 0), (0, SUBLANES - n_new), (0, 0)))

    def page_map(r):
        return lambda b, j, pt: (pt[b, j * PAGES_PER_STEP + r], 0, 0)

    step_keys = PAGES_PER_STEP * PAGE_SIZE
    n_cand_pad = -(-n_cand // LANES) * LANES
    per_seq = lambda w: pl.BlockSpec((1, nq, w), lambda b, j, pt: (b, 0, 0))
    page_specs = [pl.BlockSpec((1, KV_WIDTH, PAGE_SIZE), page_map(r)) for r in range(PAGES_PER_STEP)]
    score_spec = pl.BlockSpec((1, nq, step_keys), lambda b, j, pt: (b, 0, j))
    scores, gate, bmax = pl.pallas_call(
        functools.partial(_moba_decode_scores_kernel, n_new=n_new),
        grid_spec=pltpu.PrefetchScalarGridSpec(
            num_scalar_prefetch=1,
            grid=(bd, n_steps),
            in_specs=[per_seq(KV_WIDTH)] + page_specs,
            out_specs=[score_spec, per_seq(n_cand_pad), per_seq(n_cand_pad)]),
        out_shape=[jax.ShapeDtypeStruct((bd, nq, past), F32),
                   jax.ShapeDtypeStruct((bd, nq, n_cand_pad), F32),
                   jax.ShapeDtypeStruct((bd, nq, n_cand_pad), F32)],
        compiler_params=_params(("parallel", "arbitrary")),
        name="moba_decode_scores",
    )(page_table, wt, *([ck] * PAGES_PER_STEP))
    out = pl.pallas_call(
        functools.partial(_moba_decode_values_kernel, n_cand=n_cand, n_new=n_new),
        grid_spec=pltpu.PrefetchScalarGridSpec(
            num_scalar_prefetch=1,
            grid=(bd, n_steps),
            in_specs=[per_seq(KV_WIDTH),
                      pl.BlockSpec((1, SUBLANES, 2 * KV_WIDTH), lambda b, j, pt: (b, 0, 0)),
                      per_seq(n_cand_pad), per_seq(n_cand_pad), score_spec] + page_specs,
            out_specs=per_seq(KV_WIDTH),
            scratch_shapes=[pltpu.VMEM((nq, n_cand_pad), F32),
                            pltpu.VMEM((nq, KV_WIDTH), F32),
                            pltpu.VMEM((nq, LANES), F32),
                            pltpu.VMEM((nq, LANES), F32)]),
        out_shape=jax.ShapeDtypeStruct((bd, nq, KV_WIDTH), F32),
        compiler_params=_params(("parallel", "arbitrary")),
        name="moba_decode_values",
    )(page_table, wt, kvn, gate, bmax, scores, *([cv] * PAGES_PER_STEP))
    o = out.reshape(bd, N_KV_HEADS, KV_GROUP, n_new, N_KV_HEADS, HEAD_DIM)
    o = jnp.stack([o[:, c, :, :, c, :] for c in range(N_KV_HEADS)], axis=1)
    o = jnp.transpose(o, (0, 3, 1, 2, 4)).reshape(bd, n_new, ATT_WIDTH)
    return o.astype(BF16)


def _rwkv_kernel(zr_ref, zk_ref, zv_ref, zl_ref, sr_ref, sk_ref, sv_ref, sl_ref,
                 mu_ref, mul_ref, par_ref, w2_ref, a2_ref, g2_ref, s0_ref,
                 rw_ref, wkv_ref,
                 state_ref, pr_ref, pk_ref, pv_ref, plo_ref, *, t_real, independent):
    tb = pl.program_id(2)
    n_tb = pl.num_programs(2)
    tblk = zr_ref.shape[1]
    c_len = RWKV_CHUNK
    width = LANES
    lane = lax.broadcasted_iota(jnp.int32, (1, width), 1)
    head_a = lane < HEAD_DIM
    row = lax.broadcasted_iota(jnp.int32, (tblk, 1), 0)

    def block_diag(sa, sb):
        z = jnp.zeros((HEAD_DIM, HEAD_DIM), F32)
        return jnp.concatenate([jnp.concatenate([sa, z], axis=1),
                                jnp.concatenate([z, sb], axis=1)], axis=0)

    if independent:
        def token_shift(z, first_ref, prev_ref, mu):
            z_prev = pltpu.roll(z, 1, 0)
            first = first_ref[0]
            for c in range(tblk // c_len):
                z_prev = jnp.where(row == c * c_len, first[c:c + 1, :], z_prev)
            return z + (z_prev - z) * mu
    else:
        @pl.when(tb == 0)
        def _():
            state_ref[...] = block_diag(s0_ref[0, 0], s0_ref[0, 1])
            pr_ref[0:1, :] = sr_ref[0]
            pk_ref[0:1, :] = sk_ref[0]
            pv_ref[0:1, :] = sv_ref[0]
            plo_ref[0:1, :] = sl_ref[0]

        def token_shift(z, first_ref, prev_ref, mu):
            z_prev = jnp.where(row == 0, prev_ref[0:1, :], pltpu.roll(z, 1, 0))
            prev_ref[0:1, :] = z[tblk - 1:tblk, :]
            return z + (z_prev - z) * mu

    def seg_sum(x):
        sa = jnp.sum(jnp.where(head_a, x, 0.0), axis=1, keepdims=True)
        sb = jnp.sum(jnp.where(head_a, 0.0, x), axis=1, keepdims=True)
        return jnp.where(head_a, sa, sb)

    r = token_shift(zr_ref[0], sr_ref, pr_ref, mu_ref[0:1, :])
    k = token_shift(zk_ref[0], sk_ref, pk_ref, mu_ref[1:2, :])
    v = token_shift(zv_ref[0], sv_ref, pv_ref, mu_ref[2:3, :])
    lo = token_shift(zl_ref[0], sl_ref, plo_ref, mul_ref[...])
    par = par_ref[...]
    w0, a0, k_k, k_a, r_k, gn_g, gn_b = (par[n:n + 1, :] for n in range(7))

    lo_wa = lo[:, :LANES]
    nx = -(w0 + _dot(jnp.tanh(lo_wa).astype(BF16), w2_ref[...]))
    softplus = jnp.maximum(nx, 0.0) + jnp.log(1.0 + jnp.exp(-jnp.abs(nx)))
    logw = -jnp.exp(-softplus - 0.5)
    a = _sigmoid(a0 + _dot(lo_wa.astype(BF16), a2_ref[...]))
    g = _dot(_sigmoid(lo[:, LANES:]).astype(BF16), g2_ref[...])
    kk = k * k_k
    kk = kk / jnp.maximum(jnp.sqrt(seg_sum(kk * kk)), 1e-12)
    k = k * (1.0 + (a - 1.0) * k_a)
    bonus = seg_sum(r * k * r_k) * v

    if t_real is not None:
        valid = ((row & (c_len - 1)) if independent else (tb * tblk + row)) < t_real
        logw = jnp.where(valid, logw, 0.0)
        kk = jnp.where(valid, kk, 0.0)
        k = jnp.where(valid, k, 0.0)
        v_in = jnp.where(valid, v, 0.0)
    else:
        v_in = v

    def pair_rows(x):
        return jnp.concatenate([jnp.where(head_a, x, 0.0), jnp.where(head_a, 0.0, x)], axis=0)

    ri = lax.broadcasted_iota(jnp.int32, (c_len, 2 * c_len), 0)
    ci = lax.broadcasted_iota(jnp.int32, (c_len, 2 * c_len), 1)
    first = ci < c_len
    tcol = ci & (c_len - 1)
    strict = tcol < ri
    incl = tcol <= ri
    eye2 = (tcol == ri).astype(F32)
    bi = lax.broadcasted_iota(jnp.int32, (width, width), 0) < HEAD_DIM
    bj = lax.broadcasted_iota(jnp.int32, (width, width), 1) < HEAD_DIM
    same_head = bi == bj
    live = c_len if t_real is None else min(t_real, c_len)
    n_levels = max((live - 1).bit_length() - 1, 0)
    shift = c_len.bit_length() - 1
    chunks = [slice(c * c_len, (c + 1) * c_len) for c in range(tblk // c_len)]
    bf = lambda x: x.astype(BF16)

    tr = lax.broadcasted_iota(jnp.int32, (tblk, tblk), 0)
    tc = lax.broadcasted_iota(jnp.int32, (tblk, tblk), 1)
    tri = ((tr >= tc) & ((tr >> shift) == (tc >> shift))).astype(BF16)
    lh = logw.astype(BF16)
    l1 = logw - lh.astype(F32)
    lm = l1.astype(BF16)
    ll = (l1 - lm.astype(F32)).astype(BF16)
    lc = _dot(tri, lh) + (_dot(tri, lm) + _dot(tri, ll))
    g_in = jnp.exp(lc)
    g_inv = jnp.exp(-lc)
    a_all = -kk * jnp.exp(lc - logw)
    b_all = kk * a * g_inv
    k_all = k * g_inv
    r_all = r * g_in

    def group_terms(sls):
        a_t = [a_all[sl] for sl in sls]
        r_t = [r_all[sl] for sl in sls]
        b_t = [b_all[sl] for sl in sls]
        k_t = [k_all[sl] for sl in sls]
        v_c = [v_in[sl] for sl in sls]
        g_end = [g_in[sl.stop - 1:sl.stop, :] for sl in sls]
        a_pair = [pair_rows(bf(x)) for x in a_t]
        s4 = [_dot(jnp.concatenate([ap, pair_rows(bf(rt))], axis=0),
                   bf(jnp.concatenate([bt, kt], axis=0)), _NT)
              for ap, rt, bt, kt in zip(a_pair, r_t, b_t, k_t)]
        m_ab, m_ak, m_rb, m_rk = [], [], [], []
        for x in s4:
            sa_, sb_ = x[0:c_len], x[c_len:2 * c_len]
            ra_, rb_ = x[2 * c_len:3 * c_len], x[3 * c_len:4 * c_len]
            m_ab.append(jnp.where(strict, jnp.where(first, sa_, pltpu.roll(sb_, c_len, 1)), 0.0))
            m_ak.append(jnp.where(strict, jnp.where(first, pltpu.roll(sa_, c_len, 1), sb_), 0.0))
            m_rb.append(jnp.where(incl, jnp.where(first, ra_, pltpu.roll(rb_, c_len, 1)), 0.0))
            m_rk.append(jnp.where(incl, jnp.where(first, pltpu.roll(ra_, c_len, 1), rb_), 0.0))

        t_inv = [m + eye2 for m in m_ab]
        if n_levels > 0:
            l_pow = [bf(m) for m in m_ab]
            l_pow = [bf(_dot(lp, pair_rows(lp))) for lp in l_pow]
            for level in range(n_levels):
                qp = [_dot(lp, jnp.concatenate([pair_rows(lp), pair_rows(bf(ti))], axis=1))
                      for lp, ti in zip(l_pow, t_inv)]
                t_inv = [ti + x[:, width:] for ti, x in zip(t_inv, qp)]
                if level + 1 < n_levels:
                    l_pow = [bf(x[:, :width]) for x in qp]

        v_pair = [pair_rows(bf(x)) for x in v_c]
        mak_v = [_dot(bf(m), vp) for m, vp in zip(m_ak, v_pair)]
        tw = [_dot(bf(ti), jnp.concatenate([ap, pair_rows(bf(mv))], axis=1))
              for ti, ap, mv in zip(t_inv, a_pair, mak_v)]
        w_t = [x[:, :width] for x in tw]
        v_hat = [x[:, width:] for x in tw]
        mw = [_dot(bf(jnp.concatenate([mb, mk], axis=1)),
                   jnp.concatenate(
                       [jnp.concatenate([pair_rows(bf(w)), pair_rows(bf(vh))], axis=1),
                        jnp.concatenate([jnp.zeros_like(vp), vp], axis=1)], axis=0))
              for mb, mk, w, vh, vp in zip(m_rb, m_rk, w_t, v_hat, v_pair)]
        p_m = [rt + x[:, :width] for rt, x in zip(r_t, mw)]
        q_m = [x[:, width:] for x in mw]
        phi, psi = [], []
        for w, vh, vc, bt, kt, ge in zip(w_t, v_hat, v_c, b_t, k_t, g_end):
            bk = bf(jnp.concatenate([bt * ge, kt * ge], axis=0))
            left = jnp.concatenate(
                [jnp.concatenate([w, jnp.zeros_like(w)], axis=0),
                 jnp.concatenate([vh, vc], axis=0)], axis=1)
            pp = _dot(bf(left.T), bk)
            phi.append(bf(jnp.where(same_head, pp[:width], 0.0)))
            psi.append(jnp.where(same_head, pp[width:], 0.0))
        return list(zip(p_m, q_m, phi, psi, g_end))

    def advance(s, term):
        pm, qm, ph, ps, ge = term
        sb16 = bf(s)
        return _dot(bf(pm), sb16, _NT) + qm, s * ge + _dot(sb16, ph) + ps

    outs = []
    s = None if independent else state_ref[...]
    for g0 in range(0, len(chunks), RWKV_STAGE_GROUP):
        for c, term in enumerate(group_terms(chunks[g0:g0 + RWKV_STAGE_GROUP]), start=g0):
            if independent:
                o_c, s_c = advance(block_diag(s0_ref[c, 0], s0_ref[c, 1]), term)
                wkv_ref[c, 0] = s_c[:HEAD_DIM, :HEAD_DIM]
                wkv_ref[c, 1] = s_c[HEAD_DIM:, HEAD_DIM:]
            else:
                o_c, s = advance(s, term)
            outs.append(o_c)
    if not independent:
        state_ref[...] = s

        @pl.when(tb == n_tb - 1)
        def _():
            wkv_ref[0, 0] = s[:HEAD_DIM, :HEAD_DIM]
            wkv_ref[0, 1] = s[HEAD_DIM:, HEAD_DIM:]

    o = jnp.concatenate(outs, axis=0) if len(outs) > 1 else outs[0]
    inv_n = 1.0 / HEAD_DIM
    mu_o = seg_sum(o) * inv_n
    d = o - mu_o
    var = seg_sum(d * d) * inv_n
    on = d * lax.rsqrt(var + GN_EPS) * gn_g + gn_b
    rw_ref[0] = ((on + bonus) * g).astype(rw_ref.dtype)


def _rwkv(proj, shift0, wkv0, lw, *, tblk, t_real, independent=False):
    b, s, _ = proj.shape
    assert s % tblk == 0 and tblk % RWKV_CHUNK == 0
    n_pairs = RWKV_WIDTH // LANES
    rkv = RWKV_WIDTH // LANES
    lora_blk = LORA_COL // LORA_BLOCK
    n_first = shift0.shape[1]
    n_state = tblk // RWKV_CHUNK if independent else 1
    assert (not independent) or (s == tblk and n_first == n_state)

    def z_spec(width, col0):
        return pl.BlockSpec((1, tblk, width), lambda bi, p, t: (bi, t, col0(p)))

    def s_spec(width, col0):
        return pl.BlockSpec((1, n_first, width), lambda bi, p, t: (bi, 0, col0(p)))

    state_spec = pl.BlockSpec((n_state, 2, HEAD_DIM, HEAD_DIM), lambda bi, p, t: (bi, p, 0, 0))

    assert Z_COL % LANES == 0 and (Z_COL + LORA_COL) % LORA_BLOCK == 0
    cols = [lambda p: p, lambda p: rkv + p, lambda p: 2 * rkv + p]
    zc = Z_COL // LANES
    in_specs = ([z_spec(LANES, lambda p, c=c: zc + c(p)) for c in cols]
                + [z_spec(LORA_BLOCK, lambda p: (Z_COL + LORA_COL) // LORA_BLOCK)]
                + [s_spec(LANES, c) for c in cols] + [s_spec(LORA_BLOCK, lambda p: lora_blk)]
                + [pl.BlockSpec((SUBLANES, LANES), lambda bi, p, t: (0, p)),
                   pl.BlockSpec((1, LORA_BLOCK), lambda bi, p, t: (0, 0)),
                   pl.BlockSpec((SUBLANES, LANES), lambda bi, p, t: (0, p)),
                   pl.BlockSpec((LANES, LANES), lambda bi, p, t: (0, p)),
                   pl.BlockSpec((LANES, LANES), lambda bi, p, t: (0, p)),
                   pl.BlockSpec((GATE_LORA_PAD, LANES), lambda bi, p, t: (0, p)),
                   state_spec])
    return pl.pallas_call(
        functools.partial(_rwkv_kernel, t_real=t_real, independent=independent),
        grid=(b, n_pairs, s // tblk),
        in_specs=in_specs,
        out_specs=[pl.BlockSpec((1, tblk, LANES), lambda bi, p, t: (bi, t, p)), state_spec],
        out_shape=[jax.ShapeDtypeStruct((b, s, RWKV_WIDTH), BF16),
                   jax.ShapeDtypeStruct((b * n_state, N_RWKV_HEADS, HEAD_DIM, HEAD_DIM), F32)],
        scratch_shapes=[pltpu.VMEM((LANES, LANES), F32),
                        pltpu.VMEM((SUBLANES, LANES), F32),
                        pltpu.VMEM((SUBLANES, LANES), F32),
                        pltpu.VMEM((SUBLANES, LANES), F32),
                        pltpu.VMEM((SUBLANES, LORA_BLOCK), F32)],
        compiler_params=_params(("parallel", "parallel", "arbitrary")),
        name="rwkv7",
    )(proj, proj, proj, proj, shift0, shift0, shift0, shift0,
      lw["mu_rkv"], lw["mu_lora"], lw["rwkv_par"], lw["w2"], lw["a2"], lw["g2"], wkv0)


def _out_ln_kernel(att_ref, rw_ref, x_ref, wa_ref, wr_ref, g_ref, b_ref, o_ref, *, alpha):
    mix = _dot(att_ref[...], wa_ref[...]) + _dot(rw_ref[...], wr_ref[...])
    o_ref[...] = _layer_norm(alpha * x_ref[...] + mix, g_ref[...], b_ref[...])


def _out_ln(att, rw, x, lw, *, tm, alpha):
    m, d = x.shape
    assert m % tm == 0
    row = lambda w: pl.BlockSpec((tm, w), lambda i: (i, 0))
    full = lambda r, w: pl.BlockSpec((r, w), lambda i: (0, 0))
    return pl.pallas_call(
        functools.partial(_out_ln_kernel, alpha=alpha),
        grid=(m // tm,),
        in_specs=[row(ATT_WIDTH), row(RWKV_WIDTH), row(d),
                  full(ATT_WIDTH, d), full(RWKV_WIDTH, d), full(1, d), full(1, d)],
        out_specs=row(d),
        out_shape=jax.ShapeDtypeStruct((m, d), F32),
        compiler_params=_params(("parallel",)),
        name="out_proj_ln",
    )(att, rw, x, lw["w_out_att"], lw["w_out_rw"], lw["ln1_g"], lw["ln1_b"])


def _ffn_kernel(x_ref, wg_ref, wu_ref, wd_ref, cw_ref, cb_ref, g_ref, b_ref, *rest,
                alpha, tiles_per_seq, dec_seq):
    decode = dec_seq is not None
    if decode:
        h1_ref, h2_ref, y_ref, gp_ref, xb_ref, acc_ref = rest
    else:
        y_ref, gp_ref, xb_ref, acc_ref, tail_ref = rest
    i = pl.program_id(0)
    j = pl.program_id(1)
    tm = x_ref.shape[0]
    keep = SUBLANES

    @pl.when(j == 0)
    def _():
        xb_ref[...] = x_ref[...].astype(BF16)
        acc_ref[...] = jnp.zeros_like(acc_ref)

    xb = xb_ref[...]
    gp = _dot(xb, wg_ref[...])
    up = _dot(xb, wu_ref[...])
    row = lax.broadcasted_iota(jnp.int32, (tm, 1), 0)
    r1 = pltpu.roll(gp, 1, 0)
    r2 = pltpu.roll(gp, 2, 0)
    if decode:
        t = row % dec_seq
        g1 = jnp.where(t >= 1, r1, 0.0) + h1_ref[...]
        g2 = jnp.where(t >= 2, r2, 0.0) + h2_ref[...]
        gp_ref[...] = gp
    else:
        tail = tail_ref[j]
        seq_start = (i % tiles_per_seq) == 0
        t6 = jnp.where(seq_start, 0.0, tail[keep - 2:keep - 1, :])
        t7 = jnp.where(seq_start, 0.0, tail[keep - 1:keep, :])
        g1 = jnp.where(row == 0, t7, r1)
        g2 = jnp.where(row == 0, t6, jnp.where(row == 1, t7, r2))
        tail_ref[j] = gp[tm - keep:, :]
        gp_ref[0] = gp[tm - keep:, :]
    cw = cw_ref[...]
    conv = cw[0:1, :] * g2 + cw[1:2, :] * g1 + cw[2:3, :] * gp + cb_ref[...]
    h = conv * _sigmoid(conv) * up
    acc_ref[...] += _dot(h.astype(BF16), wd_ref[...])

    @pl.when(j == pl.num_programs(1) - 1)
    def _():
        y_ref[...] = _layer_norm(alpha * x_ref[...] + acc_ref[...], g_ref[...], b_ref[...])


def _ffn(x1, lw, *, tm, tf, alpha, seq_len=None, hist=None, dec_seq=None):
    m, d = x1.shape
    d_ff = lw["w_down"].shape[0]
    assert m % tm == 0 and d_ff % tf == 0
    nf = d_ff // tf
    decode = hist is not None
    in_specs = [pl.BlockSpec((tm, d), lambda i, j: (i, 0)),
                pl.BlockSpec((d, tf), lambda i, j: (0, j)),
                pl.BlockSpec((d, tf), lambda i, j: (0, nf + j)),
                pl.BlockSpec((tf, d), lambda i, j: (j, 0)),
                pl.BlockSpec((SUBLANES, tf), lambda i, j: (0, j)),
                pl.BlockSpec((1, tf), lambda i, j: (0, j)),
                pl.BlockSpec((1, d), lambda i, j: (0, 0)),
                pl.BlockSpec((1, d), lambda i, j: (0, 0))]
    args = [x1, lw["w_up"], lw["w_up"], lw["w_down"], lw["conv_w"], lw["conv_b"], lw["ln2_g"], lw["ln2_b"]]
    scratch = [pltpu.VMEM((tm, d), BF16), pltpu.VMEM((tm, d), F32)]
    if decode:
        assert m == tm
        in_specs += [pl.BlockSpec((tm, tf), lambda i, j: (0, j))] * 2
        args += list(hist)
        gp_shape = jax.ShapeDtypeStruct((m, d_ff), F32)
        gp_spec = pl.BlockSpec((tm, tf), lambda i, j: (0, j))
        tiles_per_seq = None
    else:
        assert seq_len % tm == 0 and tm % SUBLANES == 0
        tiles_per_seq = seq_len // tm
        gp_shape = jax.ShapeDtypeStruct((m // tm, SUBLANES, d_ff), F32)
        gp_spec = pl.BlockSpec((1, SUBLANES, tf), lambda i, j: (i, 0, j))
        scratch += [pltpu.VMEM((nf, SUBLANES, tf), F32)]
    return pl.pallas_call(
        functools.partial(_ffn_kernel, alpha=alpha, tiles_per_seq=tiles_per_seq, dec_seq=dec_seq),
        grid=(m // tm, nf),
        in_specs=in_specs,
        out_specs=[pl.BlockSpec((tm, d), lambda i, j: (i, 0)), gp_spec],
        out_shape=[jax.ShapeDtypeStruct((m, d), F32), gp_shape],
        scratch_shapes=scratch,
        compiler_params=_params(("arbitrary", "arbitrary")),
        name="conv_ffn_ln",
    )(*args)


def _prep_layer(w_in, w_out, shift_mu, w0, w2, a0, a2, g2, k_k, k_a, r_k, gn_g, gn_b,
                ln1_g, ln1_b, ln2_g, ln2_b, w_up, conv_w, conv_b, w_down):
    row = lambda v: v.reshape(1, -1)
    zpad = Z_PAD - RWKV_PROJ
    w_in_r = jnp.pad(w_in.astype(BF16), ((0, 0), (0, zpad)))
    mu = jnp.pad(shift_mu, (0, zpad))
    zero_row = jnp.zeros((RWKV_WIDTH,), F32)
    return {
        "w_in": w_in_r,
        "w_out_att": w_out[:ATT_WIDTH].astype(BF16),
        "w_out_rw": w_out[ATT_WIDTH:].astype(BF16),
        "mu_rkv": jnp.pad(mu[:LORA_COL].reshape(3, RWKV_WIDTH), ((0, SUBLANES - 3), (0, 0))),
        "mu_lora": row(mu[LORA_COL:LORA_COL + LORA_BLOCK]),
        "rwkv_par": jnp.stack([w0, a0, k_k, k_a, r_k, gn_g, gn_b, zero_row]),
        "w2": jnp.pad(w2, ((0, LANES - D_DECAY_LORA), (0, 0))).astype(BF16),
        "a2": jnp.pad(a2, ((D_DECAY_LORA, LANES - D_DECAY_LORA - D_AAA_LORA), (0, 0))).astype(BF16),
        "g2": jnp.pad(g2, ((0, GATE_LORA_PAD - D_GATE_LORA), (0, 0))).astype(BF16),
        "ln1_g": row(ln1_g), "ln1_b": row(ln1_b), "ln2_g": row(ln2_g), "ln2_b": row(ln2_b),
        "w_up": w_up.astype(BF16),
        "conv_w": jnp.pad(conv_w, ((0, SUBLANES - CONV_W), (0, 0))),
        "conv_b": row(conv_b),
        "w_down": w_down.astype(BF16),
    }


def _pick_tile(n, cap):
    t = min(n, cap)
    while n % t:
        t //= 2
    return t


def _prompt_layer(x, lw, alpha):
    b, s, d = x.shape
    d_ff = lw["w_down"].shape[0]
    pos = jnp.arange(s, dtype=jnp.int32)
    proj, k_new, v_new, kh, vt = _in_proj(x, lw["w_in"], pos, tm=_pick_tile(s, 1024), attn_extras=True)
    att = _moba_prompt(proj, kh, vt)
    rw, wkv = _rwkv(proj, jnp.zeros((b, 1, Z_PAD), F32),
                    jnp.zeros((b, N_RWKV_HEADS, HEAD_DIM, HEAD_DIM), F32), lw,
                    tblk=_pick_tile(s, 512), t_real=None)
    x1 = _out_ln(att.reshape(b * s, ATT_WIDTH), rw.reshape(b * s, RWKV_WIDTH), x.reshape(b * s, d), lw,
                 tm=_pick_tile(s, 512), alpha=alpha)
    tm = _pick_tile(s, 512)
    y, gp_tail = _ffn(x1, lw, tm=tm, tf=512, alpha=alpha, seq_len=s)
    k_new = k_new.reshape(b, s, N_KV_HEADS, HEAD_DIM)
    v_new = v_new.reshape(b, s, N_KV_HEADS, HEAD_DIM)
    shift_new = proj[:, s - 1, Z_COL:Z_COL + RWKV_PROJ]
    conv_new = gp_tail.reshape(b, s // tm, SUBLANES, d_ff)[:, -1, SUBLANES - (CONV_W - 1):, :]
    return y.reshape(b, s, d), k_new, v_new, wkv, shift_new, conv_new


def _decode_layer(x, cache_k, cache_v, page_table, shift0, wkv0, conv0, lw, alpha):
    bd, t, d = x.shape
    m = bd * t
    past = page_table.shape[1] * PAGE_SIZE
    pos = past + (jnp.arange(m, dtype=jnp.int32) % t)
    (proj,) = _in_proj(x.reshape(1, m, d), lw["w_in"], pos, tm=m, attn_extras=False)
    proj = proj.reshape(bd, t, PROJ_W)
    att = _moba_decode(proj, cache_k, cache_v, page_table, t)
    grp = _pick_tile(bd, RWKV_SEQS_PER_STEP)
    proj_pad = jnp.pad(proj, ((0, 0), (0, RWKV_CHUNK - t), (0, 0))).reshape(bd // grp, grp * RWKV_CHUNK, PROJ_W)
    shift_pad = jnp.pad(shift0, ((0, 0), (0, Z_PAD - RWKV_PROJ))).reshape(bd // grp, grp, Z_PAD)
    rw, wkv = _rwkv(proj_pad, shift_pad, wkv0, lw, tblk=grp * RWKV_CHUNK, t_real=t, independent=True)
    rw = rw.reshape(bd, RWKV_CHUNK, RWKV_WIDTH)[:, :t]
    x1 = _out_ln(att.reshape(m, ATT_WIDTH), rw.reshape(m, RWKV_WIDTH), x.reshape(m, d), lw, tm=m, alpha=alpha)
    d_ff = conv0.shape[-1]
    zeros = lambda n: jnp.zeros((bd, n, d_ff), F32)
    h1 = jnp.concatenate([conv0[:, 1:2], zeros(t - 1)], axis=1).reshape(m, d_ff)
    h2 = jnp.concatenate([conv0[:, 0:1], conv0[:, 1:2], zeros(t - 2)], axis=1).reshape(m, d_ff)
    y, gp = _ffn(x1, lw, tm=m, tf=512, alpha=alpha, hist=(h1, h2), dec_seq=t)
    k_new = proj[:, :, K_COL:V_COL].reshape(bd, t, N_KV_HEADS, HEAD_DIM)
    v_new = proj[:, :, V_COL:Z_COL].reshape(bd, t, N_KV_HEADS, HEAD_DIM)
    shift_new = proj[:, t - 1, Z_COL:Z_COL + RWKV_PROJ]
    conv_new = gp.reshape(bd, t, d_ff)[:, t - (CONV_W - 1):]
    return y.reshape(bd, t, d), k_new, v_new, wkv, shift_new, conv_new


def kernel(x_prompt, x_sample, cache_k, cache_v, page_table, state_wkv, state_shift, state_conv,
           w_in, w_out, shift_mu, w0, w2, a0, a2, g2, k_k, k_a, r_k, gn_g, gn_b,
           ln1_g, ln1_b, ln2_g, ln2_b, w_up, conv_w, conv_b, w_down):
    depth = w_in.shape[0]
    alpha = (2.0 * depth) ** 0.25
    assert x_sample.shape[1] >= CONV_W - 1
    yp, ys = x_prompt, x_sample
    outs_p, outs_s = [], []
    for l in range(depth):
        lw = _prep_layer(w_in[l], w_out[l], shift_mu[l], w0[l], w2[l], a0[l], a2[l], g2[l], k_k[l], k_a[l],
                         r_k[l], gn_g[l], gn_b[l], ln1_g[l], ln1_b[l], ln2_g[l], ln2_b[l],
                         w_up[l], conv_w[l], conv_b[l], w_down[l])
        yp, *rest_p = _prompt_layer(yp, lw, alpha)
        outs_p.append(rest_p)
        ys, *rest_s = _decode_layer(ys, cache_k[l], cache_v[l], page_table, state_shift[l], state_wkv[l],
                                    state_conv[l], lw, alpha)
        outs_s.append(rest_s)
    stack = lambda outs, n: jnp.stack([o[n] for o in outs])
    return (yp, ys,
            stack(outs_p, 0), stack(outs_p, 1), stack(outs_p, 2), stack(outs_p, 3), stack(outs_p, 4),
            stack(outs_s, 0), stack(outs_s, 1), stack(outs_s, 2), stack(outs_s, 3), stack(outs_s, 4))
```

```python
import functools

import jax
import jax.numpy as jnp
from jax import lax
from jax.experimental import pallas as pl
from jax.experimental.pallas import tpu as pltpu

F32 = jnp.float32
BF16 = jnp.bfloat16

HEAD_DIM = 64
N_ATT_HEADS = 16
N_KV_HEADS = 4
KV_GROUP = N_ATT_HEADS // N_KV_HEADS
ATT_WIDTH = N_ATT_HEADS * HEAD_DIM
KV_WIDTH = N_KV_HEADS * HEAD_DIM
N_RWKV_HEADS = 16
RWKV_WIDTH = N_RWKV_HEADS * HEAD_DIM
ROPE_DIM = HEAD_DIM // 4
ROPE_THETA = 500000.0
MOBA_BLOCK = 256
MOBA_TOPK = 3
D_DECAY_LORA = 64
D_AAA_LORA = 64
D_GATE_LORA = 160
LORA_WIDTH = D_DECAY_LORA + D_AAA_LORA + D_GATE_LORA
RWKV_PROJ = 3 * RWKV_WIDTH + LORA_WIDTH
CONV_W = 3
LN_EPS = 1e-5
GN_EPS = 64e-5
NEG_INF = -1e30
LOG2_E = 1.4426950408889634
PAGE_SIZE = 128

LANES = 128
SUBLANES = 8
VMEM_LIMIT_BYTES = 56 * 1024 * 1024

PROJ_TN = 512
Z_PAD = 3584
Q_COL = 0
K_COL = Q_COL + ATT_WIDTH
V_COL = K_COL + KV_WIDTH
Z_COL = V_COL + KV_WIDTH
PROJ_W = Z_COL + Z_PAD
LORA_COL = 3 * RWKV_WIDTH
LORA_BLOCK = 384
GATE_LORA_PAD = 256

RWKV_CHUNK = HEAD_DIM
RWKV_SEQS_PER_STEP = 8
RWKV_STAGE_GROUP = 8
PAGES_PER_STEP = 128

_NN = (((1,), (0,)), ((), ()))
_NT = (((1,), (1,)), ((), ()))


def _dot(a, b, dims=_NN):
    return lax.dot_general(a, b, dims, preferred_element_type=F32)


def _split_bf16(x):
    hi = x.astype(BF16)
    lo = (x - hi.astype(F32)).astype(BF16)
    return hi, lo


def _dot3(a, b, dims=_NN):
    ah, al = _split_bf16(a)
    bh, bl = _split_bf16(b)
    return _dot(ah, bh, dims) + (_dot(ah, bl, dims) + _dot(al, bh, dims))


def _sigmoid(x):
    return 1.0 / (1.0 + jnp.exp(-x))


def _layer_norm(x, g, b):
    mu = jnp.mean(x, axis=-1, keepdims=True)
    d = x - mu
    var = jnp.mean(d * d, axis=-1, keepdims=True)
    return d * lax.rsqrt(var + LN_EPS) * g + b


def _params(sem):
    return pltpu.CompilerParams(dimension_semantics=sem, vmem_limit_bytes=VMEM_LIMIT_BYTES)


def _rope_tables(pos):
    half = ROPE_DIM // 2
    inv = jnp.power(ROPE_THETA, -(jnp.arange(half, dtype=F32) * 2.0 / ROPE_DIM))
    ang = pos.astype(F32)[:, None] * inv[None, :]
    cos = jnp.cos(ang)
    sin = jnp.sin(ang)
    t = pos.shape[0]
    one = jnp.ones((t, HEAD_DIM - ROPE_DIM), F32)
    z8 = jnp.zeros((t, half), F32)
    zr = jnp.zeros((t, HEAD_DIM - ROPE_DIM), F32)
    c64 = jnp.concatenate([cos, cos, one], axis=1)
    a64 = jnp.concatenate([-sin, z8, zr], axis=1)
    b64 = jnp.concatenate([z8, sin, zr], axis=1)
    rep = LANES // HEAD_DIM
    return tuple(jnp.tile(m, (1, rep)) for m in (c64, a64, b64))


def _in_proj_kernel(x_ref, w_ref, cos_ref, sa_ref, sb_ref, proj_ref, *rest, attn_extras):
    if attn_extras:
        kh_ref, vt_ref, xb_ref = rest
    else:
        (xb_ref,) = rest
    j = pl.program_id(2)
    n_q = ATT_WIDTH // PROJ_TN
    half = ROPE_DIM // 2

    @pl.when(j == 0)
    def _():
        xb_ref[...] = x_ref[0].astype(BF16)

    acc = _dot(xb_ref[...], w_ref[...])

    def rope(slab):
        return (slab * cos_ref[...] + pltpu.roll(slab, LANES - half, 1) * sa_ref[...]
                + pltpu.roll(slab, half, 1) * sb_ref[...])

    @pl.when(j > n_q)
    def _():
        proj_ref[0] = acc

    @pl.when(j < n_q)
    def _():
        for s in range(PROJ_TN // LANES):
            proj_ref[0, :, s * LANES:(s + 1) * LANES] = rope(acc[:, s * LANES:(s + 1) * LANES])

    @pl.when(j == n_q)
    def _():
        k_slabs = []
        for s in range(KV_WIDTH // LANES):
            ks = rope(acc[:, s * LANES:(s + 1) * LANES])
            proj_ref[0, :, s * LANES:(s + 1) * LANES] = ks
            k_slabs.append(ks)
        v = acc[:, KV_WIDTH:]
        proj_ref[0, :, KV_WIDTH:] = v
        if attn_extras:
            k = jnp.concatenate(k_slabs, axis=1)
            tm = acc.shape[0]
            row_blk = (pl.program_id(1) * tm + lax.broadcasted_iota(jnp.int32, (tm, HEAD_DIM), 0)) // MOBA_BLOCK
            one_hot = (row_blk == lax.broadcasted_iota(jnp.int32, (tm, HEAD_DIM), 1)).astype(BF16)
            for c in range(N_KV_HEADS):
                kh_ref[0, c] = jnp.concatenate(
                    [k[:, c * HEAD_DIM:(c + 1) * HEAD_DIM].astype(BF16), one_hot], axis=1)
            vt_ref[0] = v.T.astype(BF16)


def _in_proj(x, w_b, pos, *, tm, attn_extras):
    b, s, d = x.shape
    assert s % tm == 0 and PROJ_W % PROJ_TN == 0
    cos, sa, sb = _rope_tables(pos)
    tab_spec = pl.BlockSpec((tm, LANES), lambda bi, i, j: (i, 0))
    out_shape = [jax.ShapeDtypeStruct((b, s, PROJ_W), F32)]
    out_specs = [pl.BlockSpec((1, tm, PROJ_TN), lambda bi, i, j: (bi, i, j))]
    if attn_extras:
        assert s // MOBA_BLOCK <= HEAD_DIM
        out_shape += [jax.ShapeDtypeStruct((b, N_KV_HEADS, s, 2 * HEAD_DIM), BF16),
                      jax.ShapeDtypeStruct((b, KV_WIDTH, s), BF16)]
        out_specs += [pl.BlockSpec((1, N_KV_HEADS, tm, 2 * HEAD_DIM), lambda bi, i, j: (bi, 0, i, 0)),
                      pl.BlockSpec((1, KV_WIDTH, tm), lambda bi, i, j: (bi, 0, i))]
    return pl.pallas_call(
        functools.partial(_in_proj_kernel, attn_extras=attn_extras),
        grid=(b, s // tm, PROJ_W // PROJ_TN),
        in_specs=[pl.BlockSpec((1, tm, d), lambda bi, i, j: (bi, i, 0)),
                  pl.BlockSpec((d, PROJ_TN), lambda bi, i, j: (0, j)),
                  tab_spec, tab_spec, tab_spec],
        out_specs=out_specs,
        out_shape=out_shape,
        scratch_shapes=[pltpu.VMEM((tm, d), BF16)],
        compiler_params=_params(("parallel", "parallel", "arbitrary")),
        name="in_proj",
    )(x, w_b, cos, sa, sb)


def _moba_prompt_kernel(q_ref, kh_ref, vt_ref, o_ref, kmean_ref, *, n_blk, topk):
    i = pl.program_id(2)
    blk = MOBA_BLOCK
    n_cand = n_blk - 1
    scale = HEAD_DIM ** -0.5

    @pl.when(i == 0)
    def _():
        kmean_ref[...] = jnp.zeros_like(kmean_ref)
        for n in range(n_blk):
            kmean_ref[n:n + 1, :] = jnp.mean(
                kh_ref[0, 0, n * blk:(n + 1) * blk, :].astype(F32), axis=0, keepdims=True)

    q_t = q_ref[0].T
    qs_t = jnp.concatenate([q_t[g * HEAD_DIM:(g + 1) * HEAD_DIM, :] for g in range(KV_GROUP)],
                           axis=1)
    nq = KV_GROUP * blk

    gate = _dot3(kmean_ref[...], jnp.concatenate([qs_t, jnp.zeros((HEAD_DIM, nq), F32)], axis=0))
    n_idx = lax.broadcasted_iota(jnp.int32, gate.shape, 0)
    gate = jnp.where(n_idx < i, gate, NEG_INF)
    rank = jnp.zeros(gate.shape, F32)
    for m in range(n_cand):
        gm = gate[m:m + 1, :]
        beats = (gm > gate) | ((gm == gate) & (m < n_idx))
        rank = rank + beats.astype(F32)
    blk_bias = jnp.where(((rank < topk) & (n_idx < i)) | (n_idx == i), 0.0, NEG_INF)
    q_aug = jnp.concatenate(
        [qs_t * (scale * LOG2_E), blk_bias, jnp.zeros((HEAD_DIM - blk_bias.shape[0], nq), F32)],
        axis=0).astype(BF16)

    kpos = lax.broadcasted_iota(jnp.int32, (blk, nq), 0)
    qpos = lax.broadcasted_iota(jnp.int32, (blk, nq), 1) & (blk - 1)
    causal_bias = jnp.where(kpos <= qpos, 0.0, NEG_INF)

    def attend(nb):
        s = _dot(kh_ref[0, 0, 0:nb * blk, :], q_aug)
        own = s[(nb - 1) * blk:, :] + causal_bias
        s = jnp.concatenate([s[:(nb - 1) * blk, :], own], axis=0) if nb > 1 else own
        p = jnp.exp2(s - jnp.max(s, axis=0, keepdims=True))
        l = jnp.sum(p, axis=0, keepdims=True)
        o = _dot(vt_ref[0, :, 0:nb * blk], p.astype(BF16)) / l
        o4 = jnp.concatenate([o[:, g * blk:(g + 1) * blk] for g in range(KV_GROUP)], axis=0)
        o_ref[0] = o4.T.astype(o_ref.dtype)

    for nb in range(1, n_blk + 1):
        pl.when(i == nb - 1)(functools.partial(attend, nb))


def _moba_prompt(proj, kh, vt):
    b, s, _ = proj.shape
    assert s % MOBA_BLOCK == 0
    n_blk = s // MOBA_BLOCK
    topk = min(MOBA_TOPK, n_blk - 1)
    gw = KV_GROUP * HEAD_DIM
    n_rows = -(-n_blk // SUBLANES) * SUBLANES
    assert n_rows <= HEAD_DIM
    return pl.pallas_call(
        functools.partial(_moba_prompt_kernel, n_blk=n_blk, topk=topk),
        grid=(b, N_KV_HEADS, n_blk),
        in_specs=[pl.BlockSpec((1, MOBA_BLOCK, gw), lambda bi, c, i: (bi, i, Q_COL // gw + c)),
                  pl.BlockSpec((1, 1, s, 2 * HEAD_DIM), lambda bi, c, i: (bi, c, 0, 0)),
                  pl.BlockSpec((1, HEAD_DIM, s), lambda bi, c, i: (bi, c, 0))],
        out_specs=pl.BlockSpec((1, MOBA_BLOCK, gw), lambda bi, c, i: (bi, i, c)),
        out_shape=jax.ShapeDtypeStruct((b, s, ATT_WIDTH), BF16),
        scratch_shapes=[pltpu.VMEM((n_rows, 2 * HEAD_DIM), F32)],
        compiler_params=_params(("parallel", "parallel", "arbitrary")),
        name="moba_prompt",
    )(proj, kh, vt)


def _moba_decode_scores_kernel(pt_ref, wt_ref, *rest, n_new):
    del pt_ref
    r_pages = PAGES_PER_STEP
    k_refs = rest[:r_pages]
    s_ref, gate_ref, bmax_ref = rest[r_pages:]
    j = pl.program_id(1)
    scale = HEAD_DIM ** -0.5
    nq = N_ATT_HEADS * n_new
    pages_per_blk = MOBA_BLOCK // PAGE_SIZE
    blks_per_step = r_pages // pages_per_blk
    lane = lax.broadcasted_iota(jnp.int32, gate_ref.shape[1:], 1)

    @pl.when(j == 0)
    def _():
        gate_ref[0] = jnp.zeros(gate_ref.shape[1:], F32)
        bmax_ref[0] = jnp.full(bmax_ref.shape[1:], NEG_INF, F32)

    wt2 = jnp.concatenate(_split_bf16(wt_ref[0]), axis=0)
    gate = gate_ref[0]
    bmax = bmax_ref[0]
    for r in range(r_pages):
        s2 = _dot(wt2, k_refs[r][0].astype(BF16))
        raw = s2[:nq] + s2[nq:]
        s_ref[0, :, r * PAGE_SIZE:(r + 1) * PAGE_SIZE] = raw * scale
        first = r % pages_per_blk == 0
        blk_sum = raw if first else blk_sum + raw
        blk_max = raw if first else jnp.maximum(blk_max, raw)
        if r % pages_per_blk == pages_per_blk - 1:
            here = lane == j * blks_per_step + r // pages_per_blk
            gate = jnp.where(here, jnp.sum(blk_sum, axis=1, keepdims=True), gate)
            bmax = jnp.where(here, jnp.max(blk_max, axis=1, keepdims=True) * scale, bmax)
    gate_ref[0] = gate
    bmax_ref[0] = bmax


def _moba_decode_values_kernel(pt_ref, wt_ref, kvn_ref, gate_ref, bmax_ref, s_ref, *rest,
                               n_cand, n_new):
    del pt_ref
    r_pages = PAGES_PER_STEP
    v_refs = rest[:r_pages]
    o_ref = rest[r_pages]
    sel_ref, acc_ref, l_ref, m_ref = rest[r_pages + 1:]
    n_cand_pad = gate_ref.shape[2]
    j = pl.program_id(1)
    scale = HEAD_DIM ** -0.5
    pages_per_blk = MOBA_BLOCK // PAGE_SIZE
    blks_per_step = r_pages // pages_per_blk

    @pl.when(j == 0)
    def _():
        lane = lax.broadcasted_iota(jnp.int32, gate_ref.shape[1:], 1)
        lane_f = lane.astype(F32)
        gate = jnp.where(lane < n_cand, gate_ref[0] * (1.0 / MOBA_BLOCK), NEG_INF)
        sel = lane < 0
        for _ in range(min(MOBA_TOPK, n_cand)):
            top = jnp.max(gate, axis=1, keepdims=True)
            first_top = jnp.min(jnp.where(gate == top, lane_f, float(n_cand_pad)), axis=1, keepdims=True)
            pick = lane_f == first_top
            sel = sel | pick
            gate = jnp.where(pick, -jnp.inf, gate)
        sel = sel & (lane < n_cand)
        sel_ref[...] = sel.astype(F32)

        kvn = kvn_ref[0]
        k_new = kvn[:, :KV_WIDTH].astype(BF16)
        v_new = kvn[:, KV_WIDTH:].astype(BF16)
        s_new = _dot((wt_ref[0] * scale).astype(BF16), k_new, _NT)
        t_q = lax.broadcasted_iota(jnp.int32, s_new.shape, 0) % n_new
        t_k = lax.broadcasted_iota(jnp.int32, s_new.shape, 1)
        ok_new = (t_k <= t_q) & (t_k < n_new)
        m = jnp.maximum(jnp.max(jnp.where(ok_new, s_new, NEG_INF), axis=1, keepdims=True),
                        jnp.max(jnp.where(sel, bmax_ref[0], NEG_INF), axis=1, keepdims=True))
        p_new = jnp.where(ok_new, jnp.exp(s_new - m), 0.0)
        m_ref[...] = jnp.broadcast_to(m, m_ref.shape)
        l_lane = lax.broadcasted_iota(jnp.int32, l_ref.shape, 1)
        l_ref[...] = jnp.where(l_lane == 0, jnp.sum(p_new, axis=1, keepdims=True), 0.0)
        acc_ref[...] = _dot(p_new.astype(BF16), v_new)

    src = lax.broadcasted_iota(jnp.int32, (n_cand_pad, LANES), 0)
    dst = lax.broadcasted_iota(jnp.int32, (n_cand_pad, LANES), 1)
    onehot = ((src == j * blks_per_step + dst) & (dst < blks_per_step)).astype(BF16)
    sel_here = _dot(sel_ref[...].astype(BF16), onehot)
    m = m_ref[:, 0:1]
    acc = acc_ref[...]
    l_part = l_ref[...]
    for r in range(r_pages):
        blk = r // pages_per_blk
        sb = s_ref[0, :, r * PAGE_SIZE:(r + 1) * PAGE_SIZE]
        pb = jnp.where(sel_here[:, blk:blk + 1] > 0.5, jnp.exp(sb - m), 0.0)
        l_part = l_part + pb
        acc = acc + _dot(pb.astype(BF16), v_refs[r][0].astype(BF16), _NT)
    acc_ref[...] = acc
    l_ref[...] = l_part

    @pl.when(j == pl.num_programs(1) - 1)
    def _():
        o_ref[0] = acc / jnp.sum(l_part, axis=1, keepdims=True)


def _moba_decode(proj_d, cache_k, cache_v, page_table, n_new):
    bd = proj_d.shape[0]
    n_pages = page_table.shape[1]
    past = n_pages * PAGE_SIZE
    assert past % MOBA_BLOCK == 0 and n_pages % PAGES_PER_STEP == 0 and n_new <= SUBLANES
    n_cand = past // MOBA_BLOCK
    n_steps = n_pages // PAGES_PER_STEP
    n_phys = cache_k.shape[0]
    nq = N_ATT_HEADS * n_new
    ck = jnp.transpose(cache_k, (0, 2, 3, 1)).reshape(n_phys, KV_WIDTH, PAGE_SIZE)
    cv = jnp.transpose(cache_v, (0, 2, 3, 1)).reshape(n_phys, KV_WIDTH, PAGE_SIZE)

    q = proj_d[:, :, Q_COL:Q_COL + ATT_WIDTH].reshape(bd, n_new, N_KV_HEADS, KV_GROUP, HEAD_DIM)
    q = jnp.transpose(q, (0, 2, 3, 1, 4)).reshape(bd, N_KV_HEADS, KV_GROUP * n_new, 1, HEAD_DIM)
    eye = jnp.eye(N_KV_HEADS, dtype=F32)[None, :, None, :, None]
    wt = (q * eye).reshape(bd, nq, KV_WIDTH)
    kvn = jnp.pad(proj_d[:, :, K_COL:K_COL + 2 * KV_WIDTH], ((0, 0), (0, SUBLANES - n_new), (0, 0)))

    def page_map(r):
        return lambda b, j, pt: (pt[b, j * PAGES_PER_STEP + r], 0, 0)

    step_keys = PAGES_PER_STEP * PAGE_SIZE
    n_cand_pad = -(-n_cand // LANES) * LANES
    per_seq = lambda w: pl.BlockSpec((1, nq, w), lambda b, j, pt: (b, 0, 0))
    page_specs = [pl.BlockSpec((1, KV_WIDTH, PAGE_SIZE), page_map(r)) for r in range(PAGES_PER_STEP)]
    score_spec = pl.BlockSpec((1, nq, step_keys), lambda b, j, pt: (b, 0, j))
    scores, gate, bmax = pl.pallas_call(
        functools.partial(_moba_decode_scores_kernel, n_new=n_new),
        grid_spec=pltpu.PrefetchScalarGridSpec(
            num_scalar_prefetch=1,
            grid=(bd, n_steps),
            in_specs=[per_seq(KV_WIDTH)] + page_specs,
            out_specs=[score_spec, per_seq(n_cand_pad), per_seq(n_cand_pad)]),
        out_shape=[jax.ShapeDtypeStruct((bd, nq, past), F32),
                   jax.ShapeDtypeStruct((bd, nq, n_cand_pad), F32),
                   jax.ShapeDtypeStruct((bd, nq, n_cand_pad), F32)],
        compiler_params=_params(("parallel", "arbitrary")),
        name="moba_decode_scores",
    )(page_table, wt, *([ck] * PAGES_PER_STEP))
    out = pl.pallas_call(
        functools.partial(_moba_decode_values_kernel, n_cand=n_cand, n_new=n_new),
        grid_spec=pltpu.PrefetchScalarGridSpec(
            num_scalar_prefetch=1,
            grid=(bd, n_steps),
            in_specs=[per_seq(KV_WIDTH),
                      pl.BlockSpec((1, SUBLANES, 2 * KV_WIDTH), lambda b, j, pt: (b, 0, 0)),
                      per_seq(n_cand_pad), per_seq(n_cand_pad), score_spec] + page_specs,
            out_specs=per_seq(KV_WIDTH),
            scratch_shapes=[pltpu.VMEM((nq, n_cand_pad), F32),
                            pltpu.VMEM((nq, KV_WIDTH), F32),
                            pltpu.VMEM((nq, LANES), F32),
                            pltpu.VMEM((nq, LANES), F32)]),
        out_shape=jax.ShapeDtypeStruct((bd, nq, KV_WIDTH), F32),
        compiler_params=_params(("parallel", "arbitrary")),
        name="moba_decode_values",
    )(page_table, wt, kvn, gate, bmax, scores, *([cv] * PAGES_PER_STEP))
    o = out.reshape(bd, N_KV_HEADS, KV_GROUP, n_new, N_KV_HEADS, HEAD_DIM)
    o = jnp.stack([o[:, c, :, :, c, :] for c in range(N_KV_HEADS)], axis=1)
    o = jnp.transpose(o, (0, 3, 1, 2, 4)).reshape(bd, n_new, ATT_WIDTH)
    return o.astype(BF16)


def _rwkv_kernel(zr_ref, zk_ref, zv_ref, zl_ref, sr_ref, sk_ref, sv_ref, sl_ref,
                 mu_ref, mul_ref, par_ref, w2_ref, a2_ref, g2_ref, s0_ref,
                 rw_ref, wkv_ref,
                 state_ref, pr_ref, pk_ref, pv_ref, plo_ref, *, t_real, independent):
    tb = pl.program_id(2)
    n_tb = pl.num_programs(2)
    tblk = zr_ref.shape[1]
    c_len = RWKV_CHUNK
    width = LANES
    lane = lax.broadcasted_iota(jnp.int32, (1, width), 1)
    head_a = lane < HEAD_DIM
    row = lax.broadcasted_iota(jnp.int32, (tblk, 1), 0)

    def block_diag(sa, sb):
        z = jnp.zeros((HEAD_DIM, HEAD_DIM), F32)
        return jnp.concatenate([jnp.concatenate([sa, z], axis=1),
                                jnp.concatenate([z, sb], axis=1)], axis=0)

    if independent:
        def token_shift(z, first_ref, prev_ref, mu):
            z_prev = pltpu.roll(z, 1, 0)
            first = first_ref[0]
            for c in range(tblk // c_len):
                z_prev = jnp.where(row == c * c_len, first[c:c + 1, :], z_prev)
            return z + (z_prev - z) * mu
    else:
        @pl.when(tb == 0)
        def _():
            state_ref[...] = block_diag(s0_ref[0, 0], s0_ref[0, 1])
            pr_ref[0:1, :] = sr_ref[0]
            pk_ref[0:1, :] = sk_ref[0]
            pv_ref[0:1, :] = sv_ref[0]
            plo_ref[0:1, :] = sl_ref[0]

        def token_shift(z, first_ref, prev_ref, mu):
            z_prev = jnp.where(row == 0, prev_ref[0:1, :], pltpu.roll(z, 1, 0))
            prev_ref[0:1, :] = z[tblk - 1:tblk, :]
            return z + (z_prev - z) * mu

    def seg_sum(x):
        sa = jnp.sum(jnp.where(head_a, x, 0.0), axis=1, keepdims=True)
        sb = jnp.sum(jnp.where(head_a, 0.0, x), axis=1, keepdims=True)
        return jnp.where(head_a, sa, sb)

    r = token_shift(zr_ref[0], sr_ref, pr_ref, mu_ref[0:1, :])
    k = token_shift(zk_ref[0], sk_ref, pk_ref, mu_ref[1:2, :])
    v = token_shift(zv_ref[0], sv_ref, pv_ref, mu_ref[2:3, :])
    lo = token_shift(zl_ref[0], sl_ref, plo_ref, mul_ref[...])
    par = par_ref[...]
    w0, a0, k_k, k_a, r_k, gn_g, gn_b = (par[n:n + 1, :] for n in range(7))

    lo_wa = lo[:, :LANES]
    nx = -(w0 + _dot(jnp.tanh(lo_wa).astype(BF16), w2_ref[...]))
    softplus = jnp.maximum(nx, 0.0) + jnp.log(1.0 + jnp.exp(-jnp.abs(nx)))
    logw = -jnp.exp(-softplus - 0.5)
    a = _sigmoid(a0 + _dot(lo_wa.astype(BF16), a2_ref[...]))
    g = _dot(_sigmoid(lo[:, LANES:]).astype(BF16), g2_ref[...])
    kk = k * k_k
    kk = kk / jnp.maximum(jnp.sqrt(seg_sum(kk * kk)), 1e-12)
    k = k * (1.0 + (a - 1.0) * k_a)
    bonus = seg_sum(r * k * r_k) * v

    if t_real is not None:
        valid = ((row & (c_len - 1)) if independent else (tb * tblk + row)) < t_real
        logw = jnp.where(valid, logw, 0.0)
        kk = jnp.where(valid, kk, 0.0)
        k = jnp.where(valid, k, 0.0)
        v_in = jnp.where(valid, v, 0.0)
    else:
        v_in = v

    def pair_rows(x):
        return jnp.concatenate([jnp.where(head_a, x, 0.0), jnp.where(head_a, 0.0, x)], axis=0)

    ri = lax.broadcasted_iota(jnp.int32, (c_len, 2 * c_len), 0)
    ci = lax.broadcasted_iota(jnp.int32, (c_len, 2 * c_len), 1)
    first = ci < c_len
    tcol = ci & (c_len - 1)
    strict = tcol < ri
    incl = tcol <= ri
    eye2 = (tcol == ri).astype(F32)
    bi = lax.broadcasted_iota(jnp.int32, (width, width), 0) < HEAD_DIM
    bj = lax.broadcasted_iota(jnp.int32, (width, width), 1) < HEAD_DIM
    same_head = bi == bj
    live = c_len if t_real is None else min(t_real, c_len)
    n_levels = max((live - 1).bit_length() - 1, 0)
    shift = c_len.bit_length() - 1
    chunks = [slice(c * c_len, (c + 1) * c_len) for c in range(tblk // c_len)]
    bf = lambda x: x.astype(BF16)

    tr = lax.broadcasted_iota(jnp.int32, (tblk, tblk), 0)
    tc = lax.broadcasted_iota(jnp.int32, (tblk, tblk), 1)
    tri = ((tr >= tc) & ((tr >> shift) == (tc >> shift))).astype(BF16)
    lh = logw.astype(BF16)
    l1 = logw - lh.astype(F32)
    lm = l1.astype(BF16)
    ll = (l1 - lm.astype(F32)).astype(BF16)
    lc = _dot(tri, lh) + (_dot(tri, lm) + _dot(tri, ll))
    g_in = jnp.exp(lc)
    g_inv = jnp.exp(-lc)
    a_all = -kk * jnp.exp(lc - logw)
    b_all = kk * a * g_inv
    k_all = k * g_inv
    r_all = r * g_in

    def group_terms(sls):
        a_t = [a_all[sl] for sl in sls]
        r_t = [r_all[sl] for sl in sls]
        b_t = [b_all[sl] for sl in sls]
        k_t = [k_all[sl] for sl in sls]
        v_c = [v_in[sl] for sl in sls]
        g_end = [g_in[sl.stop - 1:sl.stop, :] for sl in sls]
        a_pair = [pair_rows(bf(x)) for x in a_t]
        s4 = [_dot(jnp.concatenate([ap, pair_rows(bf(rt))], axis=0),
                   bf(jnp.concatenate([bt, kt], axis=0)), _NT)
              for ap, rt, bt, kt in zip(a_pair, r_t, b_t, k_t)]
        m_ab, m_ak, m_rb, m_rk = [], [], [], []
        for x in s4:
            sa_, sb_ = x[0:c_len], x[c_len:2 * c_len]
            ra_, rb_ = x[2 * c_len:3 * c_len], x[3 * c_len:4 * c_len]
            m_ab.append(jnp.where(strict, jnp.where(first, sa_, pltpu.roll(sb_, c_len, 1)), 0.0))
            m_ak.append(jnp.where(strict, jnp.where(first, pltpu.roll(sa_, c_len, 1), sb_), 0.0))
            m_rb.append(jnp.where(incl, jnp.where(first, ra_, pltpu.roll(rb_, c_len, 1)), 0.0))
            m_rk.append(jnp.where(incl, jnp.where(first, pltpu.roll(ra_, c_len, 1), rb_), 0.0))

        t_inv = [m + eye2 for m in m_ab]
        if n_levels > 0:
            l_pow = [bf(m) for m in m_ab]
            l_pow = [bf(_dot(lp, pair_rows(lp))) for lp in l_pow]
            for level in range(n_levels):
                qp = [_dot(lp, jnp.concatenate([pair_rows(lp), pair_rows(bf(ti))], axis=1))
                      for lp, ti in zip(l_pow, t_inv)]
                t_inv = [ti + x[:, width:] for ti, x in zip(t_inv, qp)]
                if level + 1 < n_levels:
                    l_pow = [bf(x[:, :width]) for x in qp]

        v_pair = [pair_rows(bf(x)) for x in v_c]
        mak_v = [_dot(bf(m), vp) for m, vp in zip(m_ak, v_pair)]
        tw = [_dot(bf(ti), jnp.concatenate([ap, pair_rows(bf(mv))], axis=1))
              for ti, ap, mv in zip(t_inv, a_pair, mak_v)]
        w_t = [x[:, :width] for x in tw]
        v_hat = [x[:, width:] for x in tw]
        mw = [_dot(bf(jnp.concatenate([mb, mk], axis=1)),
                   jnp.concatenate(
                       [jnp.concatenate([pair_rows(bf(w)), pair_rows(bf(vh))], axis=1),
                        jnp.concatenate([jnp.zeros_like(vp), vp], axis=1)], axis=0))
              for mb, mk, w, vh, vp in zip(m_rb, m_rk, w_t, v_hat, v_pair)]
        p_m = [rt + x[:, :width] for rt, x in zip(r_t, mw)]
        q_m = [x[:, width:] for x in mw]
        phi, psi = [], []
        for w, vh, vc, bt, kt, ge in zip(w_t, v_hat, v_c, b_t, k_t, g_end):
            bk = bf(jnp.concatenate([bt * ge, kt * ge], axis=0))
            left = jnp.concatenate(
                [jnp.concatenate([w, jnp.zeros_like(w)], axis=0),
                 jnp.concatenate([vh, vc], axis=0)], axis=1)
            pp = _dot(bf(left.T), bk)
            phi.append(bf(jnp.where(same_head, pp[:width], 0.0)))
            psi.append(jnp.where(same_head, pp[width:], 0.0))
        return list(zip(p_m, q_m, phi, psi, g_end))

    def advance(s, term):
        pm, qm, ph, ps, ge = term
        sb16 = bf(s)
        return _dot(bf(pm), sb16, _NT) + qm, s * ge + _dot(sb16, ph) + ps

    outs = []
    s = None if independent else state_ref[...]
    for g0 in range(0, len(chunks), RWKV_STAGE_GROUP):
        for c, term in enumerate(group_terms(chunks[g0:g0 + RWKV_STAGE_GROUP]), start=g0):
            if independent:
                o_c, s_c = advance(block_diag(s0_ref[c, 0], s0_ref[c, 1]), term)
                wkv_ref[c, 0] = s_c[:HEAD_DIM, :HEAD_DIM]
                wkv_ref[c, 1] = s_c[HEAD_DIM:, HEAD_DIM:]
            else:
                o_c, s = advance(s, term)
            outs.append(o_c)
    if not independent:
        state_ref[...] = s

        @pl.when(tb == n_tb - 1)
        def _():
            wkv_ref[0, 0] = s[:HEAD_DIM, :HEAD_DIM]
            wkv_ref[0, 1] = s[HEAD_DIM:, HEAD_DIM:]

    o = jnp.concatenate(outs, axis=0) if len(outs) > 1 else outs[0]
    inv_n = 1.0 / HEAD_DIM
    mu_o = seg_sum(o) * inv_n
    d = o - mu_o
    var = seg_sum(d * d) * inv_n
    on = d * lax.rsqrt(var + GN_EPS) * gn_g + gn_b
    rw_ref[0] = ((on + bonus) * g).astype(rw_ref.dtype)


def _rwkv(proj, shift0, wkv0, lw, *, tblk, t_real, independent=False):
    b, s, _ = proj.shape
    assert s % tblk == 0 and tblk % RWKV_CHUNK == 0
    n_pairs = RWKV_WIDTH // LANES
    rkv = RWKV_WIDTH // LANES
    lora_blk = LORA_COL // LORA_BLOCK
    n_first = shift0.shape[1]
    n_state = tblk // RWKV_CHUNK if independent else 1
    assert (not independent) or (s == tblk and n_first == n_state)

    def z_spec(width, col0):
        return pl.BlockSpec((1, tblk, width), lambda bi, p, t: (bi, t, col0(p)))

    def s_spec(width, col0):
        return pl.BlockSpec((1, n_first, width), lambda bi, p, t: (bi, 0, col0(p)))

    state_spec = pl.BlockSpec((n_state, 2, HEAD_DIM, HEAD_DIM), lambda bi, p, t: (bi, p, 0, 0))

    assert Z_COL % LANES == 0 and (Z_COL + LORA_COL) % LORA_BLOCK == 0
    cols = [lambda p: p, lambda p: rkv + p, lambda p: 2 * rkv + p]
    zc = Z_COL // LANES
    in_specs = ([z_spec(LANES, lambda p, c=c: zc + c(p)) for c in cols]
                + [z_spec(LORA_BLOCK, lambda p: (Z_COL + LORA_COL) // LORA_BLOCK)]
                + [s_spec(LANES, c) for c in cols] + [s_spec(LORA_BLOCK, lambda p: lora_blk)]
                + [pl.BlockSpec((SUBLANES, LANES), lambda bi, p, t: (0, p)),
                   pl.BlockSpec((1, LORA_BLOCK), lambda bi, p, t: (0, 0)),
                   pl.BlockSpec((SUBLANES, LANES), lambda bi, p, t: (0, p)),
                   pl.BlockSpec((LANES, LANES), lambda bi, p, t: (0, p)),
                   pl.BlockSpec((LANES, LANES), lambda bi, p, t: (0, p)),
                   pl.BlockSpec((GATE_LORA_PAD, LANES), lambda bi, p, t: (0, p)),
                   state_spec])
    return pl.pallas_call(
        functools.partial(_rwkv_kernel, t_real=t_real, independent=independent),
        grid=(b, n_pairs, s // tblk),
        in_specs=in_specs,
        out_specs=[pl.BlockSpec((1, tblk, LANES), lambda bi, p, t: (bi, t, p)), state_spec],
        out_shape=[jax.ShapeDtypeStruct((b, s, RWKV_WIDTH), BF16),
                   jax.ShapeDtypeStruct((b * n_state, N_RWKV_HEADS, HEAD_DIM, HEAD_DIM), F32)],
        scratch_shapes=[pltpu.VMEM((LANES, LANES), F32),
                        pltpu.VMEM((SUBLANES, LANES), F32),
                        pltpu.VMEM((SUBLANES, LANES), F32),
                        pltpu.VMEM((SUBLANES, LANES), F32),
                        pltpu.VMEM((SUBLANES, LORA_BLOCK), F32)],
        compiler_params=_params(("parallel", "parallel", "arbitrary")),
        name="rwkv7",
    )(proj, proj, proj, proj, shift0, shift0, shift0, shift0,
      lw["mu_rkv"], lw["mu_lora"], lw["rwkv_par"], lw["w2"], lw["a2"], lw["g2"], wkv0)


def _out_ln_kernel(att_ref, rw_ref, x_ref, wa_ref, wr_ref, g_ref, b_ref, o_ref, *, alpha):
    mix = _dot(att_ref[...], wa_ref[...]) + _dot(rw_ref[...], wr_ref[...])
    o_ref[...] = _layer_norm(alpha * x_ref[...] + mix, g_ref[...], b_ref[...])


def _out_ln(att, rw, x, lw, *, tm, alpha):
    m, d = x.shape
    assert m % tm == 0
    row = lambda w: pl.BlockSpec((tm, w), lambda i: (i, 0))
    full = lambda r, w: pl.BlockSpec((r, w), lambda i: (0, 0))
    return pl.pallas_call(
        functools.partial(_out_ln_kernel, alpha=alpha),
        grid=(m // tm,),
        in_specs=[row(ATT_WIDTH), row(RWKV_WIDTH), row(d),
                  full(ATT_WIDTH, d), full(RWKV_WIDTH, d), full(1, d), full(1, d)],
        out_specs=row(d),
        out_shape=jax.ShapeDtypeStruct((m, d), F32),
        compiler_params=_params(("parallel",)),
        name="out_proj_ln",
    )(att, rw, x, lw["w_out_att"], lw["w_out_rw"], lw["ln1_g"], lw["ln1_b"])


def _ffn_kernel(x_ref, wg_ref, wu_ref, wd_ref, cw_ref, cb_ref, g_ref, b_ref, *rest,
                alpha, tiles_per_seq, dec_seq):
    decode = dec_seq is not None
    if decode:
        h1_ref, h2_ref, y_ref, gp_ref, xb_ref, acc_ref = rest
    else:
        y_ref, gp_ref, xb_ref, acc_ref, tail_ref = rest
    i = pl.program_id(0)
    j = pl.program_id(1)
    tm = x_ref.shape[0]
    keep = SUBLANES

    @pl.when(j == 0)
    def _():
        xb_ref[...] = x_ref[...].astype(BF16)
        acc_ref[...] = jnp.zeros_like(acc_ref)

    xb = xb_ref[...]
    gp = _dot(xb, wg_ref[...])
    up = _dot(xb, wu_ref[...])
    row = lax.broadcasted_iota(jnp.int32, (tm, 1), 0)
    r1 = pltpu.roll(gp, 1, 0)
    r2 = pltpu.roll(gp, 2, 0)
    if decode:
        t = row % dec_seq
        g1 = jnp.where(t >= 1, r1, 0.0) + h1_ref[...]
        g2 = jnp.where(t >= 2, r2, 0.0) + h2_ref[...]
        gp_ref[...] = gp
    else:
        tail = tail_ref[j]
        seq_start = (i % tiles_per_seq) == 0
        t6 = jnp.where(seq_start, 0.0, tail[keep - 2:keep - 1, :])
        t7 = jnp.where(seq_start, 0.0, tail[keep - 1:keep, :])
        g1 = jnp.where(row == 0, t7, r1)
        g2 = jnp.where(row == 0, t6, jnp.where(row == 1, t7, r2))
        tail_ref[j] = gp[tm - keep:, :]
        gp_ref[0] = gp[tm - keep:, :]
    cw = cw_ref[...]
    conv = cw[0:1, :] * g2 + cw[1:2, :] * g1 + cw[2:3, :] * gp + cb_ref[...]
    h = conv * _sigmoid(conv) * up
    acc_ref[...] += _dot(h.astype(BF16), wd_ref[...])

    @pl.when(j == pl.num_programs(1) - 1)
    def _():
        y_ref[...] = _layer_norm(alpha * x_ref[...] + acc_ref[...], g_ref[...], b_ref[...])


def _ffn(x1, lw, *, tm, tf, alpha, seq_len=None, hist=None, dec_seq=None):
    m, d = x1.shape
    d_ff = lw["w_down"].shape[0]
    assert m % tm == 0 and d_ff % tf == 0
    nf = d_ff // tf
    decode = hist is not None
    in_specs = [pl.BlockSpec((tm, d), lambda i, j: (i, 0)),
                pl.BlockSpec((d, tf), lambda i, j: (0, j)),
                pl.BlockSpec((d, tf), lambda i, j: (0, nf + j)),
                pl.BlockSpec((tf, d), lambda i, j: (j, 0)),
                pl.BlockSpec((SUBLANES, tf), lambda i, j: (0, j)),
                pl.BlockSpec((1, tf), lambda i, j: (0, j)),
                pl.BlockSpec((1, d), lambda i, j: (0, 0)),
                pl.BlockSpec((1, d), lambda i, j: (0, 0))]
    args = [x1, lw["w_up"], lw["w_up"], lw["w_down"], lw["conv_w"], lw["conv_b"], lw["ln2_g"], lw["ln2_b"]]
    scratch = [pltpu.VMEM((tm, d), BF16), pltpu.VMEM((tm, d), F32)]
    if decode:
        assert m == tm
        in_specs += [pl.BlockSpec((tm, tf), lambda i, j: (0, j))] * 2
        args += list(hist)
        gp_shape = jax.ShapeDtypeStruct((m, d_ff), F32)
        gp_spec = pl.BlockSpec((tm, tf), lambda i, j: (0, j))
        tiles_per_seq = None
    else:
        assert seq_len % tm == 0 and tm % SUBLANES == 0
        tiles_per_seq = seq_len // tm
        gp_shape = jax.ShapeDtypeStruct((m // tm, SUBLANES, d_ff), F32)
        gp_spec = pl.BlockSpec((1, SUBLANES, tf), lambda i, j: (i, 0, j))
        scratch += [pltpu.VMEM((nf, SUBLANES, tf), F32)]
    return pl.pallas_call(
        functools.partial(_ffn_kernel, alpha=alpha, tiles_per_seq=tiles_per_seq, dec_seq=dec_seq),
        grid=(m // tm, nf),
        in_specs=in_specs,
        out_specs=[pl.BlockSpec((tm, d), lambda i, j: (i, 0)), gp_spec],
        out_shape=[jax.ShapeDtypeStruct((m, d), F32), gp_shape],
        scratch_shapes=scratch,
        compiler_params=_params(("arbitrary", "arbitrary")),
        name="conv_ffn_ln",
    )(*args)


def _prep_layer(w_in, w_out, shift_mu, w0, w2, a0, a2, g2, k_k, k_a, r_k, gn_g, gn_b,
                ln1_g, ln1_b, ln2_g, ln2_b, w_up, conv_w, conv_b, w_down):
    row = lambda v: v.reshape(1, -1)
    zpad = Z_PAD - RWKV_PROJ
    w_in_r = jnp.pad(w_in.astype(BF16), ((0, 0), (0, zpad)))
    mu = jnp.pad(shift_mu, (0, zpad))
    zero_row = jnp.zeros((RWKV_WIDTH,), F32)
    return {
        "w_in": w_in_r,
        "w_out_att": w_out[:ATT_WIDTH].astype(BF16),
        "w_out_rw": w_out[ATT_WIDTH:].astype(BF16),
        "mu_rkv": jnp.pad(mu[:LORA_COL].reshape(3, RWKV_WIDTH), ((0, SUBLANES - 3), (0, 0))),
        "mu_lora": row(mu[LORA_COL:LORA_COL + LORA_BLOCK]),
        "rwkv_par": jnp.stack([w0, a0, k_k, k_a, r_k, gn_g, gn_b, zero_row]),
        "w2": jnp.pad(w2, ((0, LANES - D_DECAY_LORA), (0, 0))).astype(BF16),
        "a2": jnp.pad(a2, ((D_DECAY_LORA, LANES - D_DECAY_LORA - D_AAA_LORA), (0, 0))).astype(BF16),
        "g2": jnp.pad(g2, ((0, GATE_LORA_PAD - D_GATE_LORA), (0, 0))).astype(BF16),
        "ln1_g": row(ln1_g), "ln1_b": row(ln1_b), "ln2_g": row(ln2_g), "ln2_b": row(ln2_b),
        "w_up": w_up.astype(BF16),
        "conv_w": jnp.pad(conv_w, ((0, SUBLANES - CONV_W), (0, 0))),
        "conv_b": row(conv_b),
        "w_down": w_down.astype(BF16),
    }


def _pick_tile(n, cap):
    t = min(n, cap)
    while n % t:
        t //= 2
    return t


def _prompt_layer(x, lw, alpha):
    b, s, d = x.shape
    d_ff = lw["w_down"].shape[0]
    pos = jnp.arange(s, dtype=jnp.int32)
    proj, kh, vt = _in_proj(x, lw["w_in"], pos, tm=_pick_tile(s, 1024), attn_extras=True)
    att = _moba_prompt(proj, kh, vt)
    rw, wkv = _rwkv(proj, jnp.zeros((b, 1, Z_PAD), F32),
                    jnp.zeros((b, N_RWKV_HEADS, HEAD_DIM, HEAD_DIM), F32), lw,
                    tblk=_pick_tile(s, 512), t_real=None)
    x1 = _out_ln(att.reshape(b * s, ATT_WIDTH), rw.reshape(b * s, RWKV_WIDTH), x.reshape(b * s, d), lw,
                 tm=_pick_tile(s, 512), alpha=alpha)
    tm = _pick_tile(s, 512)
    y, gp_tail = _ffn(x1, lw, tm=tm, tf=512, alpha=alpha, seq_len=s)
    k_new = proj[:, :, K_COL:V_COL].reshape(b, s, N_KV_HEADS, HEAD_DIM)
    v_new = proj[:, :, V_COL:Z_COL].reshape(b, s, N_KV_HEADS, HEAD_DIM)
    shift_new = proj[:, s - 1, Z_COL:Z_COL + RWKV_PROJ]
    conv_new = gp_tail.reshape(b, s // tm, SUBLANES, d_ff)[:, -1, SUBLANES - (CONV_W - 1):, :]
    return y.reshape(b, s, d), k_new, v_new, wkv, shift_new, conv_new


def _decode_layer(x, cache_k, cache_v, page_table, shift0, wkv0, conv0, lw, alpha):
    bd, t, d = x.shape
    m = bd * t
    past = page_table.shape[1] * PAGE_SIZE
    pos = past + (jnp.arange(m, dtype=jnp.int32) % t)
    (proj,) = _in_proj(x.reshape(1, m, d), lw["w_in"], pos, tm=m, attn_extras=False)
    proj = proj.reshape(bd, t, PROJ_W)
    att = _moba_decode(proj, cache_k, cache_v, page_table, t)
    grp = _pick_tile(bd, RWKV_SEQS_PER_STEP)
    proj_pad = jnp.pad(proj, ((0, 0), (0, RWKV_CHUNK - t), (0, 0))).reshape(bd // grp, grp * RWKV_CHUNK, PROJ_W)
    shift_pad = jnp.pad(shift0, ((0, 0), (0, Z_PAD - RWKV_PROJ))).reshape(bd // grp, grp, Z_PAD)
    rw, wkv = _rwkv(proj_pad, shift_pad, wkv0, lw, tblk=grp * RWKV_CHUNK, t_real=t, independent=True)
    rw = rw.reshape(bd, RWKV_CHUNK, RWKV_WIDTH)[:, :t]
    x1 = _out_ln(att.reshape(m, ATT_WIDTH), rw.reshape(m, RWKV_WIDTH), x.reshape(m, d), lw, tm=m, alpha=alpha)
    d_ff = conv0.shape[-1]
    zeros = lambda n: jnp.zeros((bd, n, d_ff), F32)
    h1 = jnp.concatenate([conv0[:, 1:2], zeros(t - 1)], axis=1).reshape(m, d_ff)
    h2 = jnp.concatenate([conv0[:, 0:1], conv0[:, 1:2], zeros(t - 2)], axis=1).reshape(m, d_ff)
    y, gp = _ffn(x1, lw, tm=m, tf=512, alpha=alpha, hist=(h1, h2), dec_seq=t)
    k_new = proj[:, :, K_COL:V_COL].reshape(bd, t, N_KV_HEADS, HEAD_DIM)
    v_new = proj[:, :, V_COL:Z_COL].reshape(bd, t, N_KV_HEADS, HEAD_DIM)
    shift_new = proj[:, t - 1, Z_COL:Z_COL + RWKV_PROJ]
    conv_new = gp.reshape(bd, t, d_ff)[:, t - (CONV_W - 1):]
    return y.reshape(bd, t, d), k_new, v_new, wkv, shift_new, conv_new


def kernel(x_prompt, x_sample, cache_k, cache_v, page_table, state_wkv, state_shift, state_conv,
           w_in, w_out, shift_mu, w0, w2, a0, a2, g2, k_k, k_a, r_k, gn_g, gn_b,
           ln1_g, ln1_b, ln2_g, ln2_b, w_up, conv_w, conv_b, w_down):
    depth = w_in.shape[0]
    alpha = (2.0 * depth) ** 0.25
    assert x_sample.shape[1] >= CONV_W - 1
    yp, ys = x_prompt, x_sample
    outs_p, outs_s = [], []
    for l in range(depth):
        lw = _prep_layer(w_in[l], w_out[l], shift_mu[l], w0[l], w2[l], a0[l], a2[l], g2[l], k_k[l], k_a[l],
                         r_k[l], gn_g[l], gn_b[l], ln1_g[l], ln1_b[l], ln2_g[l], ln2_b[l],
                         w_up[l], conv_w[l], conv_b[l], w_down[l])
        yp, *rest_p = _prompt_layer(yp, lw, alpha)
        outs_p.append(rest_p)
        ys, *rest_s = _decode_layer(ys, cache_k[l], cache_v[l], page_table, state_shift[l], state_wkv[l],
                                    state_conv[l], lw, alpha)
        outs_s.append(rest_s)
    stack = lambda outs, n: jnp.stack([o[n] for o in outs])
    return (yp, ys,
            stack(outs_p, 0), stack(outs_p, 1), stack(outs_p, 2), stack(outs_p, 3), stack(outs_p, 4),
            stack(outs_s, 0), stack(outs_s, 1), stack(outs_s, 2), stack(outs_s, 3), stack(outs_s, 4))
```

```python
import functools

import jax
import jax.numpy as jnp
from jax import lax
from jax.experimental import pallas as pl
from jax.experimental.pallas import tpu as pltpu

F32 = jnp.float32
BF16 = jnp.bfloat16

HEAD_DIM = 64
N_ATT_HEADS = 16
N_KV_HEADS = 4
KV_GROUP = N_ATT_HEADS // N_KV_HEADS
ATT_WIDTH = N_ATT_HEADS * HEAD_DIM
KV_WIDTH = N_KV_HEADS * HEAD_DIM
N_RWKV_HEADS = 16
RWKV_WIDTH = N_RWKV_HEADS * HEAD_DIM
ROPE_DIM = HEAD_DIM // 4
ROPE_THETA = 500000.0
MOBA_BLOCK = 256
MOBA_TOPK = 3
D_DECAY_LORA = 64
D_AAA_LORA = 64
D_GATE_LORA = 160
LORA_WIDTH = D_DECAY_LORA + D_AAA_LORA + D_GATE_LORA
RWKV_PROJ = 3 * RWKV_WIDTH + LORA_WIDTH
CONV_W = 3
LN_EPS = 1e-5
GN_EPS = 64e-5
NEG_INF = -1e30
LOG2_E = 1.4426950408889634
PAGE_SIZE = 128

LANES = 128
SUBLANES = 8
VMEM_LIMIT_BYTES = 56 * 1024 * 1024

PROJ_TN = 512
Z_PAD = 3584
Q_COL = 0
K_COL = Q_COL + ATT_WIDTH
V_COL = K_COL + KV_WIDTH
Z_COL = V_COL + KV_WIDTH
PROJ_W = Z_COL + Z_PAD
LORA_COL = 3 * RWKV_WIDTH
LORA_BLOCK = 384
GATE_LORA_PAD = 256

RWKV_CHUNK = HEAD_DIM
RWKV_SEQS_PER_STEP = 8
RWKV_STAGE_GROUP = 8
PAGES_PER_STEP = 128

_NN = (((1,), (0,)), ((), ()))
_NT = (((1,), (1,)), ((), ()))


def _dot(a, b, dims=_NN):
    return lax.dot_general(a, b, dims, preferred_element_type=F32)


def _split_bf16(x):
    hi = x.astype(BF16)
    lo = (x - hi.astype(F32)).astype(BF16)
    return hi, lo


def _dot3(a, b, dims=_NN):
    ah, al = _split_bf16(a)
    bh, bl = _split_bf16(b)
    return _dot(ah, bh, dims) + (_dot(ah, bl, dims) + _dot(al, bh, dims))


def _sigmoid(x):
    return 1.0 / (1.0 + jnp.exp(-x))


def _layer_norm(x, g, b):
    mu = jnp.mean(x, axis=-1, keepdims=True)
    d = x - mu
    var = jnp.mean(d * d, axis=-1, keepdims=True)
    return d * lax.rsqrt(var + LN_EPS) * g + b


def _params(sem):
    return pltpu.CompilerParams(dimension_semantics=sem, vmem_limit_bytes=VMEM_LIMIT_BYTES)


def _rope_tables(pos):
    half = ROPE_DIM // 2
    inv = jnp.power(ROPE_THETA, -(jnp.arange(half, dtype=F32) * 2.0 / ROPE_DIM))
    ang = pos.astype(F32)[:, None] * inv[None, :]
    cos = jnp.cos(ang)
    sin = jnp.sin(ang)
    t = pos.shape[0]
    one = jnp.ones((t, HEAD_DIM - ROPE_DIM), F32)
    z8 = jnp.zeros((t, half), F32)
    zr = jnp.zeros((t, HEAD_DIM - ROPE_DIM), F32)
    c64 = jnp.concatenate([cos, cos, one], axis=1)
    a64 = jnp.concatenate([-sin, z8, zr], axis=1)
    b64 = jnp.concatenate([z8, sin, zr], axis=1)
    rep = LANES // HEAD_DIM
    return tuple(jnp.tile(m, (1, rep)) for m in (c64, a64, b64))


def _proj_block(step):
    n_blocks = PROJ_W // PROJ_TN
    return (step + Z_COL // PROJ_TN) % n_blocks


def _in_proj_kernel(x_ref, w_ref, cos_ref, sa_ref, sb_ref, proj_ref, *rest, attn_extras):
    if attn_extras:
        kh_ref, vt_ref, xb_ref = rest
    else:
        (xb_ref,) = rest
    n_q = ATT_WIDTH // PROJ_TN
    half = ROPE_DIM // 2

    @pl.when(pl.program_id(2) == 0)
    def _():
        xb_ref[...] = x_ref[0].astype(BF16)

    j = _proj_block(pl.program_id(2))

    acc = _dot(xb_ref[...], w_ref[...])

    def rope(slab):
        return (slab * cos_ref[...] + pltpu.roll(slab, LANES - half, 1) * sa_ref[...]
                + pltpu.roll(slab, half, 1) * sb_ref[...])

    @pl.when(j > n_q)
    def _():
        proj_ref[0] = acc

    @pl.when(j < n_q)
    def _():
        for s in range(PROJ_TN // LANES):
            proj_ref[0, :, s * LANES:(s + 1) * LANES] = rope(acc[:, s * LANES:(s + 1) * LANES])

    @pl.when(j == n_q)
    def _():
        k_slabs = []
        for s in range(KV_WIDTH // LANES):
            ks = rope(acc[:, s * LANES:(s + 1) * LANES])
            proj_ref[0, :, s * LANES:(s + 1) * LANES] = ks
            k_slabs.append(ks)
        v = acc[:, KV_WIDTH:]
        proj_ref[0, :, KV_WIDTH:] = v
        if attn_extras:
            k = jnp.concatenate(k_slabs, axis=1)
            tm = acc.shape[0]
            row_blk = (pl.program_id(1) * tm + lax.broadcasted_iota(jnp.int32, (tm, HEAD_DIM), 0)) // MOBA_BLOCK
            one_hot = (row_blk == lax.broadcasted_iota(jnp.int32, (tm, HEAD_DIM), 1)).astype(BF16)
            for c in range(N_KV_HEADS):
                kh_ref[0, c] = jnp.concatenate(
                    [k[:, c * HEAD_DIM:(c + 1) * HEAD_DIM].astype(BF16), one_hot], axis=1)
            vt_ref[0] = v.T.astype(BF16)


def _in_proj(x, w_b, pos, *, tm, attn_extras):
    b, s, d = x.shape
    assert s % tm == 0 and PROJ_W % PROJ_TN == 0
    cos, sa, sb = _rope_tables(pos)
    tab_spec = pl.BlockSpec((tm, LANES), lambda bi, i, j: (i, 0))
    out_shape = [jax.ShapeDtypeStruct((b, s, PROJ_W), F32)]
    out_specs = [pl.BlockSpec((1, tm, PROJ_TN), lambda bi, i, j: (bi, i, _proj_block(j)))]
    if attn_extras:
        assert s // MOBA_BLOCK <= HEAD_DIM
        out_shape += [jax.ShapeDtypeStruct((b, N_KV_HEADS, s, 2 * HEAD_DIM), BF16),
                      jax.ShapeDtypeStruct((b, KV_WIDTH, s), BF16)]
        out_specs += [pl.BlockSpec((1, N_KV_HEADS, tm, 2 * HEAD_DIM), lambda bi, i, j: (bi, 0, i, 0)),
                      pl.BlockSpec((1, KV_WIDTH, tm), lambda bi, i, j: (bi, 0, i))]
    return pl.pallas_call(
        functools.partial(_in_proj_kernel, attn_extras=attn_extras),
        grid=(b, s // tm, PROJ_W // PROJ_TN),
        in_specs=[pl.BlockSpec((1, tm, d), lambda bi, i, j: (bi, i, 0)),
                  pl.BlockSpec((d, PROJ_TN), lambda bi, i, j: (0, _proj_block(j))),
                  tab_spec, tab_spec, tab_spec],
        out_specs=out_specs,
        out_shape=out_shape,
        scratch_shapes=[pltpu.VMEM((tm, d), BF16)],
        compiler_params=_params(("parallel", "parallel", "arbitrary")),
        name="in_proj",
    )(x, w_b, cos, sa, sb)


def _moba_prompt_kernel(q_ref, kh_ref, vt_ref, o_ref, kmean_ref, *, n_blk, topk):
    i = pl.program_id(2)
    blk = MOBA_BLOCK
    n_cand = n_blk - 1
    scale = HEAD_DIM ** -0.5

    @pl.when(i == 0)
    def _():
        kmean_ref[...] = jnp.zeros_like(kmean_ref)
        for n in range(n_blk):
            kmean_ref[n:n + 1, :] = jnp.mean(
                kh_ref[0, 0, n * blk:(n + 1) * blk, :].astype(F32), axis=0, keepdims=True)

    q_t = q_ref[0].T
    qs_t = jnp.concatenate([q_t[g * HEAD_DIM:(g + 1) * HEAD_DIM, :] for g in range(KV_GROUP)],
                           axis=1)
    nq = KV_GROUP * blk

    gate = _dot3(kmean_ref[...], jnp.concatenate([qs_t, jnp.zeros((HEAD_DIM, nq), F32)], axis=0))
    n_idx = lax.broadcasted_iota(jnp.int32, gate.shape, 0)
    gate = jnp.where(n_idx < i, gate, NEG_INF)
    rank = jnp.zeros(gate.shape, F32)
    for m in range(n_cand):
        gm = gate[m:m + 1, :]
        beats = (gm > gate) | ((gm == gate) & (m < n_idx))
        rank = rank + beats.astype(F32)
    blk_bias = jnp.where(((rank < topk) & (n_idx < i)) | (n_idx == i), 0.0, NEG_INF)
    q_aug = jnp.concatenate(
        [qs_t * (scale * LOG2_E), blk_bias, jnp.zeros((HEAD_DIM - blk_bias.shape[0], nq), F32)],
        axis=0).astype(BF16)

    kpos = lax.broadcasted_iota(jnp.int32, (blk, nq), 0)
    qpos = lax.broadcasted_iota(jnp.int32, (blk, nq), 1) & (blk - 1)
    causal_bias = jnp.where(kpos <= qpos, 0.0, NEG_INF)

    def attend(nb):
        s = _dot(kh_ref[0, 0, 0:nb * blk, :], q_aug)
        own = s[(nb - 1) * blk:, :] + causal_bias
        s = jnp.concatenate([s[:(nb - 1) * blk, :], own], axis=0) if nb > 1 else own
        p = jnp.exp2(s - jnp.max(s, axis=0, keepdims=True))
        l = jnp.sum(p, axis=0, keepdims=True)
        o = _dot(vt_ref[0, :, 0:nb * blk], p.astype(BF16)) / l
        o4 = jnp.concatenate([o[:, g * blk:(g + 1) * blk] for g in range(KV_GROUP)], axis=0)
        o_ref[0] = o4.T.astype(o_ref.dtype)

    for nb in range(1, n_blk + 1):
        pl.when(i == nb - 1)(functools.partial(attend, nb))


def _moba_prompt(proj, kh, vt):
    b, s, _ = proj.shape
    assert s % MOBA_BLOCK == 0
    n_blk = s // MOBA_BLOCK
    topk = min(MOBA_TOPK, n_blk - 1)
    gw = KV_GROUP * HEAD_DIM
    n_rows = -(-n_blk // SUBLANES) * SUBLANES
    assert n_rows <= HEAD_DIM
    return pl.pallas_call(
        functools.partial(_moba_prompt_kernel, n_blk=n_blk, topk=topk),
        grid=(b, N_KV_HEADS, n_blk),
        in_specs=[pl.BlockSpec((1, MOBA_BLOCK, gw), lambda bi, c, i: (bi, i, Q_COL // gw + c)),
                  pl.BlockSpec((1, 1, s, 2 * HEAD_DIM), lambda bi, c, i: (bi, c, 0, 0)),
                  pl.BlockSpec((1, HEAD_DIM, s), lambda bi, c, i: (bi, c, 0))],
        out_specs=pl.BlockSpec((1, MOBA_BLOCK, gw), lambda bi, c, i: (bi, i, c)),
        out_shape=jax.ShapeDtypeStruct((b, s, ATT_WIDTH), BF16),
        scratch_shapes=[pltpu.VMEM((n_rows, 2 * HEAD_DIM), F32)],
        compiler_params=_params(("parallel", "parallel", "arbitrary")),
        name="moba_prompt",
    )(proj, kh, vt)


def _moba_decode_scores_kernel(pt_ref, wt_ref, *rest, n_new):
    del pt_ref
    r_pages = PAGES_PER_STEP
    k_refs = rest[:r_pages]
    s_ref, gate_ref, bmax_ref = rest[r_pages:]
    j = pl.program_id(1)
    scale = HEAD_DIM ** -0.5
    nq = N_ATT_HEADS * n_new
    pages_per_blk = MOBA_BLOCK // PAGE_SIZE
    blks_per_step = r_pages // pages_per_blk
    lane = lax.broadcasted_iota(jnp.int32, gate_ref.shape[1:], 1)

    @pl.when(j == 0)
    def _():
        gate_ref[0] = jnp.zeros(gate_ref.shape[1:], F32)
        bmax_ref[0] = jnp.full(bmax_ref.shape[1:], NEG_INF, F32)

    wt2 = jnp.concatenate(_split_bf16(wt_ref[0]), axis=0)
    gate = gate_ref[0]
    bmax = bmax_ref[0]
    for r in range(r_pages):
        s2 = _dot(wt2, k_refs[r][0].astype(BF16))
        raw = s2[:nq] + s2[nq:]
        s_ref[0, :, r * PAGE_SIZE:(r + 1) * PAGE_SIZE] = raw * scale
        first = r % pages_per_blk == 0
        blk_sum = raw if first else blk_sum + raw
        blk_max = raw if first else jnp.maximum(blk_max, raw)
        if r % pages_per_blk == pages_per_blk - 1:
            here = lane == j * blks_per_step + r // pages_per_blk
            gate = jnp.where(here, jnp.sum(blk_sum, axis=1, keepdims=True), gate)
            bmax = jnp.where(here, jnp.max(blk_max, axis=1, keepdims=True) * scale, bmax)
    gate_ref[0] = gate
    bmax_ref[0] = bmax


def _moba_decode_values_kernel(pt_ref, wt_ref, kvn_ref, gate_ref, bmax_ref, s_ref, *rest,
                               n_cand, n_new):
    del pt_ref
    r_pages = PAGES_PER_STEP
    v_refs = rest[:r_pages]
    o_ref = rest[r_pages]
    sel_ref, acc_ref, l_ref, m_ref = rest[r_pages + 1:]
    n_cand_pad = gate_ref.shape[2]
    j = pl.program_id(1)
    scale = HEAD_DIM ** -0.5
    pages_per_blk = MOBA_BLOCK // PAGE_SIZE
    blks_per_step = r_pages // pages_per_blk

    @pl.when(j == 0)
    def _():
        lane = lax.broadcasted_iota(jnp.int32, gate_ref.shape[1:], 1)
        lane_f = lane.astype(F32)
        gate = jnp.where(lane < n_cand, gate_ref[0] * (1.0 / MOBA_BLOCK), NEG_INF)
        sel = lane < 0
        for _ in range(min(MOBA_TOPK, n_cand)):
            top = jnp.max(gate, axis=1, keepdims=True)
            first_top = jnp.min(jnp.where(gate == top, lane_f, float(n_cand_pad)), axis=1, keepdims=True)
            pick = lane_f == first_top
            sel = sel | pick
            gate = jnp.where(pick, -jnp.inf, gate)
        sel = sel & (lane < n_cand)
        sel_ref[...] = sel.astype(F32)

        kvn = kvn_ref[0]
        k_new = kvn[:, :KV_WIDTH].astype(BF16)
        v_new = kvn[:, KV_WIDTH:].astype(BF16)
        s_new = _dot((wt_ref[0] * scale).astype(BF16), k_new, _NT)
        t_q = lax.broadcasted_iota(jnp.int32, s_new.shape, 0) % n_new
        t_k = lax.broadcasted_iota(jnp.int32, s_new.shape, 1)
        ok_new = (t_k <= t_q) & (t_k < n_new)
        m = jnp.maximum(jnp.max(jnp.where(ok_new, s_new, NEG_INF), axis=1, keepdims=True),
                        jnp.max(jnp.where(sel, bmax_ref[0], NEG_INF), axis=1, keepdims=True))
        p_new = jnp.where(ok_new, jnp.exp(s_new - m), 0.0)
        m_ref[...] = jnp.broadcast_to(m, m_ref.shape)
        l_lane = lax.broadcasted_iota(jnp.int32, l_ref.shape, 1)
        l_ref[...] = jnp.where(l_lane == 0, jnp.sum(p_new, axis=1, keepdims=True), 0.0)
        acc_ref[...] = _dot(p_new.astype(BF16), v_new)

    src = lax.broadcasted_iota(jnp.int32, (n_cand_pad, LANES), 0)
    dst = lax.broadcasted_iota(jnp.int32, (n_cand_pad, LANES), 1)
    onehot = ((src == j * blks_per_step + dst) & (dst < blks_per_step)).astype(BF16)
    sel_here = _dot(sel_ref[...].astype(BF16), onehot)
    m = m_ref[:, 0:1]
    acc = acc_ref[...]
    l_part = l_ref[...]
    for r in range(r_pages):
        blk = r // pages_per_blk
        sb = s_ref[0, :, r * PAGE_SIZE:(r + 1) * PAGE_SIZE]
        pb = jnp.where(sel_here[:, blk:blk + 1] > 0.5, jnp.exp(sb - m), 0.0)
        l_part = l_part + pb
        acc = acc + _dot(pb.astype(BF16), v_refs[r][0].astype(BF16), _NT)
    acc_ref[...] = acc
    l_ref[...] = l_part

    @pl.when(j == pl.num_programs(1) - 1)
    def _():
        o_ref[0] = acc / jnp.sum(l_part, axis=1, keepdims=True)


def _moba_decode(proj_d, cache_k, cache_v, page_table, n_new):
    bd = proj_d.shape[0]
    n_pages = page_table.shape[1]
    past = n_pages * PAGE_SIZE
    assert past % MOBA_BLOCK == 0 and n_pages % PAGES_PER_STEP == 0 and n_new <= SUBLANES
    n_cand = past // MOBA_BLOCK
    n_steps = n_pages // PAGES_PER_STEP
    n_phys = cache_k.shape[0]
    nq = N_ATT_HEADS * n_new
    ck = jnp.transpose(cache_k, (0, 2, 3, 1)).reshape(n_phys, KV_WIDTH, PAGE_SIZE)
    cv = jnp.transpose(cache_v, (0, 2, 3, 1)).reshape(n_phys, KV_WIDTH, PAGE_SIZE)

    q = proj_d[:, :, Q_COL:Q_COL + ATT_WIDTH].reshape(bd, n_new, N_KV_HEADS, KV_GROUP, HEAD_DIM)
    q = jnp.transpose(q, (0, 2, 3, 1, 4)).reshape(bd, N_KV_HEADS, KV_GROUP * n_new, 1, HEAD_DIM)
    eye = jnp.eye(N_KV_HEADS, dtype=F32)[None, :, None, :, None]
    wt = (q * eye).reshape(bd, nq, KV_WIDTH)
    kvn = jnp.pad(proj_d[:, :, K_COL:K_COL + 2 * KV_WIDTH], ((0, 0), (0, SUBLANES - n_new), (0, 0)))

    def page_map(r):
        return lambda b, j, pt: (pt[b, j * PAGES_PER_STEP + r], 0, 0)

    step_keys = PAGES_PER_STEP * PAGE_SIZE
    n_cand_pad = -(-n_cand // LANES) * LANES
    per_seq = lambda w: pl.BlockSpec((1, nq, w), lambda b, j, pt: (b, 0, 0))
    page_specs = [pl.BlockSpec((1, KV_WIDTH, PAGE_SIZE), page_map(r)) for r in range(PAGES_PER_STEP)]
    score_spec = pl.BlockSpec((1, nq, step_keys), lambda b, j, pt: (b, 0, j))
    scores, gate, bmax = pl.pallas_call(
        functools.partial(_moba_decode_scores_kernel, n_new=n_new),
        grid_spec=pltpu.PrefetchScalarGridSpec(
            num_scalar_prefetch=1,
            grid=(bd, n_steps),
            in_specs=[per_seq(KV_WIDTH)] + page_specs,
            out_specs=[score_spec, per_seq(n_cand_pad), per_seq(n_cand_pad)]),
        out_shape=[jax.ShapeDtypeStruct((bd, nq, past), F32),
                   jax.ShapeDtypeStruct((bd, nq, n_cand_pad), F32),
                   jax.ShapeDtypeStruct((bd, nq, n_cand_pad), F32)],
        compiler_params=_params(("parallel", "arbitrary")),
        name="moba_decode_scores",
    )(page_table, wt, *([ck] * PAGES_PER_STEP))
    out = pl.pallas_call(
        functools.partial(_moba_decode_values_kernel, n_cand=n_cand, n_new=n_new),
        grid_spec=pltpu.PrefetchScalarGridSpec(
            num_scalar_prefetch=1,
            grid=(bd, n_steps),
            in_specs=[per_seq(KV_WIDTH),
                      pl.BlockSpec((1, SUBLANES, 2 * KV_WIDTH), lambda b, j, pt: (b, 0, 0)),
                      per_seq(n_cand_pad), per_seq(n_cand_pad), score_spec] + page_specs,
            out_specs=per_seq(KV_WIDTH),
            scratch_shapes=[pltpu.VMEM((nq, n_cand_pad), F32),
                            pltpu.VMEM((nq, KV_WIDTH), F32),
                            pltpu.VMEM((nq, LANES), F32),
                            pltpu.VMEM((nq, LANES), F32)]),
        out_shape=jax.ShapeDtypeStruct((bd, nq, KV_WIDTH), F32),
        compiler_params=_params(("parallel", "arbitrary")),
        name="moba_decode_values",
    )(page_table, wt, kvn, gate, bmax, scores, *([cv] * PAGES_PER_STEP))
    o = out.reshape(bd, N_KV_HEADS, KV_GROUP, n_new, N_KV_HEADS, HEAD_DIM)
    o = jnp.stack([o[:, c, :, :, c, :] for c in range(N_KV_HEADS)], axis=1)
    o = jnp.transpose(o, (0, 3, 1, 2, 4)).reshape(bd, n_new, ATT_WIDTH)
    return o.astype(BF16)


def _rwkv_kernel(zr_ref, zk_ref, zv_ref, zl_ref, sr_ref, sk_ref, sv_ref, sl_ref,
                 mu_ref, mul_ref, par_ref, w2_ref, a2_ref, g2_ref, s0_ref,
                 rw_ref, wkv_ref,
                 state_ref, pr_ref, pk_ref, pv_ref, plo_ref, *, t_real, independent):
    tb = pl.program_id(2)
    n_tb = pl.num_programs(2)
    tblk = zr_ref.shape[1]
    c_len = RWKV_CHUNK
    width = LANES
    lane = lax.broadcasted_iota(jnp.int32, (1, width), 1)
    head_a = lane < HEAD_DIM
    row = lax.broadcasted_iota(jnp.int32, (tblk, 1), 0)

    def block_diag(sa, sb):
        z = jnp.zeros((HEAD_DIM, HEAD_DIM), F32)
        return jnp.concatenate([jnp.concatenate([sa, z], axis=1),
                                jnp.concatenate([z, sb], axis=1)], axis=0)

    if independent:
        def token_shift(z, first_ref, prev_ref, mu):
            z_prev = pltpu.roll(z, 1, 0)
            first = first_ref[0]
            for c in range(tblk // c_len):
                z_prev = jnp.where(row == c * c_len, first[c:c + 1, :], z_prev)
            return z + (z_prev - z) * mu
    else:
        @pl.when(tb == 0)
        def _():
            state_ref[...] = block_diag(s0_ref[0, 0], s0_ref[0, 1])
            pr_ref[0:1, :] = sr_ref[0]
            pk_ref[0:1, :] = sk_ref[0]
            pv_ref[0:1, :] = sv_ref[0]
            plo_ref[0:1, :] = sl_ref[0]

        def token_shift(z, first_ref, prev_ref, mu):
            z_prev = jnp.where(row == 0, prev_ref[0:1, :], pltpu.roll(z, 1, 0))
            prev_ref[0:1, :] = z[tblk - 1:tblk, :]
            return z + (z_prev - z) * mu

    def seg_sum(x):
        sa = jnp.sum(jnp.where(head_a, x, 0.0), axis=1, keepdims=True)
        sb = jnp.sum(jnp.where(head_a, 0.0, x), axis=1, keepdims=True)
        return jnp.where(head_a, sa, sb)

    r = token_shift(zr_ref[0], sr_ref, pr_ref, mu_ref[0:1, :])
    k = token_shift(zk_ref[0], sk_ref, pk_ref, mu_ref[1:2, :])
    v = token_shift(zv_ref[0], sv_ref, pv_ref, mu_ref[2:3, :])
    lo = token_shift(zl_ref[0], sl_ref, plo_ref, mul_ref[...])
    par = par_ref[...]
    w0, a0, k_k, k_a, r_k, gn_g, gn_b = (par[n:n + 1, :] for n in range(7))

    lo_wa = lo[:, :LANES]
    nx = -(w0 + _dot(jnp.tanh(lo_wa).astype(BF16), w2_ref[...]))
    softplus = jnp.maximum(nx, 0.0) + jnp.log(1.0 + jnp.exp(-jnp.abs(nx)))
    logw = -jnp.exp(-softplus - 0.5)
    a = _sigmoid(a0 + _dot(lo_wa.astype(BF16), a2_ref[...]))
    g = _dot(_sigmoid(lo[:, LANES:]).astype(BF16), g2_ref[...])
    kk = k * k_k
    kk = kk / jnp.maximum(jnp.sqrt(seg_sum(kk * kk)), 1e-12)
    k = k * (1.0 + (a - 1.0) * k_a)
    bonus = seg_sum(r * k * r_k) * v

    if t_real is not None:
        valid = ((row & (c_len - 1)) if independent else (tb * tblk + row)) < t_real
        logw = jnp.where(valid, logw, 0.0)
        kk = jnp.where(valid, kk, 0.0)
        k = jnp.where(valid, k, 0.0)
        v_in = jnp.where(valid, v, 0.0)
    else:
        v_in = v

    def pair_rows(x):
        return jnp.concatenate([jnp.where(head_a, x, 0.0), jnp.where(head_a, 0.0, x)], axis=0)

    ri = lax.broadcasted_iota(jnp.int32, (c_len, 2 * c_len), 0)
    ci = lax.broadcasted_iota(jnp.int32, (c_len, 2 * c_len), 1)
    first = ci < c_len
    tcol = ci & (c_len - 1)
    strict = tcol < ri
    incl = tcol <= ri
    eye2 = (tcol == ri).astype(F32)
    bi = lax.broadcasted_iota(jnp.int32, (width, width), 0) < HEAD_DIM
    bj = lax.broadcasted_iota(jnp.int32, (width, width), 1) < HEAD_DIM
    same_head = bi == bj
    live = c_len if t_real is None else min(t_real, c_len)
    n_levels = max((live - 1).bit_length() - 1, 0)
    shift = c_len.bit_length() - 1
    chunks = [slice(c * c_len, (c + 1) * c_len) for c in range(tblk // c_len)]
    bf = lambda x: x.astype(BF16)

    tr = lax.broadcasted_iota(jnp.int32, (tblk, tblk), 0)
    tc = lax.broadcasted_iota(jnp.int32, (tblk, tblk), 1)
    tri = ((tr >= tc) & ((tr >> shift) == (tc >> shift))).astype(BF16)
    lh = logw.astype(BF16)
    l1 = logw - lh.astype(F32)
    lm = l1.astype(BF16)
    ll = (l1 - lm.astype(F32)).astype(BF16)
    lc = _dot(tri, lh) + (_dot(tri, lm) + _dot(tri, ll))
    g_in = jnp.exp(lc)
    g_inv = jnp.exp(-lc)
    a_all = -kk * jnp.exp(lc - logw)
    b_all = kk * a * g_inv
    k_all = k * g_inv
    r_all = r * g_in

    def group_terms(sls):
        a_t = [a_all[sl] for sl in sls]
        r_t = [r_all[sl] for sl in sls]
        b_t = [b_all[sl] for sl in sls]
        k_t = [k_all[sl] for sl in sls]
        v_c = [v_in[sl] for sl in sls]
        g_end = [g_in[sl.stop - 1:sl.stop, :] for sl in sls]
        a_pair = [pair_rows(bf(x)) for x in a_t]
        s4 = [_dot(jnp.concatenate([ap, pair_rows(bf(rt))], axis=0),
                   bf(jnp.concatenate([bt, kt], axis=0)), _NT)
              for ap, rt, bt, kt in zip(a_pair, r_t, b_t, k_t)]
        m_ab, m_ak, m_rb, m_rk = [], [], [], []
        for x in s4:
            sa_, sb_ = x[0:c_len], x[c_len:2 * c_len]
            ra_, rb_ = x[2 * c_len:3 * c_len], x[3 * c_len:4 * c_len]
            m_ab.append(jnp.where(strict, jnp.where(first, sa_, pltpu.roll(sb_, c_len, 1)), 0.0))
            m_ak.append(jnp.where(strict, jnp.where(first, pltpu.roll(sa_, c_len, 1), sb_), 0.0))
            m_rb.append(jnp.where(incl, jnp.where(first, ra_, pltpu.roll(rb_, c_len, 1)), 0.0))
            m_rk.append(jnp.where(incl, jnp.where(first, pltpu.roll(ra_, c_len, 1), rb_), 0.0))

        t_inv = [m + eye2 for m in m_ab]
        if n_levels > 0:
            l_pow = [bf(m) for m in m_ab]
            l_pow = [bf(_dot(lp, pair_rows(lp))) for lp in l_pow]
            for level in range(n_levels):
                qp = [_dot(lp, jnp.concatenate([pair_rows(lp), pair_rows(bf(ti))], axis=1))
                      for lp, ti in zip(l_pow, t_inv)]
                t_inv = [ti + x[:, width:] for ti, x in zip(t_inv, qp)]
                if level + 1 < n_levels:
                    l_pow = [bf(x[:, :width]) for x in qp]

        v_pair = [pair_rows(bf(x)) for x in v_c]
        mak_v = [_dot(bf(m), vp) for m, vp in zip(m_ak, v_pair)]
        tw = [_dot(bf(ti), jnp.concatenate([ap, pair_rows(bf(mv))], axis=1))
              for ti, ap, mv in zip(t_inv, a_pair, mak_v)]
        w_t = [x[:, :width] for x in tw]
        v_hat = [x[:, width:] for x in tw]
        mw = [_dot(bf(jnp.concatenate([mb, mk], axis=1)),
                   jnp.concatenate(
                       [jnp.concatenate([pair_rows(bf(w)), pair_rows(bf(vh))], axis=1),
                        jnp.concatenate([jnp.zeros_like(vp), vp], axis=1)], axis=0))
              for mb, mk, w, vh, vp in zip(m_rb, m_rk, w_t, v_hat, v_pair)]
        p_m = [rt + x[:, :width] for rt, x in zip(r_t, mw)]
        q_m = [x[:, width:] for x in mw]
        phi, psi = [], []
        for w, vh, vc, bt, kt, ge in zip(w_t, v_hat, v_c, b_t, k_t, g_end):
            bk = bf(jnp.concatenate([bt * ge, kt * ge], axis=0))
            left = jnp.concatenate(
                [jnp.concatenate([w, jnp.zeros_like(w)], axis=0),
                 jnp.concatenate([vh, vc], axis=0)], axis=1)
            pp = _dot(bf(left.T), bk)
            phi.append(bf(jnp.where(same_head, pp[:width], 0.0)))
            psi.append(jnp.where(same_head, pp[width:], 0.0))
        return list(zip(p_m, q_m, phi, psi, g_end))

    def advance(s, term):
        pm, qm, ph, ps, ge = term
        sb16 = bf(s)
        return _dot(bf(pm), sb16, _NT) + qm, s * ge + _dot(sb16, ph) + ps

    outs = []
    s = None if independent else state_ref[...]
    for g0 in range(0, len(chunks), RWKV_STAGE_GROUP):
        for c, term in enumerate(group_terms(chunks[g0:g0 + RWKV_STAGE_GROUP]), start=g0):
            if independent:
                o_c, s_c = advance(block_diag(s0_ref[c, 0], s0_ref[c, 1]), term)
                wkv_ref[c, 0] = s_c[:HEAD_DIM, :HEAD_DIM]
                wkv_ref[c, 1] = s_c[HEAD_DIM:, HEAD_DIM:]
            else:
                o_c, s = advance(s, term)
            outs.append(o_c)
    if not independent:
        state_ref[...] = s

        @pl.when(tb == n_tb - 1)
        def _():
            wkv_ref[0, 0] = s[:HEAD_DIM, :HEAD_DIM]
            wkv_ref[0, 1] = s[HEAD_DIM:, HEAD_DIM:]

    o = jnp.concatenate(outs, axis=0) if len(outs) > 1 else outs[0]
    inv_n = 1.0 / HEAD_DIM
    mu_o = seg_sum(o) * inv_n
    d = o - mu_o
    var = seg_sum(d * d) * inv_n
    on = d * lax.rsqrt(var + GN_EPS) * gn_g + gn_b
    rw_ref[0] = ((on + bonus) * g).astype(rw_ref.dtype)


def _rwkv(proj, shift0, wkv0, lw, *, tblk, t_real, independent=False):
    b, s, _ = proj.shape
    assert s % tblk == 0 and tblk % RWKV_CHUNK == 0
    n_pairs = RWKV_WIDTH // LANES
    rkv = RWKV_WIDTH // LANES
    lora_blk = LORA_COL // LORA_BLOCK
    n_first = shift0.shape[1]
    n_state = tblk // RWKV_CHUNK if independent else 1
    assert (not independent) or (s == tblk and n_first == n_state)

    def z_spec(width, col0):
        return pl.BlockSpec((1, tblk, width), lambda bi, p, t: (bi, t, col0(p)))

    def s_spec(width, col0):
        return pl.BlockSpec((1, n_first, width), lambda bi, p, t: (bi, 0, col0(p)))

    state_spec = pl.BlockSpec((n_state, 2, HEAD_DIM, HEAD_DIM), lambda bi, p, t: (bi, p, 0, 0))

    assert Z_COL % LANES == 0 and (Z_COL + LORA_COL) % LORA_BLOCK == 0
    cols = [lambda p: p, lambda p: rkv + p, lambda p: 2 * rkv + p]
    zc = Z_COL // LANES
    in_specs = ([z_spec(LANES, lambda p, c=c: zc + c(p)) for c in cols]
                + [z_spec(LORA_BLOCK, lambda p: (Z_COL + LORA_COL) // LORA_BLOCK)]
                + [s_spec(LANES, c) for c in cols] + [s_spec(LORA_BLOCK, lambda p: lora_blk)]
                + [pl.BlockSpec((SUBLANES, LANES), lambda bi, p, t: (0, p)),
                   pl.BlockSpec((1, LORA_BLOCK), lambda bi, p, t: (0, 0)),
                   pl.BlockSpec((SUBLANES, LANES), lambda bi, p, t: (0, p)),
                   pl.BlockSpec((LANES, LANES), lambda bi, p, t: (0, p)),
                   pl.BlockSpec((LANES, LANES), lambda bi, p, t: (0, p)),
                   pl.BlockSpec((GATE_LORA_PAD, LANES), lambda bi, p, t: (0, p)),
                   state_spec])
    return pl.pallas_call(
        functools.partial(_rwkv_kernel, t_real=t_real, independent=independent),
        grid=(b, n_pairs, s // tblk),
        in_specs=in_specs,
        out_specs=[pl.BlockSpec((1, tblk, LANES), lambda bi, p, t: (bi, t, p)), state_spec],
        out_shape=[jax.ShapeDtypeStruct((b, s, RWKV_WIDTH), BF16),
                   jax.ShapeDtypeStruct((b * n_state, N_RWKV_HEADS, HEAD_DIM, HEAD_DIM), F32)],
        scratch_shapes=[pltpu.VMEM((LANES, LANES), F32),
                        pltpu.VMEM((SUBLANES, LANES), F32),
                        pltpu.VMEM((SUBLANES, LANES), F32),
                        pltpu.VMEM((SUBLANES, LANES), F32),
                        pltpu.VMEM((SUBLANES, LORA_BLOCK), F32)],
        compiler_params=_params(("parallel", "parallel", "arbitrary")),
        name="rwkv7",
    )(proj, proj, proj, proj, shift0, shift0, shift0, shift0,
      lw["mu_rkv"], lw["mu_lora"], lw["rwkv_par"], lw["w2"], lw["a2"], lw["g2"], wkv0)


def _out_ln_kernel(att_ref, rw_ref, x_ref, wa_ref, wr_ref, g_ref, b_ref, o_ref, *, alpha):
    mix = _dot(att_ref[...], wa_ref[...]) + _dot(rw_ref[...], wr_ref[...])
    o_ref[...] = _layer_norm(alpha * x_ref[...] + mix, g_ref[...], b_ref[...])


def _out_ln(att, rw, x, lw, *, tm, alpha):
    m, d = x.shape
    assert m % tm == 0
    row = lambda w: pl.BlockSpec((tm, w), lambda i: (i, 0))
    full = lambda r, w: pl.BlockSpec((r, w), lambda i: (0, 0))
    return pl.pallas_call(
        functools.partial(_out_ln_kernel, alpha=alpha),
        grid=(m // tm,),
        in_specs=[row(ATT_WIDTH), row(RWKV_WIDTH), row(d),
                  full(ATT_WIDTH, d), full(RWKV_WIDTH, d), full(1, d), full(1, d)],
        out_specs=row(d),
        out_shape=jax.ShapeDtypeStruct((m, d), F32),
        compiler_params=_params(("parallel",)),
        name="out_proj_ln",
    )(att, rw, x, lw["w_out_att"], lw["w_out_rw"], lw["ln1_g"], lw["ln1_b"])


def _ffn_kernel(x_ref, wg_ref, wu_ref, wd_ref, cw_ref, cb_ref, g_ref, b_ref, *rest,
                alpha, tiles_per_seq, dec_seq):
    decode = dec_seq is not None
    if decode:
        h1_ref, h2_ref, y_ref, gp_ref, xb_ref, acc_ref = rest
    else:
        y_ref, gp_ref, xb_ref, acc_ref, tail_ref = rest
    i = pl.program_id(0)
    j = pl.program_id(1)
    tm = x_ref.shape[0]
    keep = SUBLANES

    @pl.when(j == 0)
    def _():
        xb_ref[...] = x_ref[...].astype(BF16)
        acc_ref[...] = jnp.zeros_like(acc_ref)

    xb = xb_ref[...]
    gp = _dot(xb, wg_ref[...])
    up = _dot(xb, wu_ref[...])
    row = lax.broadcasted_iota(jnp.int32, (tm, 1), 0)
    r1 = pltpu.roll(gp, 1, 0)
    r2 = pltpu.roll(gp, 2, 0)
    if decode:
        t = row % dec_seq
        g1 = jnp.where(t >= 1, r1, 0.0) + h1_ref[...]
        g2 = jnp.where(t >= 2, r2, 0.0) + h2_ref[...]
        gp_ref[...] = gp
    else:
        tail = tail_ref[j]
        seq_start = (i % tiles_per_seq) == 0
        t6 = jnp.where(seq_start, 0.0, tail[keep - 2:keep - 1, :])
        t7 = jnp.where(seq_start, 0.0, tail[keep - 1:keep, :])
        g1 = jnp.where(row == 0, t7, r1)
        g2 = jnp.where(row == 0, t6, jnp.where(row == 1, t7, r2))
        tail_ref[j] = gp[tm - keep:, :]
        gp_ref[0] = gp[tm - keep:, :]
    cw = cw_ref[...]
    conv = cw[0:1, :] * g2 + cw[1:2, :] * g1 + cw[2:3, :] * gp + cb_ref[...]
    h = conv * _sigmoid(conv) * up
    acc_ref[...] += _dot(h.astype(BF16), wd_ref[...])

    @pl.when(j == pl.num_programs(1) - 1)
    def _():
        y_ref[...] = _layer_norm(alpha * x_ref[...] + acc_ref[...], g_ref[...], b_ref[...])


def _ffn(x1, lw, *, tm, tf, alpha, seq_len=None, hist=None, dec_seq=None):
    m, d = x1.shape
    d_ff = lw["w_down"].shape[0]
    assert m % tm == 0 and d_ff % tf == 0
    nf = d_ff // tf
    decode = hist is not None
    in_specs = [pl.BlockSpec((tm, d), lambda i, j: (i, 0)),
                pl.BlockSpec((d, tf), lambda i, j: (0, j)),
                pl.BlockSpec((d, tf), lambda i, j: (0, nf + j)),
                pl.BlockSpec((tf, d), lambda i, j: (j, 0)),
                pl.BlockSpec((SUBLANES, tf), lambda i, j: (0, j)),
                pl.BlockSpec((1, tf), lambda i, j: (0, j)),
                pl.BlockSpec((1, d), lambda i, j: (0, 0)),
                pl.BlockSpec((1, d), lambda i, j: (0, 0))]
    args = [x1, lw["w_up"], lw["w_up"], lw["w_down"], lw["conv_w"], lw["conv_b"], lw["ln2_g"], lw["ln2_b"]]
    scratch = [pltpu.VMEM((tm, d), BF16), pltpu.VMEM((tm, d), F32)]
    if decode:
        assert m == tm
        in_specs += [pl.BlockSpec((tm, tf), lambda i, j: (0, j))] * 2
        args += list(hist)
        gp_shape = jax.ShapeDtypeStruct((m, d_ff), F32)
        gp_spec = pl.BlockSpec((tm, tf), lambda i, j: (0, j))
        tiles_per_seq = None
    else:
        assert seq_len % tm == 0 and tm % SUBLANES == 0
        tiles_per_seq = seq_len // tm
        gp_shape = jax.ShapeDtypeStruct((m // tm, SUBLANES, d_ff), F32)
        gp_spec = pl.BlockSpec((1, SUBLANES, tf), lambda i, j: (i, 0, j))
        scratch += [pltpu.VMEM((nf, SUBLANES, tf), F32)]
    return pl.pallas_call(
        functools.partial(_ffn_kernel, alpha=alpha, tiles_per_seq=tiles_per_seq, dec_seq=dec_seq),
        grid=(m // tm, nf),
        in_specs=in_specs,
        out_specs=[pl.BlockSpec((tm, d), lambda i, j: (i, 0)), gp_spec],
        out_shape=[jax.ShapeDtypeStruct((m, d), F32), gp_shape],
        scratch_shapes=scratch,
        compiler_params=_params(("arbitrary", "arbitrary")),
        name="conv_ffn_ln",
    )(*args)


def _prep_layer(w_in, w_out, shift_mu, w0, w2, a0, a2, g2, k_k, k_a, r_k, gn_g, gn_b,
                ln1_g, ln1_b, ln2_g, ln2_b, w_up, conv_w, conv_b, w_down):
    row = lambda v: v.reshape(1, -1)
    zpad = Z_PAD - RWKV_PROJ
    w_in_r = jnp.pad(w_in.astype(BF16), ((0, 0), (0, zpad)))
    mu = jnp.pad(shift_mu, (0, zpad))
    zero_row = jnp.zeros((RWKV_WIDTH,), F32)
    return {
        "w_in": w_in_r,
        "w_out_att": w_out[:ATT_WIDTH].astype(BF16),
        "w_out_rw": w_out[ATT_WIDTH:].astype(BF16),
        "mu_rkv": jnp.pad(mu[:LORA_COL].reshape(3, RWKV_WIDTH), ((0, SUBLANES - 3), (0, 0))),
        "mu_lora": row(mu[LORA_COL:LORA_COL + LORA_BLOCK]),
        "rwkv_par": jnp.stack([w0, a0, k_k, k_a, r_k, gn_g, gn_b, zero_row]),
        "w2": jnp.pad(w2, ((0, LANES - D_DECAY_LORA), (0, 0))).astype(BF16),
        "a2": jnp.pad(a2, ((D_DECAY_LORA, LANES - D_DECAY_LORA - D_AAA_LORA), (0, 0))).astype(BF16),
        "g2": jnp.pad(g2, ((0, GATE_LORA_PAD - D_GATE_LORA), (0, 0))).astype(BF16),
        "ln1_g": row(ln1_g), "ln1_b": row(ln1_b), "ln2_g": row(ln2_g), "ln2_b": row(ln2_b),
        "w_up": w_up.astype(BF16),
        "conv_w": jnp.pad(conv_w, ((0, SUBLANES - CONV_W), (0, 0))),
        "conv_b": row(conv_b),
        "w_down": w_down.astype(BF16),
    }


def _pick_tile(n, cap):
    t = min(n, cap)
    while n % t:
        t //= 2
    return t


def _prompt_layer(x, lw, alpha):
    b, s, d = x.shape
    d_ff = lw["w_down"].shape[0]
    pos = jnp.arange(s, dtype=jnp.int32)
    proj, kh, vt = _in_proj(x, lw["w_in"], pos, tm=_pick_tile(s, 1024), attn_extras=True)
    att = _moba_prompt(proj, kh, vt)
    rw, wkv = _rwkv(proj, jnp.zeros((b, 1, Z_PAD), F32),
                    jnp.zeros((b, N_RWKV_HEADS, HEAD_DIM, HEAD_DIM), F32), lw,
                    tblk=_pick_tile(s, 512), t_real=None)
    x1 = _out_ln(att.reshape(b * s, ATT_WIDTH), rw.reshape(b * s, RWKV_WIDTH), x.reshape(b * s, d), lw,
                 tm=_pick_tile(s, 512), alpha=alpha)
    tm = _pick_tile(s, 512)
    y, gp_tail = _ffn(x1, lw, tm=tm, tf=512, alpha=alpha, seq_len=s)
    k_new = proj[:, :, K_COL:V_COL].reshape(b, s, N_KV_HEADS, HEAD_DIM)
    v_new = proj[:, :, V_COL:Z_COL].reshape(b, s, N_KV_HEADS, HEAD_DIM)
    shift_new = proj[:, s - 1, Z_COL:Z_COL + RWKV_PROJ]
    conv_new = gp_tail.reshape(b, s // tm, SUBLANES, d_ff)[:, -1, SUBLANES - (CONV_W - 1):, :]
    return y.reshape(b, s, d), k_new, v_new, wkv, shift_new, conv_new


def _decode_layer(x, cache_k, cache_v, page_table, shift0, wkv0, conv0, lw, alpha):
    bd, t, d = x.shape
    m = bd * t
    past = page_table.shape[1] * PAGE_SIZE
    pos = past + (jnp.arange(m, dtype=jnp.int32) % t)
    (proj,) = _in_proj(x.reshape(1, m, d), lw["w_in"], pos, tm=m, attn_extras=False)
    proj = proj.reshape(bd, t, PROJ_W)
    att = _moba_decode(proj, cache_k, cache_v, page_table, t)
    grp = _pick_tile(bd, RWKV_SEQS_PER_STEP)
    proj_pad = jnp.pad(proj, ((0, 0), (0, RWKV_CHUNK - t), (0, 0))).reshape(bd // grp, grp * RWKV_CHUNK, PROJ_W)
    shift_pad = jnp.pad(shift0, ((0, 0), (0, Z_PAD - RWKV_PROJ))).reshape(bd // grp, grp, Z_PAD)
    rw, wkv = _rwkv(proj_pad, shift_pad, wkv0, lw, tblk=grp * RWKV_CHUNK, t_real=t, independent=True)
    rw = rw.reshape(bd, RWKV_CHUNK, RWKV_WIDTH)[:, :t]
    x1 = _out_ln(att.reshape(m, ATT_WIDTH), rw.reshape(m, RWKV_WIDTH), x.reshape(m, d), lw, tm=m, alpha=alpha)
    d_ff = conv0.shape[-1]
    zeros = lambda n: jnp.zeros((bd, n, d_ff), F32)
    h1 = jnp.concatenate([conv0[:, 1:2], zeros(t - 1)], axis=1).reshape(m, d_ff)
    h2 = jnp.concatenate([conv0[:, 0:1], conv0[:, 1:2], zeros(t - 2)], axis=1).reshape(m, d_ff)
    y, gp = _ffn(x1, lw, tm=m, tf=512, alpha=alpha, hist=(h1, h2), dec_seq=t)
    k_new = proj[:, :, K_COL:V_COL].reshape(bd, t, N_KV_HEADS, HEAD_DIM)
    v_new = proj[:, :, V_COL:Z_COL].reshape(bd, t, N_KV_HEADS, HEAD_DIM)
    shift_new = proj[:, t - 1, Z_COL:Z_COL + RWKV_PROJ]
    conv_new = gp.reshape(bd, t, d_ff)[:, t - (CONV_W - 1):]
    return y.reshape(bd, t, d), k_new, v_new, wkv, shift_new, conv_new


def kernel(x_prompt, x_sample, cache_k, cache_v, page_table, state_wkv, state_shift, state_conv,
           w_in, w_out, shift_mu, w0, w2, a0, a2, g2, k_k, k_a, r_k, gn_g, gn_b,
           ln1_g, ln1_b, ln2_g, ln2_b, w_up, conv_w, conv_b, w_down):
    depth = w_in.shape[0]
    alpha = (2.0 * depth) ** 0.25
    assert x_sample.shape[1] >= CONV_W - 1
    yp, ys = x_prompt, x_sample
    outs_p, outs_s = [], []
    for l in range(depth):
        lw = _prep_layer(w_in[l], w_out[l], shift_mu[l], w0[l], w2[l], a0[l], a2[l], g2[l], k_k[l], k_a[l],
                         r_k[l], gn_g[l], gn_b[l], ln1_g[l], ln1_b[l], ln2_g[l], ln2_b[l],
                         w_up[l], conv_w[l], conv_b[l], w_down[l])
        yp, *rest_p = _prompt_layer(yp, lw, alpha)
        outs_p.append(rest_p)
        ys, *rest_s = _decode_layer(ys, cache_k[l], cache_v[l], page_table, state_shift[l], state_wkv[l],
                                    state_conv[l], lw, alpha)
        outs_s.append(rest_s)
    stack = lambda outs, n: jnp.stack([o[n] for o in outs])
    return (yp, ys,
            stack(outs_p, 0), stack(outs_p, 1), stack(outs_p, 2), stack(outs_p, 3), stack(outs_p, 4),
            stack(outs_s, 0), stack(outs_s, 1), stack(outs_s, 2), stack(outs_s, 3), stack(outs_s, 4))
```

```python
import functools

import jax
import jax.numpy as jnp
from jax import lax
from jax.experimental import pallas as pl
from jax.experimental.pallas import tpu as pltpu

F32 = jnp.float32
BF16 = jnp.bfloat16

HEAD_DIM = 64
N_ATT_HEADS = 16
N_KV_HEADS = 4
KV_GROUP = N_ATT_HEADS // N_KV_HEADS
ATT_WIDTH = N_ATT_HEADS * HEAD_DIM
KV_WIDTH = N_KV_HEADS * HEAD_DIM
N_RWKV_HEADS = 16
RWKV_WIDTH = N_RWKV_HEADS * HEAD_DIM
ROPE_DIM = HEAD_DIM // 4
ROPE_THETA = 500000.0
MOBA_BLOCK = 256
MOBA_TOPK = 3
D_DECAY_LORA = 64
D_AAA_LORA = 64
D_GATE_LORA = 160
LORA_WIDTH = D_DECAY_LORA + D_AAA_LORA + D_GATE_LORA
RWKV_PROJ = 3 * RWKV_WIDTH + LORA_WIDTH
CONV_W = 3
LN_EPS = 1e-5
GN_EPS = 64e-5
NEG_INF = -1e30
LOG2_E = 1.4426950408889634
PAGE_SIZE = 128

LANES = 128
SUBLANES = 8
VMEM_LIMIT_BYTES = 56 * 1024 * 1024

PROJ_TN = 512
Z_PAD = 3584
Q_COL = 0
K_COL = Q_COL + ATT_WIDTH
V_COL = K_COL + KV_WIDTH
Z_COL = V_COL + KV_WIDTH
PROJ_W = Z_COL + Z_PAD
LORA_COL = 3 * RWKV_WIDTH
LORA_BLOCK = 384
GATE_LORA_PAD = 256

RWKV_CHUNK = HEAD_DIM
RWKV_SEQS_PER_STEP = 8
RWKV_STAGE_GROUP = 8
PAGES_PER_STEP = 128

_NN = (((1,), (0,)), ((), ()))
_NT = (((1,), (1,)), ((), ()))


def _dot(a, b, dims=_NN):
    return lax.dot_general(a, b, dims, preferred_element_type=F32)


def _split_bf16(x):
    hi = x.astype(BF16)
    lo = (x - hi.astype(F32)).astype(BF16)
    return hi, lo


def _dot3(a, b, dims=_NN):
    ah, al = _split_bf16(a)
    bh, bl = _split_bf16(b)
    return _dot(ah, bh, dims) + (_dot(ah, bl, dims) + _dot(al, bh, dims))


def _sigmoid(x):
    return 1.0 / (1.0 + jnp.exp(-x))


def _layer_norm(x, g, b):
    mu = jnp.mean(x, axis=-1, keepdims=True)
    d = x - mu
    var = jnp.mean(d * d, axis=-1, keepdims=True)
    return d * lax.rsqrt(var + LN_EPS) * g + b


def _params(sem):
    return pltpu.CompilerParams(dimension_semantics=sem, vmem_limit_bytes=VMEM_LIMIT_BYTES)


def _rope_tables(pos):
    half = ROPE_DIM // 2
    inv = jnp.power(ROPE_THETA, -(jnp.arange(half, dtype=F32) * 2.0 / ROPE_DIM))
    ang = pos.astype(F32)[:, None] * inv[None, :]
    cos = jnp.cos(ang)
    sin = jnp.sin(ang)
    t = pos.shape[0]
    one = jnp.ones((t, HEAD_DIM - ROPE_DIM), F32)
    z8 = jnp.zeros((t, half), F32)
    zr = jnp.zeros((t, HEAD_DIM - ROPE_DIM), F32)
    c64 = jnp.concatenate([cos, cos, one], axis=1)
    a64 = jnp.concatenate([-sin, z8, zr], axis=1)
    b64 = jnp.concatenate([z8, sin, zr], axis=1)
    rep = LANES // HEAD_DIM
    return tuple(jnp.tile(m, (1, rep)) for m in (c64, a64, b64))


def _proj_block(step):
    n_blocks = PROJ_W // PROJ_TN
    return (step + Z_COL // PROJ_TN) % n_blocks


def _in_proj_kernel(x_ref, w_ref, cos_ref, sa_ref, sb_ref, proj_ref, *rest, attn_extras):
    if attn_extras:
        kh_ref, vt_ref, xb_ref = rest
    else:
        (xb_ref,) = rest
    n_q = ATT_WIDTH // PROJ_TN
    half = ROPE_DIM // 2

    @pl.when(pl.program_id(2) == 0)
    def _():
        xb_ref[...] = x_ref[0].astype(BF16)

    j = _proj_block(pl.program_id(2))

    acc = _dot(xb_ref[...], w_ref[...])

    def rope(slab):
        return (slab * cos_ref[...] + pltpu.roll(slab, LANES - half, 1) * sa_ref[...]
                + pltpu.roll(slab, half, 1) * sb_ref[...])

    @pl.when(j > n_q)
    def _():
        proj_ref[0] = acc

    @pl.when(j < n_q)
    def _():
        for s in range(PROJ_TN // LANES):
            proj_ref[0, :, s * LANES:(s + 1) * LANES] = rope(acc[:, s * LANES:(s + 1) * LANES])

    @pl.when(j == n_q)
    def _():
        k_slabs = []
        for s in range(KV_WIDTH // LANES):
            ks = rope(acc[:, s * LANES:(s + 1) * LANES])
            proj_ref[0, :, s * LANES:(s + 1) * LANES] = ks
            k_slabs.append(ks)
        v = acc[:, KV_WIDTH:]
        proj_ref[0, :, KV_WIDTH:] = v
        if attn_extras:
            k = jnp.concatenate(k_slabs, axis=1)
            tm = acc.shape[0]
            row_blk = (pl.program_id(1) * tm + lax.broadcasted_iota(jnp.int32, (tm, HEAD_DIM), 0)) // MOBA_BLOCK
            one_hot = (row_blk == lax.broadcasted_iota(jnp.int32, (tm, HEAD_DIM), 1)).astype(BF16)
            for c in range(N_KV_HEADS):
                kh_ref[0, c] = jnp.concatenate(
                    [k[:, c * HEAD_DIM:(c + 1) * HEAD_DIM].astype(BF16), one_hot], axis=1)
            vt_ref[0] = v.T.astype(BF16)


def _in_proj(x, w_b, pos, *, tm, attn_extras):
    b, s, d = x.shape
    assert s % tm == 0 and PROJ_W % PROJ_TN == 0
    cos, sa, sb = _rope_tables(pos)
    tab_spec = pl.BlockSpec((tm, LANES), lambda bi, i, j: (i, 0))
    out_shape = [jax.ShapeDtypeStruct((b, s, PROJ_W), F32)]
    out_specs = [pl.BlockSpec((1, tm, PROJ_TN), lambda bi, i, j: (bi, i, _proj_block(j)))]
    if attn_extras:
        assert s // MOBA_BLOCK <= HEAD_DIM
        out_shape += [jax.ShapeDtypeStruct((b, N_KV_HEADS, s, 2 * HEAD_DIM), BF16),
                      jax.ShapeDtypeStruct((b, KV_WIDTH, s), BF16)]
        out_specs += [pl.BlockSpec((1, N_KV_HEADS, tm, 2 * HEAD_DIM), lambda bi, i, j: (bi, 0, i, 0)),
                      pl.BlockSpec((1, KV_WIDTH, tm), lambda bi, i, j: (bi, 0, i))]
    return pl.pallas_call(
        functools.partial(_in_proj_kernel, attn_extras=attn_extras),
        grid=(b, s // tm, PROJ_W // PROJ_TN),
        in_specs=[pl.BlockSpec((1, tm, d), lambda bi, i, j: (bi, i, 0)),
                  pl.BlockSpec((d, PROJ_TN), lambda bi, i, j: (0, _proj_block(j))),
                  tab_spec, tab_spec, tab_spec],
        out_specs=out_specs,
        out_shape=out_shape,
        scratch_shapes=[pltpu.VMEM((tm, d), BF16)],
        compiler_params=_params(("parallel", "parallel", "arbitrary")),
        name="in_proj",
    )(x, w_b, cos, sa, sb)


def _moba_prompt_kernel(q_ref, kh_ref, vt_ref, o_ref, kmean_ref, *, n_blk, topk):
    i = pl.program_id(2)
    blk = MOBA_BLOCK
    n_cand = n_blk - 1
    scale = HEAD_DIM ** -0.5

    @pl.when(i == 0)
    def _():
        kmean_ref[...] = jnp.zeros_like(kmean_ref)
        for n in range(n_blk):
            kmean_ref[n:n + 1, :] = jnp.mean(
                kh_ref[0, 0, n * blk:(n + 1) * blk, :].astype(F32), axis=0, keepdims=True)

    q_t = q_ref[0].T
    qs_t = jnp.concatenate([q_t[g * HEAD_DIM:(g + 1) * HEAD_DIM, :] for g in range(KV_GROUP)],
                           axis=1)
    nq = KV_GROUP * blk

    gate = _dot3(kmean_ref[...], jnp.concatenate([qs_t, jnp.zeros((HEAD_DIM, nq), F32)], axis=0))
    n_idx = lax.broadcasted_iota(jnp.int32, gate.shape, 0)
    gate = jnp.where(n_idx < i, gate, NEG_INF)
    rank = jnp.zeros(gate.shape, F32)
    for m in range(n_cand):
        gm = gate[m:m + 1, :]
        beats = (gm > gate) | ((gm == gate) & (m < n_idx))
        rank = rank + beats.astype(F32)
    blk_bias = jnp.where(((rank < topk) & (n_idx < i)) | (n_idx == i), 0.0, NEG_INF)
    q_aug = jnp.concatenate(
        [qs_t * (scale * LOG2_E), blk_bias, jnp.zeros((HEAD_DIM - blk_bias.shape[0], nq), F32)],
        axis=0).astype(BF16)

    kpos = lax.broadcasted_iota(jnp.int32, (blk, nq), 0)
    qpos = lax.broadcasted_iota(jnp.int32, (blk, nq), 1) & (blk - 1)
    causal_bias = jnp.where(kpos <= qpos, 0.0, NEG_INF)

    def attend(nb):
        s = _dot(kh_ref[0, 0, 0:nb * blk, :], q_aug)
        own = s[(nb - 1) * blk:, :] + causal_bias
        s = jnp.concatenate([s[:(nb - 1) * blk, :], own], axis=0) if nb > 1 else own
        p = jnp.exp2(s - jnp.max(s, axis=0, keepdims=True))
        l = jnp.sum(p, axis=0, keepdims=True)
        o = _dot(vt_ref[0, :, 0:nb * blk], p.astype(BF16)) / l
        o4 = jnp.concatenate([o[:, g * blk:(g + 1) * blk] for g in range(KV_GROUP)], axis=0)
        o_ref[0] = o4.T.astype(o_ref.dtype)

    for nb in range(1, n_blk + 1):
        pl.when(i == nb - 1)(functools.partial(attend, nb))


def _moba_prompt(proj, kh, vt):
    b, s, _ = proj.shape
    assert s % MOBA_BLOCK == 0
    n_blk = s // MOBA_BLOCK
    topk = min(MOBA_TOPK, n_blk - 1)
    gw = KV_GROUP * HEAD_DIM
    n_rows = -(-n_blk // SUBLANES) * SUBLANES
    assert n_rows <= HEAD_DIM
    return pl.pallas_call(
        functools.partial(_moba_prompt_kernel, n_blk=n_blk, topk=topk),
        grid=(b, N_KV_HEADS, n_blk),
        in_specs=[pl.BlockSpec((1, MOBA_BLOCK, gw), lambda bi, c, i: (bi, i, Q_COL // gw + c)),
                  pl.BlockSpec((1, 1, s, 2 * HEAD_DIM), lambda bi, c, i: (bi, c, 0, 0)),
                  pl.BlockSpec((1, HEAD_DIM, s), lambda bi, c, i: (bi, c, 0))],
        out_specs=pl.BlockSpec((1, MOBA_BLOCK, gw), lambda bi, c, i: (bi, i, c)),
        out_shape=jax.ShapeDtypeStruct((b, s, ATT_WIDTH), BF16),
        scratch_shapes=[pltpu.VMEM((n_rows, 2 * HEAD_DIM), F32)],
        compiler_params=_params(("parallel", "parallel", "arbitrary")),
        name="moba_prompt",
    )(proj, kh, vt)


def _moba_decode_scores_kernel(pt_ref, wt_ref, *rest, n_new):
    del pt_ref
    r_pages = PAGES_PER_STEP
    k_refs = rest[:r_pages]
    s_ref, gate_ref, bmax_ref = rest[r_pages:]
    j = pl.program_id(1)
    scale = HEAD_DIM ** -0.5
    nq = N_ATT_HEADS * n_new
    pages_per_blk = MOBA_BLOCK // PAGE_SIZE
    blks_per_step = r_pages // pages_per_blk
    lane = lax.broadcasted_iota(jnp.int32, gate_ref.shape[1:], 1)

    @pl.when(j == 0)
    def _():
        gate_ref[0] = jnp.zeros(gate_ref.shape[1:], F32)
        bmax_ref[0] = jnp.full(bmax_ref.shape[1:], NEG_INF, F32)

    wt2 = jnp.concatenate(_split_bf16(wt_ref[0]), axis=0)
    gate = gate_ref[0]
    bmax = bmax_ref[0]
    for r in range(r_pages):
        s2 = _dot(wt2, k_refs[r][0].astype(BF16))
        raw = s2[:nq] + s2[nq:]
        s_ref[0, :, r * PAGE_SIZE:(r + 1) * PAGE_SIZE] = raw * scale
        first = r % pages_per_blk == 0
        blk_sum = raw if first else blk_sum + raw
        blk_max = raw if first else jnp.maximum(blk_max, raw)
        if r % pages_per_blk == pages_per_blk - 1:
            here = lane == j * blks_per_step + r // pages_per_blk
            gate = jnp.where(here, jnp.sum(blk_sum, axis=1, keepdims=True), gate)
            bmax = jnp.where(here, jnp.max(blk_max, axis=1, keepdims=True) * scale, bmax)
    gate_ref[0] = gate
    bmax_ref[0] = bmax


def _moba_decode_values_kernel(pt_ref, wt_ref, kvn_ref, gate_ref, bmax_ref, s_ref, *rest,
                               n_cand, n_new):
    del pt_ref
    r_pages = PAGES_PER_STEP
    v_refs = rest[:r_pages]
    o_ref = rest[r_pages]
    sel_ref, acc_ref, l_ref, m_ref = rest[r_pages + 1:]
    n_cand_pad = gate_ref.shape[2]
    j = pl.program_id(1)
    scale = HEAD_DIM ** -0.5
    pages_per_blk = MOBA_BLOCK // PAGE_SIZE
    blks_per_step = r_pages // pages_per_blk

    @pl.when(j == 0)
    def _():
        lane = lax.broadcasted_iota(jnp.int32, gate_ref.shape[1:], 1)
        lane_f = lane.astype(F32)
        gate = jnp.where(lane < n_cand, gate_ref[0] * (1.0 / MOBA_BLOCK), NEG_INF)
        sel = lane < 0
        for _ in range(min(MOBA_TOPK, n_cand)):
            top = jnp.max(gate, axis=1, keepdims=True)
            first_top = jnp.min(jnp.where(gate == top, lane_f, float(n_cand_pad)), axis=1, keepdims=True)
            pick = lane_f == first_top
            sel = sel | pick
            gate = jnp.where(pick, -jnp.inf, gate)
        sel = sel & (lane < n_cand)
        sel_ref[...] = sel.astype(F32)

        kvn = kvn_ref[0]
        k_new = kvn[:, :KV_WIDTH].astype(BF16)
        v_new = kvn[:, KV_WIDTH:].astype(BF16)
        s_new = _dot((wt_ref[0] * scale).astype(BF16), k_new, _NT)
        t_q = lax.broadcasted_iota(jnp.int32, s_new.shape, 0) % n_new
        t_k = lax.broadcasted_iota(jnp.int32, s_new.shape, 1)
        ok_new = (t_k <= t_q) & (t_k < n_new)
        m = jnp.maximum(jnp.max(jnp.where(ok_new, s_new, NEG_INF), axis=1, keepdims=True),
                        jnp.max(jnp.where(sel, bmax_ref[0], NEG_INF), axis=1, keepdims=True))
        p_new = jnp.where(ok_new, jnp.exp(s_new - m), 0.0)
        m_ref[...] = jnp.broadcast_to(m, m_ref.shape)
        l_lane = lax.broadcasted_iota(jnp.int32, l_ref.shape, 1)
        l_ref[...] = jnp.where(l_lane == 0, jnp.sum(p_new, axis=1, keepdims=True), 0.0)
        acc_ref[...] = _dot(p_new.astype(BF16), v_new)

    src = lax.broadcasted_iota(jnp.int32, (n_cand_pad, LANES), 0)
    dst = lax.broadcasted_iota(jnp.int32, (n_cand_pad, LANES), 1)
    onehot = ((src == j * blks_per_step + dst) & (dst < blks_per_step)).astype(BF16)
    sel_here = _dot(sel_ref[...].astype(BF16), onehot)
    m = m_ref[:, 0:1]
    acc = acc_ref[...]
    l_part = l_ref[...]
    for r in range(r_pages):
        blk = r // pages_per_blk
        sb = s_ref[0, :, r * PAGE_SIZE:(r + 1) * PAGE_SIZE]
        pb = jnp.where(sel_here[:, blk:blk + 1] > 0.5, jnp.exp(sb - m), 0.0)
        l_part = l_part + pb
        acc = acc + _dot(pb.astype(BF16), v_refs[r][0].astype(BF16), _NT)
    acc_ref[...] = acc
    l_ref[...] = l_part

    @pl.when(j == pl.num_programs(1) - 1)
    def _():
        o_ref[0] = acc / jnp.sum(l_part, axis=1, keepdims=True)


def _moba_decode(proj_d, cache_k, cache_v, page_table, n_new):
    bd = proj_d.shape[0]
    n_pages = page_table.shape[1]
    past = n_pages * PAGE_SIZE
    assert past % MOBA_BLOCK == 0 and n_pages % PAGES_PER_STEP == 0 and n_new <= SUBLANES
    n_cand = past // MOBA_BLOCK
    n_steps = n_pages // PAGES_PER_STEP
    n_phys = cache_k.shape[0]
    nq = N_ATT_HEADS * n_new
    ck = jnp.transpose(cache_k, (0, 2, 3, 1)).reshape(n_phys, KV_WIDTH, PAGE_SIZE)
    cv = jnp.transpose(cache_v, (0, 2, 3, 1)).reshape(n_phys, KV_WIDTH, PAGE_SIZE)

    q = proj_d[:, :, Q_COL:Q_COL + ATT_WIDTH].reshape(bd, n_new, N_KV_HEADS, KV_GROUP, HEAD_DIM)
    q = jnp.transpose(q, (0, 2, 3, 1, 4)).reshape(bd, N_KV_HEADS, KV_GROUP * n_new, 1, HEAD_DIM)
    eye = jnp.eye(N_KV_HEADS, dtype=F32)[None, :, None, :, None]
    wt = (q * eye).reshape(bd, nq, KV_WIDTH)
    kvn = jnp.pad(proj_d[:, :, K_COL:K_COL + 2 * KV_WIDTH], ((0, 0), (0, SUBLANES - n_new), (0, 0)))

    def page_map(r):
        return lambda b, j, pt: (pt[b, j * PAGES_PER_STEP + r], 0, 0)

    step_keys = PAGES_PER_STEP * PAGE_SIZE
    n_cand_pad = -(-n_cand // LANES) * LANES
    per_seq = lambda w: pl.BlockSpec((1, nq, w), lambda b, j, pt: (b, 0, 0))
    page_specs = [pl.BlockSpec((1, KV_WIDTH, PAGE_SIZE), page_map(r)) for r in range(PAGES_PER_STEP)]
    score_spec = pl.BlockSpec((1, nq, step_keys), lambda b, j, pt: (b, 0, j))
    scores, gate, bmax = pl.pallas_call(
        functools.partial(_moba_decode_scores_kernel, n_new=n_new),
        grid_spec=pltpu.PrefetchScalarGridSpec(
            num_scalar_prefetch=1,
            grid=(bd, n_steps),
            in_specs=[per_seq(KV_WIDTH)] + page_specs,
            out_specs=[score_spec, per_seq(n_cand_pad), per_seq(n_cand_pad)]),
        out_shape=[jax.ShapeDtypeStruct((bd, nq, past), F32),
                   jax.ShapeDtypeStruct((bd, nq, n_cand_pad), F32),
                   jax.ShapeDtypeStruct((bd, nq, n_cand_pad), F32)],
        compiler_params=_params(("parallel", "arbitrary")),
        name="moba_decode_scores",
    )(page_table, wt, *([ck] * PAGES_PER_STEP))
    out = pl.pallas_call(
        functools.partial(_moba_decode_values_kernel, n_cand=n_cand, n_new=n_new),
        grid_spec=pltpu.PrefetchScalarGridSpec(
            num_scalar_prefetch=1,
            grid=(bd, n_steps),
            in_specs=[per_seq(KV_WIDTH),
                      pl.BlockSpec((1, SUBLANES, 2 * KV_WIDTH), lambda b, j, pt: (b, 0, 0)),
                      per_seq(n_cand_pad), per_seq(n_cand_pad), score_spec] + page_specs,
            out_specs=per_seq(KV_WIDTH),
            scratch_shapes=[pltpu.VMEM((nq, n_cand_pad), F32),
                            pltpu.VMEM((nq, KV_WIDTH), F32),
                            pltpu.VMEM((nq, LANES), F32),
                            pltpu.VMEM((nq, LANES), F32)]),
        out_shape=jax.ShapeDtypeStruct((bd, nq, KV_WIDTH), F32),
        compiler_params=_params(("parallel", "arbitrary")),
        name="moba_decode_values",
    )(page_table, wt, kvn, gate, bmax, scores, *([cv] * PAGES_PER_STEP))
    o = out.reshape(bd, N_KV_HEADS, KV_GROUP, n_new, N_KV_HEADS, HEAD_DIM)
    o = jnp.stack([o[:, c, :, :, c, :] for c in range(N_KV_HEADS)], axis=1)
    o = jnp.transpose(o, (0, 3, 1, 2, 4)).reshape(bd, n_new, ATT_WIDTH)
    return o.astype(BF16)


def _rwkv_kernel(zr_ref, zk_ref, zv_ref, zl_ref, sr_ref, sk_ref, sv_ref, sl_ref,
                 mu_ref, mul_ref, par_ref, w2_ref, a2_ref, g2_ref, s0_ref,
                 rw_ref, wkv_ref,
                 state_ref, pr_ref, pk_ref, pv_ref, plo_ref, *, t_real, independent):
    tb = pl.program_id(2)
    n_tb = pl.num_programs(2)
    tblk = zr_ref.shape[1]
    c_len = RWKV_CHUNK
    width = LANES
    lane = lax.broadcasted_iota(jnp.int32, (1, width), 1)
    head_a = lane < HEAD_DIM
    row = lax.broadcasted_iota(jnp.int32, (tblk, 1), 0)

    def block_diag(sa, sb):
        z = jnp.zeros((HEAD_DIM, HEAD_DIM), F32)
        return jnp.concatenate([jnp.concatenate([sa, z], axis=1),
                                jnp.concatenate([z, sb], axis=1)], axis=0)

    if independent:
        def token_shift(z, first_ref, prev_ref, mu):
            z_prev = pltpu.roll(z, 1, 0)
            first = first_ref[0]
            for c in range(tblk // c_len):
                z_prev = jnp.where(row == c * c_len, first[c:c + 1, :], z_prev)
            return z + (z_prev - z) * mu
    else:
        @pl.when(tb == 0)
        def _():
            state_ref[...] = block_diag(s0_ref[0, 0], s0_ref[0, 1])
            pr_ref[0:1, :] = sr_ref[0]
            pk_ref[0:1, :] = sk_ref[0]
            pv_ref[0:1, :] = sv_ref[0]
            plo_ref[0:1, :] = sl_ref[0]

        def token_shift(z, first_ref, prev_ref, mu):
            z_prev = jnp.where(row == 0, prev_ref[0:1, :], pltpu.roll(z, 1, 0))
            prev_ref[0:1, :] = z[tblk - 1:tblk, :]
            return z + (z_prev - z) * mu

    def seg_sum(x):
        sa = jnp.sum(jnp.where(head_a, x, 0.0), axis=1, keepdims=True)
        sb = jnp.sum(jnp.where(head_a, 0.0, x), axis=1, keepdims=True)
        return jnp.where(head_a, sa, sb)

    r = token_shift(zr_ref[0], sr_ref, pr_ref, mu_ref[0:1, :])
    k = token_shift(zk_ref[0], sk_ref, pk_ref, mu_ref[1:2, :])
    v = token_shift(zv_ref[0], sv_ref, pv_ref, mu_ref[2:3, :])
    lo = token_shift(zl_ref[0], sl_ref, plo_ref, mul_ref[...])
    par = par_ref[...]
    w0, a0, k_k, k_a, r_k, gn_g, gn_b = (par[n:n + 1, :] for n in range(7))

    lo_wa = lo[:, :LANES]
    nx = -(w0 + _dot(jnp.tanh(lo_wa).astype(BF16), w2_ref[...]))
    softplus = jnp.maximum(nx, 0.0) + jnp.log(1.0 + jnp.exp(-jnp.abs(nx)))
    logw = -jnp.exp(-softplus - 0.5)
    a = _sigmoid(a0 + _dot(lo_wa.astype(BF16), a2_ref[...]))
    g = _dot(_sigmoid(lo[:, LANES:]).astype(BF16), g2_ref[...])
    kk = k * k_k
    kk = kk / jnp.maximum(jnp.sqrt(seg_sum(kk * kk)), 1e-12)
    k = k * (1.0 + (a - 1.0) * k_a)
    bonus = seg_sum(r * k * r_k) * v

    if t_real is not None:
        valid = ((row & (c_len - 1)) if independent else (tb * tblk + row)) < t_real
        logw = jnp.where(valid, logw, 0.0)
        kk = jnp.where(valid, kk, 0.0)
        k = jnp.where(valid, k, 0.0)
        v_in = jnp.where(valid, v, 0.0)
    else:
        v_in = v

    def pair_rows(x):
        return jnp.concatenate([jnp.where(head_a, x, 0.0), jnp.where(head_a, 0.0, x)], axis=0)

    ri = lax.broadcasted_iota(jnp.int32, (c_len, 2 * c_len), 0)
    ci = lax.broadcasted_iota(jnp.int32, (c_len, 2 * c_len), 1)
    first = ci < c_len
    tcol = ci & (c_len - 1)
    strict = tcol < ri
    incl = tcol <= ri
    eye2 = (tcol == ri).astype(F32)
    bi = lax.broadcasted_iota(jnp.int32, (width, width), 0) < HEAD_DIM
    bj = lax.broadcasted_iota(jnp.int32, (width, width), 1) < HEAD_DIM
    same_head = bi == bj
    live = c_len if t_real is None else min(t_real, c_len)
    n_levels = max((live - 1).bit_length() - 1, 0)
    shift = c_len.bit_length() - 1
    chunks = [slice(c * c_len, (c + 1) * c_len) for c in range(tblk // c_len)]
    bf = lambda x: x.astype(BF16)

    tr = lax.broadcasted_iota(jnp.int32, (tblk, tblk), 0)
    tc = lax.broadcasted_iota(jnp.int32, (tblk, tblk), 1)
    tri = ((tr >= tc) & ((tr >> shift) == (tc >> shift))).astype(BF16)
    lh = logw.astype(BF16)
    l1 = logw - lh.astype(F32)
    lm = l1.astype(BF16)
    ll = (l1 - lm.astype(F32)).astype(BF16)
    lc = _dot(tri, lh) + (_dot(tri, lm) + _dot(tri, ll))
    g_in = jnp.exp(lc)
    g_inv = jnp.exp(-lc)
    a_all = -kk * jnp.exp(lc - logw)
    b_all = kk * a * g_inv
    k_all = k * g_inv
    r_all = r * g_in

    def group_terms(sls):
        a_t = [a_all[sl] for sl in sls]
        r_t = [r_all[sl] for sl in sls]
        b_t = [b_all[sl] for sl in sls]
        k_t = [k_all[sl] for sl in sls]
        v_c = [v_in[sl] for sl in sls]
        g_end = [g_in[sl.stop - 1:sl.stop, :] for sl in sls]
        a_pair = [pair_rows(bf(x)) for x in a_t]
        s4 = [_dot(jnp.concatenate([ap, pair_rows(bf(rt))], axis=0),
                   bf(jnp.concatenate([bt, kt], axis=0)), _NT)
              for ap, rt, bt, kt in zip(a_pair, r_t, b_t, k_t)]
        m_ab, m_ak, m_rb, m_rk = [], [], [], []
        for x in s4:
            sa_, sb_ = x[0:c_len], x[c_len:2 * c_len]
            ra_, rb_ = x[2 * c_len:3 * c_len], x[3 * c_len:4 * c_len]
            m_ab.append(jnp.where(strict, jnp.where(first, sa_, pltpu.roll(sb_, c_len, 1)), 0.0))
            m_ak.append(jnp.where(strict, jnp.where(first, pltpu.roll(sa_, c_len, 1), sb_), 0.0))
            m_rb.append(jnp.where(incl, jnp.where(first, ra_, pltpu.roll(rb_, c_len, 1)), 0.0))
            m_rk.append(jnp.where(incl, jnp.where(first, pltpu.roll(ra_, c_len, 1), rb_), 0.0))

        t_inv = [m + eye2 for m in m_ab]
        if n_levels > 0:
            l_pow = [bf(m) for m in m_ab]
            l_pow = [bf(_dot(lp, pair_rows(lp))) for lp in l_pow]
            for level in range(n_levels):
                qp = [_dot(lp, jnp.concatenate([pair_rows(lp), pair_rows(bf(ti))], axis=1))
                      for lp, ti in zip(l_pow, t_inv)]
                t_inv = [ti + x[:, width:] for ti, x in zip(t_inv, qp)]
                if level + 1 < n_levels:
                    l_pow = [bf(x[:, :width]) for x in qp]

        v_pair = [pair_rows(bf(x)) for x in v_c]
        mak_v = [_dot(bf(m), vp) for m, vp in zip(m_ak, v_pair)]
        tw = [_dot(bf(ti), jnp.concatenate([ap, pair_rows(bf(mv))], axis=1))
              for ti, ap, mv in zip(t_inv, a_pair, mak_v)]
        w_t = [x[:, :width] for x in tw]
        v_hat = [x[:, width:] for x in tw]
        mw = [_dot(bf(jnp.concatenate([mb, mk], axis=1)),
                   jnp.concatenate(
                       [jnp.concatenate([pair_rows(bf(w)), pair_rows(bf(vh))], axis=1),
                        jnp.concatenate([jnp.zeros_like(vp), vp], axis=1)], axis=0))
              for mb, mk, w, vh, vp in zip(m_rb, m_rk, w_t, v_hat, v_pair)]
        p_m = [rt + x[:, :width] for rt, x in zip(r_t, mw)]
        q_m = [x[:, width:] for x in mw]
        phi, psi = [], []
        for w, vh, vc, bt, kt, ge in zip(w_t, v_hat, v_c, b_t, k_t, g_end):
            bk = bf(jnp.concatenate([bt * ge, kt * ge], axis=0))
            left = jnp.concatenate(
                [jnp.concatenate([w, jnp.zeros_like(w)], axis=0),
                 jnp.concatenate([vh, vc], axis=0)], axis=1)
            pp = _dot(bf(left.T), bk)
            phi.append(bf(jnp.where(same_head, pp[:width], 0.0)))
            psi.append(jnp.where(same_head, pp[width:], 0.0))
        return list(zip(p_m, q_m, phi, psi, g_end))

    def advance(s, term):
        pm, qm, ph, ps, ge = term
        sb16 = bf(s)
        return _dot(bf(pm), sb16, _NT) + qm, s * ge + _dot(sb16, ph) + ps

    outs = []
    s = None if independent else state_ref[...]
    for g0 in range(0, len(chunks), RWKV_STAGE_GROUP):
        for c, term in enumerate(group_terms(chunks[g0:g0 + RWKV_STAGE_GROUP]), start=g0):
            if independent:
                o_c, s_c = advance(block_diag(s0_ref[c, 0], s0_ref[c, 1]), term)
                wkv_ref[c, 0] = s_c[:HEAD_DIM, :HEAD_DIM]
                wkv_ref[c, 1] = s_c[HEAD_DIM:, HEAD_DIM:]
            else:
                o_c, s = advance(s, term)
            outs.append(o_c)
    if not independent:
        state_ref[...] = s

        @pl.when(tb == n_tb - 1)
        def _():
            wkv_ref[0, 0] = s[:HEAD_DIM, :HEAD_DIM]
            wkv_ref[0, 1] = s[HEAD_DIM:, HEAD_DIM:]

    o = jnp.concatenate(outs, axis=0) if len(outs) > 1 else outs[0]
    inv_n = 1.0 / HEAD_DIM
    mu_o = seg_sum(o) * inv_n
    d = o - mu_o
    var = seg_sum(d * d) * inv_n
    on = d * lax.rsqrt(var + GN_EPS) * gn_g + gn_b
    rw_ref[0] = ((on + bonus) * g).astype(rw_ref.dtype)


def _rwkv(proj, shift0, wkv0, lw, *, tblk, t_real, independent=False):
    b, s, _ = proj.shape
    assert s % tblk == 0 and tblk % RWKV_CHUNK == 0
    n_pairs = RWKV_WIDTH // LANES
    rkv = RWKV_WIDTH // LANES
    lora_blk = LORA_COL // LORA_BLOCK
    n_first = shift0.shape[1]
    n_state = tblk // RWKV_CHUNK if independent else 1
    assert (not independent) or (s == tblk and n_first == n_state)

    def z_spec(width, col0):
        return pl.BlockSpec((1, tblk, width), lambda bi, p, t: (bi, t, col0(p)))

    def s_spec(width, col0):
        return pl.BlockSpec((1, n_first, width), lambda bi, p, t: (bi, 0, col0(p)))

    state_spec = pl.BlockSpec((n_state, 2, HEAD_DIM, HEAD_DIM), lambda bi, p, t: (bi, p, 0, 0))

    assert Z_COL % LANES == 0 and (Z_COL + LORA_COL) % LORA_BLOCK == 0
    cols = [lambda p: p, lambda p: rkv + p, lambda p: 2 * rkv + p]
    zc = Z_COL // LANES
    in_specs = ([z_spec(LANES, lambda p, c=c: zc + c(p)) for c in cols]
                + [z_spec(LORA_BLOCK, lambda p: (Z_COL + LORA_COL) // LORA_BLOCK)]
                + [s_spec(LANES, c) for c in cols] + [s_spec(LORA_BLOCK, lambda p: lora_blk)]
                + [pl.BlockSpec((SUBLANES, LANES), lambda bi, p, t: (0, p)),
                   pl.BlockSpec((1, LORA_BLOCK), lambda bi, p, t: (0, 0)),
                   pl.BlockSpec((SUBLANES, LANES), lambda bi, p, t: (0, p)),
                   pl.BlockSpec((LANES, LANES), lambda bi, p, t: (0, p)),
                   pl.BlockSpec((LANES, LANES), lambda bi, p, t: (0, p)),
                   pl.BlockSpec((GATE_LORA_PAD, LANES), lambda bi, p, t: (0, p)),
                   state_spec])
    return pl.pallas_call(
        functools.partial(_rwkv_kernel, t_real=t_real, independent=independent),
        grid=(b, n_pairs, s // tblk),
        in_specs=in_specs,
        out_specs=[pl.BlockSpec((1, tblk, LANES), lambda bi, p, t: (bi, t, p)), state_spec],
        out_shape=[jax.ShapeDtypeStruct((b, s, RWKV_WIDTH), BF16),
                   jax.ShapeDtypeStruct((b * n_state, N_RWKV_HEADS, HEAD_DIM, HEAD_DIM), F32)],
        scratch_shapes=[pltpu.VMEM((LANES, LANES), F32),
                        pltpu.VMEM((SUBLANES, LANES), F32),
                        pltpu.VMEM((SUBLANES, LANES), F32),
                        pltpu.VMEM((SUBLANES, LANES), F32),
                        pltpu.VMEM((SUBLANES, LORA_BLOCK), F32)],
        compiler_params=_params(("parallel", "parallel", "arbitrary")),
        name="rwkv7",
    )(proj, proj, proj, proj, shift0, shift0, shift0, shift0,
      lw["mu_rkv"], lw["mu_lora"], lw["rwkv_par"], lw["w2"], lw["a2"], lw["g2"], wkv0)


def _out_ln_kernel(att_ref, rw_ref, x_ref, wa_ref, wr_ref, g_ref, b_ref, o_ref, *, alpha):
    mix = _dot(att_ref[...], wa_ref[...]) + _dot(rw_ref[...], wr_ref[...])
    o_ref[...] = _layer_norm(alpha * x_ref[...] + mix, g_ref[...], b_ref[...])


def _out_ln(att, rw, x, lw, *, tm, alpha):
    m, d = x.shape
    assert m % tm == 0
    row = lambda w: pl.BlockSpec((tm, w), lambda i: (i, 0))
    full = lambda r, w: pl.BlockSpec((r, w), lambda i: (0, 0))
    return pl.pallas_call(
        functools.partial(_out_ln_kernel, alpha=alpha),
        grid=(m // tm,),
        in_specs=[row(ATT_WIDTH), row(RWKV_WIDTH), row(d),
                  full(ATT_WIDTH, d), full(RWKV_WIDTH, d), full(1, d), full(1, d)],
        out_specs=row(d),
        out_shape=jax.ShapeDtypeStruct((m, d), F32),
        compiler_params=_params(("parallel",)),
        name="out_proj_ln",
    )(att, rw, x, lw["w_out_att"], lw["w_out_rw"], lw["ln1_g"], lw["ln1_b"])


def _ffn_kernel(x_ref, wg_ref, wu_ref, wd_ref, cw_ref, cb_ref, g_ref, b_ref, *rest,
                alpha, tiles_per_seq, dec_seq):
    decode = dec_seq is not None
    if decode:
        h1_ref, h2_ref, y_ref, gp_ref, xb_ref, acc_ref = rest
    else:
        y_ref, gp_ref, xb_ref, acc_ref, tail_ref = rest
    i = pl.program_id(0)
    j = pl.program_id(1)
    tm = x_ref.shape[0]
    keep = SUBLANES

    @pl.when(j == 0)
    def _():
        xb_ref[...] = x_ref[...].astype(BF16)
        acc_ref[...] = jnp.zeros_like(acc_ref)

    xb = xb_ref[...]
    gp = _dot(xb, wg_ref[...])
    up = _dot(xb, wu_ref[...])
    row = lax.broadcasted_iota(jnp.int32, (tm, 1), 0)
    r1 = pltpu.roll(gp, 1, 0)
    r2 = pltpu.roll(gp, 2, 0)
    if decode:
        t = row % dec_seq
        g1 = jnp.where(t >= 1, r1, 0.0) + h1_ref[...]
        g2 = jnp.where(t >= 2, r2, 0.0) + h2_ref[...]
        gp_ref[...] = gp
    else:
        tail = tail_ref[j]
        seq_start = (i % tiles_per_seq) == 0
        t6 = jnp.where(seq_start, 0.0, tail[keep - 2:keep - 1, :])
        t7 = jnp.where(seq_start, 0.0, tail[keep - 1:keep, :])
        g1 = jnp.where(row == 0, t7, r1)
        g2 = jnp.where(row == 0, t6, jnp.where(row == 1, t7, r2))
        tail_ref[j] = gp[tm - keep:, :]
        gp_ref[0] = gp[tm - keep:, :]
    cw = cw_ref[...]
    conv = cw[0:1, :] * g2 + cw[1:2, :] * g1 + cw[2:3, :] * gp + cb_ref[...]
    h = conv * _sigmoid(conv) * up
    acc_ref[...] += _dot(h.astype(BF16), wd_ref[...])

    @pl.when(j == pl.num_programs(1) - 1)
    def _():
        y_ref[...] = _layer_norm(alpha * x_ref[...] + acc_ref[...], g_ref[...], b_ref[...])


def _ffn(x1, lw, *, tm, tf, alpha, seq_len=None, hist=None, dec_seq=None):
    m, d = x1.shape
    d_ff = lw["w_down"].shape[0]
    assert m % tm == 0 and d_ff % tf == 0
    nf = d_ff // tf
    decode = hist is not None
    in_specs = [pl.BlockSpec((tm, d), lambda i, j: (i, 0)),
                pl.BlockSpec((d, tf), lambda i, j: (0, j)),
                pl.BlockSpec((d, tf), lambda i, j: (0, nf + j)),
                pl.BlockSpec((tf, d), lambda i, j: (j, 0)),
                pl.BlockSpec((SUBLANES, tf), lambda i, j: (0, j)),
                pl.BlockSpec((1, tf), lambda i, j: (0, j)),
                pl.BlockSpec((1, d), lambda i, j: (0, 0)),
                pl.BlockSpec((1, d), lambda i, j: (0, 0))]
    args = [x1, lw["w_up"], lw["w_up"], lw["w_down"], lw["conv_w"], lw["conv_b"], lw["ln2_g"], lw["ln2_b"]]
    scratch = [pltpu.VMEM((tm, d), BF16), pltpu.VMEM((tm, d), F32)]
    if decode:
        assert m == tm
        in_specs += [pl.BlockSpec((tm, tf), lambda i, j: (0, j))] * 2
        args += list(hist)
        gp_shape = jax.ShapeDtypeStruct((m, d_ff), F32)
        gp_spec = pl.BlockSpec((tm, tf), lambda i, j: (0, j))
        tiles_per_seq = None
    else:
        assert seq_len % tm == 0 and tm % SUBLANES == 0
        tiles_per_seq = seq_len // tm
        gp_shape = jax.ShapeDtypeStruct((m // tm, SUBLANES, d_ff), F32)
        gp_spec = pl.BlockSpec((1, SUBLANES, tf), lambda i, j: (i, 0, j))
        scratch += [pltpu.VMEM((nf, SUBLANES, tf), F32)]
    return pl.pallas_call(
        functools.partial(_ffn_kernel, alpha=alpha, tiles_per_seq=tiles_per_seq, dec_seq=dec_seq),
        grid=(m // tm, nf),
        in_specs=in_specs,
        out_specs=[pl.BlockSpec((tm, d), lambda i, j: (i, 0)), gp_spec],
        out_shape=[jax.ShapeDtypeStruct((m, d), F32), gp_shape],
        scratch_shapes=scratch,
        compiler_params=_params(("arbitrary", "arbitrary")),
        name="conv_ffn_ln",
    )(*args)


def _prep_layer(w_in, w_out, shift_mu, w0, w2, a0, a2, g2, k_k, k_a, r_k, gn_g, gn_b,
                ln1_g, ln1_b, ln2_g, ln2_b, w_up, conv_w, conv_b, w_down):
    row = lambda v: v.reshape(1, -1)
    zpad = Z_PAD - RWKV_PROJ
    w_in_r = jnp.pad(w_in.astype(BF16), ((0, 0), (0, zpad)))
    mu = jnp.pad(shift_mu, (0, zpad))
    zero_row = jnp.zeros((RWKV_WIDTH,), F32)
    return {
        "w_in": w_in_r,
        "w_out_att": w_out[:ATT_WIDTH].astype(BF16),
        "w_out_rw": w_out[ATT_WIDTH:].astype(BF16),
        "mu_rkv": jnp.pad(mu[:LORA_COL].reshape(3, RWKV_WIDTH), ((0, SUBLANES - 3), (0, 0))),
        "mu_lora": row(mu[LORA_COL:LORA_COL + LORA_BLOCK]),
        "rwkv_par": jnp.stack([w0, a0, k_k, k_a, r_k, gn_g, gn_b, zero_row]),
        "w2": jnp.pad(w2, ((0, LANES - D_DECAY_LORA), (0, 0))).astype(BF16),
        "a2": jnp.pad(a2, ((D_DECAY_LORA, LANES - D_DECAY_LORA - D_AAA_LORA), (0, 0))).astype(BF16),
        "g2": jnp.pad(g2, ((0, GATE_LORA_PAD - D_GATE_LORA), (0, 0))).astype(BF16),
        "ln1_g": row(ln1_g), "ln1_b": row(ln1_b), "ln2_g": row(ln2_g), "ln2_b": row(ln2_b),
        "w_up": w_up.astype(BF16),
        "conv_w": jnp.pad(conv_w, ((0, SUBLANES - CONV_W), (0, 0))),
        "conv_b": row(conv_b),
        "w_down": w_down.astype(BF16),
    }


def _pick_tile(n, cap):
    t = min(n, cap)
    while n % t:
        t //= 2
    return t


def _prompt_layer(x, lw, alpha):
    b, s, d = x.shape
    d_ff = lw["w_down"].shape[0]
    pos = jnp.arange(s, dtype=jnp.int32)
    proj, kh, vt = _in_proj(x, lw["w_in"], pos, tm=_pick_tile(s, 512), attn_extras=True)
    att = _moba_prompt(proj, kh, vt)
    rw, wkv = _rwkv(proj, jnp.zeros((b, 1, Z_PAD), F32),
                    jnp.zeros((b, N_RWKV_HEADS, HEAD_DIM, HEAD_DIM), F32), lw,
                    tblk=_pick_tile(s, 512), t_real=None)
    x1 = _out_ln(att.reshape(b * s, ATT_WIDTH), rw.reshape(b * s, RWKV_WIDTH), x.reshape(b * s, d), lw,
                 tm=_pick_tile(s, 512), alpha=alpha)
    tm = _pick_tile(s, 512)
    y, gp_tail = _ffn(x1, lw, tm=tm, tf=512, alpha=alpha, seq_len=s)
    k_new = proj[:, :, K_COL:V_COL].reshape(b, s, N_KV_HEADS, HEAD_DIM)
    v_new = proj[:, :, V_COL:Z_COL].reshape(b, s, N_KV_HEADS, HEAD_DIM)
    shift_new = proj[:, s - 1, Z_COL:Z_COL + RWKV_PROJ]
    conv_new = gp_tail.reshape(b, s // tm, SUBLANES, d_ff)[:, -1, SUBLANES - (CONV_W - 1):, :]
    return y.reshape(b, s, d), k_new, v_new, wkv, shift_new, conv_new


def _decode_layer(x, cache_k, cache_v, page_table, shift0, wkv0, conv0, lw, alpha):
    bd, t, d = x.shape
    m = bd * t
    past = page_table.shape[1] * PAGE_SIZE
    pos = past + (jnp.arange(m, dtype=jnp.int32) % t)
    (proj,) = _in_proj(x.reshape(1, m, d), lw["w_in"], pos, tm=m, attn_extras=False)
    proj = proj.reshape(bd, t, PROJ_W)
    att = _moba_decode(proj, cache_k, cache_v, page_table, t)
    grp = _pick_tile(bd, RWKV_SEQS_PER_STEP)
    proj_pad = jnp.pad(proj, ((0, 0), (0, RWKV_CHUNK - t), (0, 0))).reshape(bd // grp, grp * RWKV_CHUNK, PROJ_W)
    shift_pad = jnp.pad(shift0, ((0, 0), (0, Z_PAD - RWKV_PROJ))).reshape(bd // grp, grp, Z_PAD)
    rw, wkv = _rwkv(proj_pad, shift_pad, wkv0, lw, tblk=grp * RWKV_CHUNK, t_real=t, independent=True)
    rw = rw.reshape(bd, RWKV_CHUNK, RWKV_WIDTH)[:, :t]
    x1 = _out_ln(att.reshape(m, ATT_WIDTH), rw.reshape(m, RWKV_WIDTH), x.reshape(m, d), lw, tm=m, alpha=alpha)
    d_ff = conv0.shape[-1]
    zeros = lambda n: jnp.zeros((bd, n, d_ff), F32)
    h1 = jnp.concatenate([conv0[:, 1:2], zeros(t - 1)], axis=1).reshape(m, d_ff)
    h2 = jnp.concatenate([conv0[:, 0:1], conv0[:, 1:2], zeros(t - 2)], axis=1).reshape(m, d_ff)
    y, gp = _ffn(x1, lw, tm=m, tf=512, alpha=alpha, hist=(h1, h2), dec_seq=t)
    k_new = proj[:, :, K_COL:V_COL].reshape(bd, t, N_KV_HEADS, HEAD_DIM)
    v_new = proj[:, :, V_COL:Z_COL].reshape(bd, t, N_KV_HEADS, HEAD_DIM)
    shift_new = proj[:, t - 1, Z_COL:Z_COL + RWKV_PROJ]
    conv_new = gp.reshape(bd, t, d_ff)[:, t - (CONV_W - 1):]
    return y.reshape(bd, t, d), k_new, v_new, wkv, shift_new, conv_new


def kernel(x_prompt, x_sample, cache_k, cache_v, page_table, state_wkv, state_shift, state_conv,
           w_in, w_out, shift_mu, w0, w2, a0, a2, g2, k_k, k_a, r_k, gn_g, gn_b,
           ln1_g, ln1_b, ln2_g, ln2_b, w_up, conv_w, conv_b, w_down):
    depth = w_in.shape[0]
    alpha = (2.0 * depth) ** 0.25
    assert x_sample.shape[1] >= CONV_W - 1
    yp, ys = x_prompt, x_sample
    outs_p, outs_s = [], []
    for l in range(depth):
        lw = _prep_layer(w_in[l], w_out[l], shift_mu[l], w0[l], w2[l], a0[l], a2[l], g2[l], k_k[l], k_a[l],
                         r_k[l], gn_g[l], gn_b[l], ln1_g[l], ln1_b[l], ln2_g[l], ln2_b[l],
                         w_up[l], conv_w[l], conv_b[l], w_down[l])
        yp, *rest_p = _prompt_layer(yp, lw, alpha)
        outs_p.append(rest_p)
        ys, *rest_s = _decode_layer(ys, cache_k[l], cache_v[l], page_table, state_shift[l], state_wkv[l],
                                    state_conv[l], lw, alpha)
        outs_s.append(rest_s)
    stack = lambda outs, n: jnp.stack([o[n] for o in outs])
    return (yp, ys,
            stack(outs_p, 0), stack(outs_p, 1), stack(outs_p, 2), stack(outs_p, 3), stack(outs_p, 4),
            stack(outs_s, 0), stack(outs_s, 1), stack(outs_s, 2), stack(outs_s, 3), stack(outs_s, 4))
```
